```python
import math
import jax, jax.numpy as jnp
from jax import lax
import numpy as np

D_MODEL = 1024
BATCH = 32
SEQ = 256
DEPTH = 4
DEC_BATCH = 2
DEC_SEQ = 2048
PAST_LEN = 256

GRID_W = 64
N_MIXERS = 3
N_LAYERS_DA = (DEPTH + 2) // 3
N_LAYERS_RWKV = (DEPTH + 1) // 3
N_LAYERS_SWA = DEPTH // 3
Q_BLOCK = 128
ROPE_BASE = 10000.0
NORM_EPS = 1e-6
D_FF = 4 * D_MODEL
NEG_INF = -1e30
DA_HEAD_DIM = 64
DA_HEADS = D_MODEL // (2 * DA_HEAD_DIM)
RWKV_HEAD = 64
RWKV_HEADS = D_MODEL // RWKV_HEAD
RWKV_DECAY_LORA = 64
RWKV_ICLR_LORA = 64
RWKV_GATE_LORA = 128
RWKV_GN_EPS = 64e-5
SWA_HEAD_DIM = 64
SWA_HEADS = D_MODEL // SWA_HEAD_DIM
SWA_KV_HEADS = SWA_HEADS // 4
SWA_GROUP = SWA_HEADS // SWA_KV_HEADS
WINDOW = 128

kernel_name = 'hybrid_diffusion_prefix_trunk_step'

f32 = jnp.float32


def rms_norm(x, gain=None):
    xf = x.astype(f32)
    y = xf * lax.rsqrt(jnp.mean(xf * xf, axis=-1, keepdims=True) + NORM_EPS)
    if gain is not None:
        y = y * gain.astype(f32)
    return y.astype(x.dtype)


def modulation(cond, ada_w, ada_b):
    m = jax.nn.silu(cond) @ ada_w + ada_b
    return jnp.split(m[:, None, :], 6, axis=-1)


def adaln(x, shift, scale):
    return rms_norm(x) * (1 + scale) + shift


def squared_relu_mlp(h, w1, w2):
    return jnp.square(jax.nn.relu(h @ w1)) @ w2


def axial_rope(n_tok, head_dim):
    n_rows = n_tok // GRID_W
    rows = jnp.broadcast_to(jnp.arange(n_rows, dtype=f32)[:, None], (n_rows, GRID_W)).reshape(-1)
    cols = jnp.broadcast_to(jnp.arange(GRID_W, dtype=f32)[None, :], (n_rows, GRID_W)).reshape(-1)
    n_freq = head_dim // 4
    inv = ROPE_BASE ** (-jnp.arange(n_freq, dtype=f32) / n_freq)
    ang = jnp.concatenate([rows[:, None] * inv, cols[:, None] * inv], axis=-1)
    return jnp.cos(ang), jnp.sin(ang)


def apply_rope(x, cos, sin):
    shape = (cos.shape[0],) + (1,) * (x.ndim - 3) + (cos.shape[1],)
    cos = cos.reshape(shape)
    sin = sin.reshape(shape)
    x1, x2 = jnp.split(x.astype(f32), 2, axis=-1)
    return jnp.concatenate([x1 * cos - x2 * sin, x2 * cos + x1 * sin], axis=-1).astype(x.dtype)


def map_query_blocks(fn, q):
    b, t = q.shape[:2]
    nb = t // Q_BLOCK
    qb = jnp.moveaxis(q.reshape(b, nb, Q_BLOCK, *q.shape[2:]), 1, 0)
    out = jnp.moveaxis(lax.map(fn, qb), 0, 1)
    return out.reshape(b, t, *out.shape[3:])


def da_project(h, p):
    b, t, _ = h.shape
    q, k, v = jnp.split(h @ p['wqkv'], 3, axis=-1)
    q = rms_norm(q.reshape(b, t, DA_HEADS, 2, DA_HEAD_DIM), p['q_norm'])
    k = rms_norm(k.reshape(b, t, DA_HEADS, 2, DA_HEAD_DIM), p['k_norm'])
    v = v.reshape(b, t, DA_HEADS, 2 * DA_HEAD_DIM)
    return q, k, v


def da_lambda_value(lam, lam_init):
    lp = lam.astype(f32)
    return jnp.exp(jnp.sum(lp[0] * lp[1])) - jnp.exp(jnp.sum(lp[2] * lp[3])) + lam_init


def diff_attention(q, k, v, lam):
    scale = DA_HEAD_DIM ** -0.5

    def block(qb):
        s = jnp.einsum('bqhmd,bshmd->bhmqs', qb, k).astype(f32) * scale
        pr = jax.nn.softmax(s, axis=-1)
        pr = pr[:, :, 0] - lam * pr[:, :, 1]
        return jnp.einsum('bhqs,bshe->bqhe', pr, v).astype(v.dtype)

    return map_query_blocks(block, q)


def da_output(o, p, lam_init):
    b, t = o.shape[:2]
    o = rms_norm(o, p['subln']) * (1.0 - lam_init)
    return o.reshape(b, t, -1) @ p['wo']


def da_context(h, p, lam_init):
    b, t, _ = h.shape
    q, k, v = da_project(h, p)
    lam = da_lambda_value(p['lam'], lam_init)
    o = diff_attention(q, k, v, lam)
    return da_output(o, p, lam_init), k.reshape(b, t, DA_HEADS, 2 * DA_HEAD_DIM), v


def da_latent(h, ctx_k, ctx_v, cos, sin, p, lam_init):
    b = h.shape[0]
    q, k, v = da_project(h, p)
    q = apply_rope(q, cos, sin)
    k = apply_rope(k, cos, sin)
    n_ctx = ctx_k.shape[1]
    keys = jnp.concatenate([ctx_k.reshape(b, n_ctx, DA_HEADS, 2, DA_HEAD_DIM).astype(k.dtype), k], axis=1)
    vals = jnp.concatenate([ctx_v.astype(v.dtype), v], axis=1)
    lam = da_lambda_value(p['lam'], lam_init)
    o = diff_attention(q, keys, vals, lam)
    return da_output(o, p, lam_init)


def swa_project(h, p):
    b, t, _ = h.shape
    nq = SWA_HEADS * SWA_HEAD_DIM
    nk = SWA_KV_HEADS * SWA_HEAD_DIM
    q, k, v = jnp.split(h @ p['wqkv'], [nq, nq + nk], axis=-1)
    q = rms_norm(q.reshape(b, t, SWA_KV_HEADS, SWA_GROUP, SWA_HEAD_DIM), p['q_norm'])
    k = rms_norm(k.reshape(b, t, SWA_KV_HEADS, SWA_HEAD_DIM), p['k_norm'])
    v = v.reshape(b, t, SWA_KV_HEADS, SWA_HEAD_DIM)
    return q, k, v


def swa_context(h, p):
    b, t, _ = h.shape
    q, k, v = swa_project(h, p)
    scale = SWA_HEAD_DIM ** -0.5
    sink = p['sink'].astype(f32).reshape(SWA_KV_HEADS, SWA_GROUP)

    def block(qb):
        s = jnp.einsum('bqjgd,bsjd->bjgqs', qb, k).astype(f32) * scale
        sk = jnp.broadcast_to(sink[None, :, :, None, None], s.shape[:-1] + (1,))
        pr = jax.nn.softmax(jnp.concatenate([s, sk], axis=-1), axis=-1)[..., :-1]
        return jnp.einsum('bjgqs,bsjd->bqjgd', pr, v).astype(v.dtype)

    o = map_query_blocks(block, q)
    return o.reshape(b, t, -1) @ p['wo'], k, v


def swa_latent(h, ctx_k, ctx_v, cos, sin, p):
    b, t, _ = h.shape
    q, k, v = swa_project(h, p)
    q = apply_rope(q, cos, sin)
    k = apply_rope(k, cos, sin)
    scale = SWA_HEAD_DIM ** -0.5
    nb = t // WINDOW
    qb = q.reshape(b, nb, WINDOW, SWA_KV_HEADS, SWA_GROUP, SWA_HEAD_DIM)
    pad = ((0, 0), (WINDOW, WINDOW), (0, 0), (0, 0))
    kp = jnp.pad(k, pad).reshape(b, nb + 2, WINDOW, SWA_KV_HEADS, SWA_HEAD_DIM)
    vp = jnp.pad(v, pad).reshape(b, nb + 2, WINDOW, SWA_KV_HEADS, SWA_HEAD_DIM)
    kband = jnp.concatenate([kp[:, :-2], kp[:, 1:-1], kp[:, 2:]], axis=2)
    vband = jnp.concatenate([vp[:, :-2], vp[:, 1:-1], vp[:, 2:]], axis=2)
    s_loc = jnp.einsum('bnqjgd,bnsjd->bnjgqs', qb, kband).astype(f32) * scale
    qi = jnp.arange(WINDOW)[:, None]
    si = jnp.arange(3 * WINDOW)[None, :]
    rel = si - qi
    in_window = (rel >= 0) & (rel <= 2 * WINDOW)
    key_pos = (jnp.arange(nb)[:, None, None] - 1) * WINDOW + si[None]
    valid = in_window[None] & (key_pos >= 0) & (key_pos < t)
    s_loc = jnp.where(valid[None, :, None, None], s_loc, NEG_INF)
    s_ctx = jnp.einsum('bnqjgd,bljd->bnjgql', qb, ctx_k.astype(q.dtype)).astype(f32) * scale
    sink = p['sink'].astype(f32).reshape(SWA_KV_HEADS, SWA_GROUP)
    sk = jnp.broadcast_to(sink[None, None, :, :, None, None], s_loc.shape[:-1] + (1,))
    n_ctx = ctx_k.shape[1]
    pr = jax.nn.softmax(jnp.concatenate([s_ctx, s_loc, sk], axis=-1), axis=-1)
    o = (jnp.einsum('bnjgql,bljd->bnqjgd', pr[..., :n_ctx], ctx_v.astype(f32))
         + jnp.einsum('bnjgqs,bnsjd->bnqjgd', pr[..., n_ctx:n_ctx + 3 * WINDOW], vband))
    return o.reshape(b, t, -1).astype(h.dtype) @ p['wo']


def token_shift_delta(x):
    xp = jnp.pad(x, ((0, 0), (1, 1), (0, 0)))
    return 0.5 * (xp[:, :-2] + xp[:, 2:]) - x


def wkv_scan(s0, r, w, k, v, a, bb, reverse):
    seq = tuple(jnp.moveaxis(z.astype(f32), 1, 0) for z in (r, w, k, v, a, bb))

    def step(s, inp):
        rt, wt, kt, vt, at, bt = inp
        sa = jnp.einsum('bhvk,bhk->bhv', s, at)
        s = s * wt[:, :, None, :] + sa[..., None] * bt[:, :, None, :] + vt[..., None] * kt[:, :, None, :]
        return s, jnp.einsum('bhvk,bhk->bhv', s, rt)

    s_fin, ys = lax.scan(step, s0.astype(f32), seq, reverse=reverse)
    return s_fin, jnp.moveaxis(ys, 0, 1)


def rwkv_mixer(h, s0, p):
    b, t, d = h.shape
    heads = lambda z: z.reshape(*z.shape[:-1], RWKV_HEADS, RWKV_HEAD)
    xx = token_shift_delta(h)
    xs = h[None] + xx[None] * p['mu'][:, None, None, :]
    r, k, v = jnp.einsum('mbtd,mde->mbte', xs[:3], p['wrkv'])
    xw, xa, xg = xs[3], xs[4], xs[5]
    w_lora = jnp.einsum('zbtr,zrd->zbtd', jnp.tanh(jnp.einsum('btd,zdr->zbtr', xw, p['w1'])), p['w2'])
    w_log = -jax.nn.softplus(-(p['w0'][:, None, None, :] + w_lora).astype(f32)) - 0.5
    decay = jnp.exp(-jnp.exp(w_log))
    a_lora = jnp.einsum('zbtr,zrd->zbtd', jnp.einsum('btd,zdr->zbtr', xa, p['a1']), p['a2'])
    a = jax.nn.sigmoid((p['a0'][:, None, None, :] + a_lora).astype(f32))
    g = jax.nn.sigmoid(xg @ p['g1']) @ p['g2']
    kk = heads((k * p['k_k']).astype(f32))
    kk = kk * lax.rsqrt(jnp.maximum(jnp.sum(kk * kk, axis=-1, keepdims=True), 1e-24))
    kd = heads(k.astype(f32)[None] * (1 + (a - 1) * p['k_a'].astype(f32)))
    b_vec = kk[None] * heads(a)
    r_h = heads(r.astype(f32))
    v_h = heads(v.astype(f32))
    s_f, y_f = wkv_scan(s0[:, 0], r_h, heads(decay[0]), kd[0], v_h, -kk, b_vec[0], False)
    s_b, y_b = wkv_scan(s0[:, 1], r_h, heads(decay[1]), kd[1], v_h, -kk, b_vec[1], True)
    y = y_f + y_b
    mu = jnp.mean(y, axis=-1, keepdims=True)
    var = jnp.mean(jnp.square(y - mu), axis=-1, keepdims=True)
    y = (y - mu) * lax.rsqrt(var + RWKV_GN_EPS) * heads(p['ln_w'].astype(f32)) + heads(p['ln_b'].astype(f32))
    bonus = jnp.sum(r_h[None] * kd * p['r_k'].astype(f32), axis=-1, keepdims=True).sum(axis=0) * v_h
    out = ((y + bonus).reshape(b, t, d).astype(h.dtype) * g) @ p['wo']
    return out, jnp.stack([s_f, s_b], axis=1).astype(h.dtype)


def setup_inputs(seed: int = 0) -> dict:
    key = jax.random.key(seed)
    ks = iter(jax.random.split(key, 64))
    D = D_MODEL

    def nrm(shape, scale):
        return scale * jax.random.normal(next(ks), shape, f32)

    def uni(shape, lo, hi):
        return jax.random.uniform(next(ks), shape, f32, lo, hi)

    nq = SWA_HEADS * SWA_HEAD_DIM
    return {
        'x_prompt': nrm((BATCH, SEQ, D), 1.0),
        'x_sample': nrm((DEC_BATCH, DEC_SEQ, D), 1.0),
        'cache_da_k': nrm((DEC_BATCH, N_LAYERS_DA, PAST_LEN, DA_HEADS, 2 * DA_HEAD_DIM), 1.0),
        'cache_da_v': nrm((DEC_BATCH, N_LAYERS_DA, PAST_LEN, DA_HEADS, 2 * DA_HEAD_DIM), 1.0),
        'state_rwkv': nrm((DEC_BATCH, N_LAYERS_RWKV, 2, RWKV_HEADS, RWKV_HEAD, RWKV_HEAD), 1.0),
        'cache_swa_k': nrm((DEC_BATCH, N_LAYERS_SWA, PAST_LEN, SWA_KV_HEADS, SWA_HEAD_DIM), 1.0),
        'cache_swa_v': nrm((DEC_BATCH, N_LAYERS_SWA, PAST_LEN, SWA_KV_HEADS, SWA_HEAD_DIM), 1.0),
        'c': nrm((DEC_BATCH, D), 1.0),
        'c_ctx': nrm((D,), 1.0),
        'ada_w': nrm((DEPTH, D, 6 * D), 0.5 * D ** -0.5),
        'ada_b': nrm((DEPTH, 6 * D), 0.02),
        'mlp_w1': nrm((DEPTH, D, D_FF), D ** -0.5),
        'mlp_w2': nrm((DEPTH, D_FF, D), D_FF ** -0.5),
        'da_wqkv': nrm((N_LAYERS_DA, D, 3 * D), D ** -0.5),
        'da_q_norm': 1.0 + nrm((N_LAYERS_DA, DA_HEAD_DIM), 0.02),
        'da_k_norm': 1.0 + nrm((N_LAYERS_DA, DA_HEAD_DIM), 0.02),
        'da_lambda': nrm((N_LAYERS_DA, 4, DA_HEAD_DIM), 0.1),
        'da_subln': 1.0 + nrm((N_LAYERS_DA, 2 * DA_HEAD_DIM), 0.02),
        'da_wo': nrm((N_LAYERS_DA, D, D), D ** -0.5),
        'rwkv_mu': uni((N_LAYERS_RWKV, 6, D), 0.0, 1.0),
        'rwkv_wrkv': nrm((N_LAYERS_RWKV, 3, D, D), D ** -0.5),
        'rwkv_w0': uni((N_LAYERS_RWKV, 2, D), -6.0, 1.0),
        'rwkv_w1': nrm((N_LAYERS_RWKV, 2, D, RWKV_DECAY_LORA), D ** -0.5),
        'rwkv_w2': nrm((N_LAYERS_RWKV, 2, RWKV_DECAY_LORA, D), 0.5 * RWKV_DECAY_LORA ** -0.5),
        'rwkv_a0': nrm((N_LAYERS_RWKV, 2, D), 0.1),
        'rwkv_a1': nrm((N_LAYERS_RWKV, 2, D, RWKV_ICLR_LORA), D ** -0.5),
        'rwkv_a2': nrm((N_LAYERS_RWKV, 2, RWKV_ICLR_LORA, D), 0.5 * RWKV_ICLR_LORA ** -0.5),
        'rwkv_g1': nrm((N_LAYERS_RWKV, D, RWKV_GATE_LORA), D ** -0.5),
        'rwkv_g2': nrm((N_LAYERS_RWKV, RWKV_GATE_LORA, D), RWKV_GATE_LORA ** -0.5),
        'rwkv_k_k': 0.85 + nrm((N_LAYERS_RWKV, D), 0.02),
        'rwkv_k_a': 1.0 + nrm((N_LAYERS_RWKV, D), 0.02),
        'rwkv_r_k': nrm((N_LAYERS_RWKV, RWKV_HEADS, RWKV_HEAD), 0.1),
        'rwkv_ln_w': 1.0 + nrm((N_LAYERS_RWKV, D), 0.02),
        'rwkv_ln_b': nrm((N_LAYERS_RWKV, D), 0.02),
        'rwkv_wo': nrm((N_LAYERS_RWKV, D, D), D ** -0.5),
        'swa_wqkv': nrm((N_LAYERS_SWA, D, nq + 2 * SWA_KV_HEADS * SWA_HEAD_DIM), D ** -0.5),
        'swa_q_norm': 1.0 + nrm((N_LAYERS_SWA, SWA_HEAD_DIM), 0.02),
        'swa_k_norm': 1.0 + nrm((N_LAYERS_SWA, SWA_HEAD_DIM), 0.02),
        'swa_sink': nrm((N_LAYERS_SWA, SWA_HEADS), 0.5),
        'swa_wo': nrm((N_LAYERS_SWA, nq, D), nq ** -0.5),
    }


def reference(x_prompt, x_sample, cache_da_k, cache_da_v, state_rwkv, cache_swa_k, cache_swa_v, c,
              c_ctx, ada_w, ada_b, mlp_w1, mlp_w2,
              da_wqkv, da_q_norm, da_k_norm, da_lambda, da_subln, da_wo,
              rwkv_mu, rwkv_wrkv, rwkv_w0, rwkv_w1, rwkv_w2, rwkv_a0, rwkv_a1, rwkv_a2,
              rwkv_g1, rwkv_g2, rwkv_k_k, rwkv_k_a, rwkv_r_k, rwkv_ln_w, rwkv_ln_b, rwkv_wo,
              swa_wqkv, swa_q_norm, swa_k_norm, swa_sink, swa_wo):
    x_ctx = x_prompt
    x_lat = x_sample
    n_lat = x_sample.shape[1]
    cos_da, sin_da = axial_rope(n_lat, DA_HEAD_DIM)
    cos_swa, sin_swa = axial_rope(n_lat, SWA_HEAD_DIM)
    b_ctx = x_prompt.shape[0]
    da_k_list, da_v_list, rwkv_list, swa_k_list, swa_v_list = [], [], [], [], []
    for i in range(DEPTH):
        kind = i % N_MIXERS
        j = i // N_MIXERS
        sh_m_c, sc_m_c, g_m_c, sh_f_c, sc_f_c, g_f_c = modulation(c_ctx[None, :], ada_w[i], ada_b[i])
        sh_m_l, sc_m_l, g_m_l, sh_f_l, sc_f_l, g_f_l = modulation(c, ada_w[i], ada_b[i])
        h_ctx = adaln(x_ctx, sh_m_c, sc_m_c)
        h_lat = adaln(x_lat, sh_m_l, sc_m_l)
        if kind == 0:
            p = {'wqkv': da_wqkv[j], 'q_norm': da_q_norm[j], 'k_norm': da_k_norm[j],
                 'lam': da_lambda[j], 'subln': da_subln[j], 'wo': da_wo[j]}
            lam_init = 0.8 - 0.6 * math.exp(-0.3 * i)
            o_ctx, k_new, v_new = da_context(h_ctx, p, lam_init)
            o_lat = da_latent(h_lat, cache_da_k[:, j], cache_da_v[:, j], cos_da, sin_da, p, lam_init)
            da_k_list.append(k_new)
            da_v_list.append(v_new)
        elif kind == 1:
            p = {'mu': rwkv_mu[j], 'wrkv': rwkv_wrkv[j], 'w0': rwkv_w0[j], 'w1': rwkv_w1[j],
                 'w2': rwkv_w2[j], 'a0': rwkv_a0[j], 'a1': rwkv_a1[j], 'a2': rwkv_a2[j],
                 'g1': rwkv_g1[j], 'g2': rwkv_g2[j], 'k_k': rwkv_k_k[j], 'k_a': rwkv_k_a[j],
                 'r_k': rwkv_r_k[j], 'ln_w': rwkv_ln_w[j], 'ln_b': rwkv_ln_b[j], 'wo': rwkv_wo[j]}
            s_zero = jnp.zeros((b_ctx, 2, RWKV_HEADS, RWKV_HEAD, RWKV_HEAD), f32)
            o_ctx, s_new = rwkv_mixer(h_ctx, s_zero, p)
            o_lat, _ = rwkv_mixer(h_lat, state_rwkv[:, j], p)
            rwkv_list.append(s_new)
        else:
            p = {'wqkv': swa_wqkv[j], 'q_norm': swa_q_norm[j], 'k_norm': swa_k_norm[j],
                 'sink': swa_sink[j], 'wo': swa_wo[j]}
            o_ctx, k_new, v_new = swa_context(h_ctx, p)
            o_lat = swa_latent(h_lat, cache_swa_k[:, j], cache_swa_v[:, j], cos_swa, sin_swa, p)
            swa_k_list.append(k_new)
            swa_v_list.append(v_new)
        x_ctx = x_ctx + g_m_c * o_ctx
        x_lat = x_lat + g_m_l * o_lat
        x_ctx = x_ctx + g_f_c * squared_relu_mlp(adaln(x_ctx, sh_f_c, sc_f_c), mlp_w1[i], mlp_w2[i])
        x_lat = x_lat + g_f_l * squared_relu_mlp(adaln(x_lat, sh_f_l, sc_f_l), mlp_w1[i], mlp_w2[i])
    new_da_k = jnp.stack(da_k_list, axis=1)
    new_da_v = jnp.stack(da_v_list, axis=1)
    new_state_rwkv = jnp.stack(rwkv_list, axis=1)
    new_swa_k = jnp.stack(swa_k_list, axis=1)
    new_swa_v = jnp.stack(swa_v_list, axis=1)
    return (x_ctx, x_lat, new_da_k, new_da_v, new_state_rwkv, new_swa_k, new_swa_v)
```

```python
import functools
import math
from typing import NamedTuple

import jax
import jax.numpy as jnp
from jax import lax
from jax.experimental import pallas as pl
from jax.experimental.pallas import tpu as pltpu

f32 = jnp.float32
bf16 = jnp.bfloat16

D_MODEL = 1024
N_MIXERS = 3
GRID_W = 64
ROPE_BASE = 10000.0
NORM_EPS = 1e-6
NEG_INF = -1e30
HEAD = 64
LANES = 128
DA_HEADS = D_MODEL // LANES
RWKV_HEADS = D_MODEL // HEAD
RWKV_GN_EPS = 64e-5
SWA_HEADS = D_MODEL // HEAD
SWA_KV_HEADS = SWA_HEADS // 4
SWA_GROUP = SWA_HEADS // SWA_KV_HEADS
SWA_KV = SWA_KV_HEADS * HEAD
WINDOW = 128
WKV_CHUNK = 64
VMEM_LIMIT = 56 * 1024 * 1024


class Layout(NamedTuple):
    b_ctx: int
    t_ctx: int
    b_lat: int
    t_lat: int

    @property
    def n_ctx(self):
        return self.b_ctx * self.t_ctx

    @property
    def n(self):
        return self.n_ctx + self.b_lat * self.t_lat


def _params(sem):
    return pltpu.CompilerParams(dimension_semantics=sem, vmem_limit_bytes=VMEM_LIMIT)


def _group_of_tile(i, tm, lay):
    r = i * tm
    return jnp.where(r < lay.n_ctx, 0, 1 + (r - lay.n_ctx) // lay.t_lat)


def _mod_spec(chunk, tm, lay):
    return pl.BlockSpec((None, None, 1, D_MODEL),
                        lambda i, *_: (_group_of_tile(i, tm, lay), chunk, 0, 0))


def _dot(a, b):
    return jnp.dot(a, b, preferred_element_type=f32)


def _dot_nt(a, b):
    return lax.dot_general(a, b, (((1,), (1,)), ((), ())), preferred_element_type=f32)


def _dot_tn(a, b):
    return lax.dot_general(a, b, (((0,), (0,)), ((), ())), preferred_element_type=f32)


def _split3(x):
    hi = x.astype(bf16)
    r1 = x - hi.astype(f32)
    mid = r1.astype(bf16)
    lo = (r1 - mid.astype(f32)).astype(bf16)
    return hi, mid, lo


def _dot_exact_rhs(a_bf16, x):
    hi, mid, lo = _split3(x)
    return _dot(a_bf16, hi) + (_dot(a_bf16, mid) + _dot(a_bf16, lo))


def _dot_exact_lhs(x, a_bf16):
    hi, mid, lo = _split3(x)
    return _dot(hi, a_bf16) + (_dot(mid, a_bf16) + _dot(lo, a_bf16))


def _head_ones():
    r = lax.broadcasted_iota(jnp.int32, (LANES, LANES), 0)
    c = lax.broadcasted_iota(jnp.int32, (LANES, LANES), 1)
    return ((r // HEAD) == (c // HEAD)).astype(bf16)


def _head_sum(x):
    g = _head_ones()
    cols = [
        _dot_exact_lhs(x[:, p * LANES:(p + 1) * LANES], g) for p in range(x.shape[1] // LANES)
    ]
    return cols[0] if len(cols) == 1 else jnp.concatenate(cols, axis=1)


def _head_rms(x, gain):
    ms = _head_sum(x * x) * (1.0 / HEAD)
    return x * lax.rsqrt(ms + NORM_EPS) * gain


def _adaln(x, shift, scale):
    ms = jnp.mean(x * x, axis=-1, keepdims=True)
    return x * lax.rsqrt(ms + NORM_EPS) * (1.0 + scale) + shift


def _rope(x, cos, sin):
    w = x.shape[1]
    lane = lax.broadcasted_iota(jnp.int32, x.shape, 1)
    low = (lane & (HEAD - 1)) < HEAD // 2
    up = pltpu.roll(x, w - HEAD // 2, 1)
    dn = pltpu.roll(x, HEAD // 2, 1)
    return x * cos + jnp.where(low, -up, dn) * sin


def _softplus(y):
    return jnp.maximum(y, 0.0) + jnp.log(1.0 + jnp.exp(-jnp.abs(y)))


def _mod_kernel(c_ref, w_ref, b_ref, o_ref):
    c = c_ref[...]
    s = c * jax.nn.sigmoid(c)
    sh, sm, sl = _split3(s)
    wh, wm, wl = _split3(w_ref[...])
    acc = _dot(sh, wh) + (_dot(sh, wm) + _dot(sm, wh)) + (_dot(sh, wl) + _dot(sm, wm) + _dot(sl, wh))
    o_ref[...] = acc + b_ref[...]


def _modulation(cond8, ada_w, ada_b):
    depth, d, d6 = ada_w.shape
    tn = 1536
    return pl.pallas_call(
        _mod_kernel,
        grid=(depth, d6 // tn),
        in_specs=[
            pl.BlockSpec((8, d), lambda l, j: (0, 0)),
            pl.BlockSpec((None, d, tn), lambda l, j: (l, 0, j)),
            pl.BlockSpec((None, 1, tn), lambda l, j: (l, 0, j)),
        ],
        out_specs=pl.BlockSpec((None, 8, tn), lambda l, j: (l, 0, j)),
        out_shape=jax.ShapeDtypeStruct((depth, 8, d6), f32),
        compiler_params=_params(("parallel", "parallel")),
        name="modulation",
    )(cond8, ada_w, ada_b.reshape(depth, 1, d6))


def _ln_mm_kernel(x_ref, sh_ref, sc_ref, w_ref, o_ref, h_scr):
    @pl.when(pl.program_id(1) == 0)
    def _():
        h_scr[...] = _adaln(x_ref[...], sh_ref[...], sc_ref[...]).astype(bf16)

    o_ref[...] = _dot(h_scr[...], w_ref[...])


def _ln_matmul(x, mod, w_bf16, lay, tm=512):
    n, d = x.shape
    n_out = w_bf16.shape[1]
    tn = 512
    return pl.pallas_call(
        _ln_mm_kernel,
        grid=(n // tm, n_out // tn),
        in_specs=[
            pl.BlockSpec((tm, d), lambda i, j: (i, 0)),
            _mod_spec(0, tm, lay),
            _mod_spec(1, tm, lay),
            pl.BlockSpec((d, tn), lambda i, j: (0, j)),
        ],
        out_specs=pl.BlockSpec((tm, tn), lambda i, j: (i, j)),
        out_shape=jax.ShapeDtypeStruct((n, n_out), f32),
        scratch_shapes=[pltpu.VMEM((tm, d), bf16)],
        compiler_params=_params(("parallel", "arbitrary")),
        name="adaln_proj",
    )(x, mod, mod, w_bf16)


def _mm_res_kernel(x_ref, a_ref, g_ref, w_ref, o_ref):
    o_ref[...] = x_ref[...] + g_ref[...] * _dot(a_ref[...].astype(bf16), w_ref[...])


def _matmul_residual(x, a, mod, w_bf16, lay, tm=512):
    n, d = x.shape
    k_in = a.shape[1]
    return pl.pallas_call(
        _mm_res_kernel,
        grid=(n // tm,),
        in_specs=[
            pl.BlockSpec((tm, d), lambda i: (i, 0)),
            pl.BlockSpec((tm, k_in), lambda i: (i, 0)),
            _mod_spec(2, tm, lay),
            pl.BlockSpec((k_in, d), lambda i: (0, 0)),
        ],
        out_specs=pl.BlockSpec((tm, d), lambda i: (i, 0)),
        out_shape=jax.ShapeDtypeStruct((n, d), f32),
        compiler_params=_params(("parallel",)),
        name="out_proj_residual",
    )(x, a, mod, w_bf16)


def _mlp_kernel(x_ref, sh_ref, sc_ref, g_ref, w1_ref, w2_ref, o_ref, acc_ref, *, fc):
    x = x_ref[...]
    h = _adaln(x, sh_ref[...], sc_ref[...]).astype(bf16)
    d_ff = w1_ref.shape[1]
    for c in range(d_ff // fc):
        a = _dot(h, w1_ref[:, c * fc:(c + 1) * fc])
        a = jnp.square(jnp.maximum(a, 0.0)).astype(bf16)
        part = _dot(a, w2_ref[c * fc:(c + 1) * fc, :])
        if c == 0:
            acc_ref[...] = part
        else:
            acc_ref[...] += part
    o_ref[...] = x + g_ref[...] * acc_ref[...]


def _mlp(x, mod, w1_bf16, w2_bf16, lay, tm=512, fc=1024):
    n, d = x.shape
    d_ff = w1_bf16.shape[1]
    return pl.pallas_call(
        functools.partial(_mlp_kernel, fc=fc),
        grid=(n // tm,),
        in_specs=[
            pl.BlockSpec((tm, d), lambda i: (i, 0)),
            _mod_spec(3, tm, lay),
            _mod_spec(4, tm, lay),
            _mod_spec(5, tm, lay),
            pl.BlockSpec((d, d_ff), lambda i: (0, 0)),
            pl.BlockSpec((d_ff, d), lambda i: (0, 0)),
        ],
        out_specs=pl.BlockSpec((tm, d), lambda i: (i, 0)),
        out_shape=jax.ShapeDtypeStruct((n, d), f32),
        scratch_shapes=[pltpu.VMEM((tm, d), f32)],
        compiler_params=_params(("parallel",)),
        name="mlp",
    )(x, mod, mod, mod, w1_bf16, w2_bf16)


def _da_lambda(lam_ref, lam_init):
    lp = lam_ref[...]
    e1 = jnp.exp(jnp.sum(lp[0:1] * lp[1:2], axis=-1, keepdims=True))
    e2 = jnp.exp(jnp.sum(lp[2:3] * lp[3:4], axis=-1, keepdims=True))
    return e1 - e2 + lam_init


def _diff_attention(q, k_bf16, v_bf16, lam):
    lane = lax.broadcasted_iota(jnp.int32, q.shape, 1)
    probs = []
    for m in range(2):
        in_map = (lane < HEAD) if m == 0 else (lane >= HEAD)
        qm = jnp.where(in_map, q, 0.0).astype(bf16)
        s = _dot_nt(qm, k_bf16)
        e = jnp.exp(s - jnp.max(s, axis=-1, keepdims=True))
        probs.append(e * (1.0 / jnp.sum(e, axis=-1, keepdims=True)))
    p = probs[0] - lam * probs[1]
    return _dot(p.astype(bf16), v_bf16)


def _da_finish(o, sub_ref, lam_init):
    ms = jnp.mean(o * o, axis=-1, keepdims=True)
    return o * lax.rsqrt(ms + NORM_EPS) * sub_ref[...] * (1.0 - lam_init)


def _da_ctx_kernel(lam_ref, q_ref, k_ref, v_ref, qn_ref, kn_ref, sub_ref, o_ref, knew_ref, *, lam_init):
    lam = _da_lambda(lam_ref, lam_init)
    q = _head_rms(q_ref[...], qn_ref[...]) * (HEAD ** -0.5)
    k = _head_rms(k_ref[...], kn_ref[...])
    knew_ref[...] = k
    o = _diff_attention(q, k.astype(bf16), v_ref[...].astype(bf16), lam)
    o_ref[...] = _da_finish(o, sub_ref, lam_init)


def _da_lat_kernel(lam_ref, q_ref, k_ref, v_ref, ck_ref, cv_ref, qn_ref, kn_ref, sub_ref,
                   cq_ref, sq_ref, ckk_ref, skk_ref, o_ref, kf_scr, vf_scr, *, lam_init, past):
    @pl.when(pl.program_id(2) == 0)
    def _():
        k = _rope(_head_rms(k_ref[...], kn_ref[...]), ckk_ref[...], skk_ref[...])
        kf_scr[0:past, :] = ck_ref[...].astype(bf16)
        kf_scr[past:, :] = k.astype(bf16)
        vf_scr[0:past, :] = cv_ref[...].astype(bf16)
        vf_scr[past:, :] = v_ref[...].astype(bf16)

    lam = _da_lambda(lam_ref, lam_init)
    q = _rope(_head_rms(q_ref[...], qn_ref[...]), cq_ref[...], sq_ref[...]) * (HEAD ** -0.5)
    o = _diff_attention(q, kf_scr[...], vf_scr[...], lam)
    o_ref[...] = _da_finish(o, sub_ref, lam_init)


def _da_mixer(qkv, cache_k, cache_v, j, p, lam_init, rope_tab, lay, tq=256):
    n = qkv.shape[0]
    d = D_MODEL
    nh = DA_HEADS
    qn = jnp.tile(p['q_norm'], 2).reshape(1, LANES)
    kn = jnp.tile(p['k_norm'], 2).reshape(1, LANES)
    sub = p['subln'].reshape(1, LANES)
    small = lambda shape: pl.BlockSpec(shape, lambda *_: (0,) * len(shape))
    t = lay.t_ctx
    o_ctx, k_new = pl.pallas_call(
        functools.partial(_da_ctx_kernel, lam_init=lam_init),
        grid=(lay.b_ctx, nh),
        in_specs=[
            small((4, HEAD)),
            pl.BlockSpec((t, LANES), lambda b, h: (b, h)),
            pl.BlockSpec((t, LANES), lambda b, h: (b, nh + h)),
            pl.BlockSpec((t, LANES), lambda b, h: (b, 2 * nh + h)),
            small((1, LANES)), small((1, LANES)), small((1, LANES)),
        ],
        out_specs=[
            pl.BlockSpec((t, LANES), lambda b, h: (b, h)),
            pl.BlockSpec((t, LANES), lambda b, h: (b, h)),
        ],
        out_shape=[jax.ShapeDtypeStruct((lay.n_ctx, d), f32)] * 2,
        compiler_params=_params(("parallel", "parallel")),
        name="da_ctx",
    )(p['lam'], qkv, qkv, qkv, qn, kn, sub)

    tl = lay.t_lat
    past = cache_k.shape[2]
    row0 = lay.n_ctx // tl
    q0 = lay.n_ctx // tq
    nq = tl // tq
    cos, sin = rope_tab
    ck = cache_k.reshape(lay.b_lat, cache_k.shape[1], past, d)
    cv = cache_v.reshape(lay.b_lat, cache_v.shape[1], past, d)
    o_lat = pl.pallas_call(
        functools.partial(_da_lat_kernel, lam_init=lam_init, past=past),
        grid=(lay.b_lat, nh, nq),
        in_specs=[
            small((4, HEAD)),
            pl.BlockSpec((tq, LANES), lambda b, h, i: (q0 + b * nq + i, h)),
            pl.BlockSpec((tl, LANES), lambda b, h, i: (row0 + b, nh + h)),
            pl.BlockSpec((tl, LANES), lambda b, h, i: (row0 + b, 2 * nh + h)),
            pl.BlockSpec((None, None, past, LANES), lambda b, h, i: (b, j, 0, h)),
            pl.BlockSpec((None, None, past, LANES), lambda b, h, i: (b, j, 0, h)),
            small((1, LANES)), small((1, LANES)), small((1, LANES)),
            pl.BlockSpec((tq, LANES), lambda b, h, i: (i, 0)),
            pl.BlockSpec((tq, LANES), lambda b, h, i: (i, 0)),
            small((tl, LANES)), small((tl, LANES)),
        ],
        out_specs=pl.BlockSpec((tq, LANES), lambda b, h, i: (b * nq + i, h)),
        out_shape=jax.ShapeDtypeStruct((lay.b_lat * tl, d), f32),
        scratch_shapes=[pltpu.VMEM((past + tl, LANES), bf16), pltpu.VMEM((past + tl, LANES), bf16)],
        compiler_params=_params(("parallel", "parallel", "arbitrary")),
        name="da_lat",
    )(p['lam'], qkv, qkv, qkv, ck, cv, qn, kn, sub, cos, sin, cos, sin)
    return jnp.concatenate([o_ctx, o_lat], axis=0), k_new


def _sink_column(sink_ref, j, rows):
    parts = [jnp.full((rows, 1), sink_ref[j * SWA_GROUP + g], f32) for g in range(SWA_GROUP)]
    return jnp.concatenate(parts, axis=0)


def _stack_group(x, j):
    return jnp.concatenate(
        [x[:, (j * SWA_GROUP + g) * HEAD:(j * SWA_GROUP + g + 1) * HEAD] for g in range(SWA_GROUP)],
        axis=0).astype(bf16)


def _swa_ctx_kernel(sink_ref, q_ref, k_ref, v_ref, qn_ref, kn_ref, o_ref, knew_ref):
    t = q_ref.shape[0]
    q = _head_rms(q_ref[...], qn_ref[...]) * (HEAD ** -0.5)
    k = _head_rms(k_ref[...], kn_ref[...])
    knew_ref[...] = k
    v = v_ref[...]
    for j in range(SWA_KV_HEADS):
        hs = slice(j * HEAD, (j + 1) * HEAD)
        qs = _stack_group(q, j)
        s = _dot_nt(qs, k[:, hs].astype(bf16))
        sk = _sink_column(sink_ref, j, t)
        m = jnp.maximum(jnp.max(s, axis=-1, keepdims=True), sk)
        e = jnp.exp(s - m)
        den = jnp.sum(e, axis=-1, keepdims=True) + jnp.exp(sk - m)
        o = _dot(e.astype(bf16), v[:, hs].astype(bf16)) * (1.0 / den)
        for g in range(SWA_GROUP):
            h = j * SWA_GROUP + g
            o_ref[:, h * HEAD:(h + 1) * HEAD] = o[g * t:(g + 1) * t]


def _swa_lat_kernel(sink_ref, q_ref, k_ref, v_ref, ck_ref, cv_ref, qn_ref, kn_ref,
                    cq_ref, sq_ref, ckk_ref, skk_ref, o_ref, kp_scr, vp_scr, *, t_lat):
    n = pl.program_id(1)
    w = WINDOW

    @pl.when(n == 0)
    def _():
        zeros = jnp.zeros((w, SWA_KV), f32)
        kp_scr[0:w, :] = zeros
        kp_scr[w:w + t_lat, :] = _rope(_head_rms(k_ref[...], kn_ref[...]), ckk_ref[...], skk_ref[...])
        kp_scr[w + t_lat:, :] = zeros
        vp_scr[0:w, :] = zeros
        vp_scr[w:w + t_lat, :] = v_ref[...]
        vp_scr[w + t_lat:, :] = zeros

    q = _rope(_head_rms(q_ref[...], qn_ref[...]), cq_ref[...], sq_ref[...]) * (HEAD ** -0.5)
    start = pl.multiple_of(n * w, w)
    kband = kp_scr[pl.ds(start, 3 * w), :]
    vband = vp_scr[pl.ds(start, 3 * w), :]
    ckb = ck_ref[...]
    cvb = cv_ref[...]
    rows = SWA_GROUP * w
    qi = lax.broadcasted_iota(jnp.int32, (rows, 3 * w), 0) & (w - 1)
    si = lax.broadcasted_iota(jnp.int32, (rows, 3 * w), 1)
    rel = si - qi
    key_pos = (n - 1) * w + si
    valid = (rel >= 0) & (rel <= 2 * w) & (key_pos >= 0) & (key_pos < t_lat)
    for j in range(SWA_KV_HEADS):
        hs = slice(j * HEAD, (j + 1) * HEAD)
        qs = _stack_group(q, j)
        s_loc = jnp.where(valid, _dot_nt(qs, kband[:, hs].astype(bf16)), NEG_INF)
        s_ctx = _dot_nt(qs, ckb[:, hs].astype(bf16))
        sk = _sink_column(sink_ref, j, w)
        m = jnp.maximum(jnp.maximum(jnp.max(s_loc, axis=-1, keepdims=True),
                                    jnp.max(s_ctx, axis=-1, keepdims=True)), sk)
        e_loc = jnp.exp(s_loc - m)
        e_ctx = jnp.exp(s_ctx - m)
        den = (jnp.sum(e_loc, axis=-1, keepdims=True) + jnp.sum(e_ctx, axis=-1, keepdims=True)
               + jnp.exp(sk - m))
        o = (_dot(e_ctx.astype(bf16), cvb[:, hs].astype(bf16))
             + _dot(e_loc.astype(bf16), vband[:, hs].astype(bf16))) * (1.0 / den)
        for g in range(SWA_GROUP):
            h = j * SWA_GROUP + g
            o_ref[:, h * HEAD:(h + 1) * HEAD] = o[g * w:(g + 1) * w]


def _swa_mixer(qkv, cache_k, cache_v, j, p, rope_tab, lay):
    d = D_MODEL
    qn = jnp.tile(p['q_norm'], d // HEAD).reshape(1, d)
    kn = jnp.tile(p['k_norm'], SWA_KV_HEADS).reshape(1, SWA_KV)
    small = lambda shape: pl.BlockSpec(shape, lambda *_: (0,) * len(shape))
    smem = pl.BlockSpec(memory_space=pltpu.SMEM)
    t = lay.t_ctx
    kcol = d // SWA_KV
    o_ctx, k_new = pl.pallas_call(
        _swa_ctx_kernel,
        grid=(lay.b_ctx,),
        in_specs=[
            smem,
            pl.BlockSpec((t, d), lambda b: (b, 0)),
            pl.BlockSpec((t, SWA_KV), lambda b: (b, kcol)),
            pl.BlockSpec((t, SWA_KV), lambda b: (b, kcol + 1)),
            small((1, d)), small((1, SWA_KV)),
        ],
        out_specs=[
            pl.BlockSpec((t, d), lambda b: (b, 0)),
            pl.BlockSpec((t, SWA_KV), lambda b: (b, 0)),
        ],
        out_shape=[jax.ShapeDtypeStruct((lay.n_ctx, d), f32),
                   jax.ShapeDtypeStruct((lay.n_ctx, SWA_KV), f32)],
        compiler_params=_params(("parallel",)),
        name="swa_ctx",
    )(p['sink'], qkv, qkv, qkv, qn, kn)

    tl = lay.t_lat
    w = WINDOW
    past = cache_k.shape[2]
    row0 = lay.n_ctx // tl
    q0 = lay.n_ctx // w
    nq = tl // w
    cos, sin = rope_tab
    ck = cache_k.reshape(lay.b_lat, cache_k.shape[1], past, SWA_KV)
    cv = cache_v.reshape(lay.b_lat, cache_v.shape[1], past, SWA_KV)
    o_lat = pl.pallas_call(
        functools.partial(_swa_lat_kernel, t_lat=tl),
        grid=(lay.b_lat, nq),
        in_specs=[
            smem,
            pl.BlockSpec((w, d), lambda b, i: (q0 + b * nq + i, 0)),
            pl.BlockSpec((tl, SWA_KV), lambda b, i: (row0 + b, kcol)),
            pl.BlockSpec((tl, SWA_KV), lambda b, i: (row0 + b, kcol + 1)),
            pl.BlockSpec((None, None, past, SWA_KV), lambda b, i: (b, j, 0, 0)),
            pl.BlockSpec((None, None, past, SWA_KV), lambda b, i: (b, j, 0, 0)),
            small((1, d)), small((1, SWA_KV)),
            pl.BlockSpec((w, d), lambda b, i: (i, 0)),
            pl.BlockSpec((w, d), lambda b, i: (i, 0)),
            pl.BlockSpec((tl, SWA_KV), lambda b, i: (0, 0)),
            pl.BlockSpec((tl, SWA_KV), lambda b, i: (0, 0)),
        ],
        out_specs=pl.BlockSpec((w, d), lambda b, i: (b * nq + i, 0)),
        out_shape=jax.ShapeDtypeStruct((lay.b_lat * tl, d), f32),
        scratch_shapes=[pltpu.VMEM((tl + 2 * w, SWA_KV), f32), pltpu.VMEM((tl + 2 * w, SWA_KV), f32)],
        compiler_params=_params(("parallel", "arbitrary")),
        name="swa_lat",
    )(p['sink'], qkv, qkv, qkv, ck, cv, qn, kn, cos, sin, cos, sin)
    return jnp.concatenate([o_ctx, o_lat], axis=0), k_new


def _rwkv_proj_kernel(x_ref, xp_ref, xn_ref, sh_ref, sc_ref, mu_ref, wrkv_ref, w1_ref, w2_ref, w0_ref,
                      a1_ref, a2_ref, a0_ref, g1_ref, g2_ref,
                      r_ref, k_ref, v_ref, g_ref, a_ref, lw_ref, *, lay, tm):
    i = pl.program_id(0)
    r0 = i * tm
    in_ctx = r0 < lay.n_ctx
    pos = jnp.where(in_ctx, r0 % lay.t_ctx, (r0 - lay.n_ctx) % lay.t_lat)
    seq_len = jnp.where(in_ctx, lay.t_ctx, lay.t_lat)
    keep_prev = jnp.where(pos == 0, 0.0, 1.0)
    keep_next = jnp.where(pos + tm == seq_len, 0.0, 1.0)

    sh = sh_ref[...]
    sc = sc_ref[...]
    h = _adaln(x_ref[...], sh, sc)
    h_prev_row = _adaln(xp_ref[7:8, :], sh, sc) * keep_prev
    h_next_row = _adaln(xn_ref[0:1, :], sh, sc) * keep_next
    row = lax.broadcasted_iota(jnp.int32, h.shape, 0)
    prev = jnp.where(row == 0, h_prev_row, pltpu.roll(h, 1, 0))
    nxt = jnp.where(row == tm - 1, h_next_row, pltpu.roll(h, tm - 1, 0))
    xx = 0.5 * (prev + nxt) - h

    def mix(m):
        return (h + xx * mu_ref[m:m + 1, :]).astype(bf16)

    r_ref[...] = _dot(mix(0), wrkv_ref[0])
    k_ref[...] = _dot(mix(1), wrkv_ref[1])
    v_ref[...] = _dot(mix(2), wrkv_ref[2])

    lane = lax.broadcasted_iota(jnp.int32, (tm, LANES), 1)
    first = lane < HEAD

    tw = jnp.tanh(_dot(mix(3), w1_ref[...]))
    la = _dot(mix(4), a1_ref[...])
    for z in range(2):
        sel = first if z == 0 else jnp.logical_not(first)
        w_lora = _dot(jnp.where(sel, tw, 0.0).astype(bf16), w2_ref[...])
        w_log = -_softplus(-(w0_ref[z:z + 1, :] + w_lora)) - 0.5
        lw_ref[z] = -jnp.exp(w_log)
        a_lora = _dot(jnp.where(sel, la, 0.0).astype(bf16), a2_ref[...])
        a_ref[z] = jax.nn.sigmoid(a0_ref[z:z + 1, :] + a_lora)

    g_ref[...] = _dot(jax.nn.sigmoid(_dot(mix(5), g1_ref[...])).astype(bf16), g2_ref[...])


def _rwkv_proj(x, mod, p, lay, tm=256):
    n, d = x.shape
    assert lay.t_ctx % tm == 0 and lay.t_lat % tm == 0
    nb8 = n // 8
    full = lambda a: pl.BlockSpec(a.shape, lambda i: (0,) * a.ndim)
    wrkv = p['wrkv'].astype(bf16)
    w1 = jnp.concatenate([p['w1'][0], p['w1'][1]], axis=1).astype(bf16)
    w2 = p['w2'].reshape(2 * p['w2'].shape[1], d).astype(bf16)
    a1 = jnp.concatenate([p['a1'][0], p['a1'][1]], axis=1).astype(bf16)
    a2 = p['a2'].reshape(2 * p['a2'].shape[1], d).astype(bf16)
    g1 = p['g1'].astype(bf16)
    g2 = p['g2'].astype(bf16)
    row_spec = pl.BlockSpec((tm, d), lambda i: (i, 0))
    z_spec = pl.BlockSpec((2, tm, d), lambda i: (0, i, 0))
    args = (x, x, x, mod, mod, p['mu'], wrkv, w1, w2, p['w0'], a1, a2, p['a0'], g1, g2)
    return pl.pallas_call(
        functools.partial(_rwkv_proj_kernel, lay=lay, tm=tm),
        grid=(n // tm,),
        in_specs=[
            row_spec,
            pl.BlockSpec((8, d), lambda i: (jnp.maximum(i * (tm // 8) - 1, 0), 0)),
            pl.BlockSpec((8, d), lambda i: (jnp.minimum((i + 1) * (tm // 8), nb8 - 1), 0)),
            _mod_spec(0, tm, lay),
            _mod_spec(1, tm, lay),
        ] + [full(a) for a in args[5:]],
        out_specs=[row_spec] * 4 + [z_spec] * 2,
        out_shape=[jax.ShapeDtypeStruct((n, d), f32)] * 4 + [jax.ShapeDtypeStruct((2, n, d), f32)] * 2,
        compiler_params=_params(("parallel",)),
        name="rwkv_proj",
    )(*args)


def _wkv_kernel(*refs, has_init):
    if has_init:
        (r_ref, k_ref, v_ref, a_ref, lw_ref, kk_ref, ka_ref, s0_ref, y_ref, sout_ref, s_scr) = refs
    else:
        (r_ref, k_ref, v_ref, a_ref, lw_ref, kk_ref, ka_ref, y_ref, sout_ref, s_scr) = refs
    c_len = WKV_CHUNK
    direction = pl.program_id(2)
    c = pl.program_id(3)

    @pl.when(c == 0)
    def _():
        if has_init:
            s_scr[...] = s0_ref[...]
        else:
            s_scr[...] = jnp.zeros_like(s_scr)

    sgn = jnp.where(direction == 0, 1, -1)
    t_s = lax.broadcasted_iota(jnp.int32, (c_len, LANES), 0)
    i_s = lax.broadcasted_iota(jnp.int32, (c_len, LANES), 1) & (c_len - 1)
    order = (t_s - i_s) * sgn
    strict = order > 0
    incl = order >= 0
    t_c = lax.broadcasted_iota(jnp.int32, (c_len, c_len), 0)
    i_c = lax.broadcasted_iota(jnp.int32, (c_len, c_len), 1)
    tri = (((t_c - i_c) * sgn) >= 0).astype(bf16)
    rb = lax.broadcasted_iota(jnp.int32, (2 * c_len, LANES), 0) // c_len
    cb = lax.broadcasted_iota(jnp.int32, (2 * c_len, LANES), 1) // HEAD
    same_head = rb == cb

    def bd(x):
        return jnp.where(same_head, jnp.concatenate([x, x], axis=0), 0.0).astype(bf16)

    r = r_ref[...]
    k = k_ref[...]
    v = v_ref[...]
    a = a_ref[...]
    lw = lw_ref[...]

    kkr = k * kk_ref[...]
    kk = kkr * lax.rsqrt(jnp.maximum(_head_sum(kkr * kkr), 1e-24))
    beta = kk * a
    kd = k * (1.0 + (a - 1.0) * ka_ref[...])

    cum = _dot_exact_rhs(tri, lw)
    total = jnp.where(direction == 0, cum[c_len - 1:c_len, :], cum[0:1, :])
    a_t = -kk * jnp.exp(cum - lw)
    r_t = r * jnp.exp(cum)
    inv = jnp.exp(-cum)
    rem = jnp.exp(total - cum)

    lhs = jnp.concatenate([a_t, r_t], axis=0).astype(bf16)
    sc_b = _dot_nt(lhs, bd(beta * inv))
    sc_k = _dot_nt(lhs, bd(kd * inv))
    s0 = s_scr[...]
    x0 = _dot_nt(lhs, s0.astype(bf16))

    n_mat = jnp.where(strict, sc_b[:c_len], 0.0)
    ak = jnp.where(strict, sc_k[:c_len], 0.0)
    rbm = jnp.where(incl, sc_b[c_len:], 0.0)
    rkm = jnp.where(incl, sc_k[c_len:], 0.0)

    v_bd = bd(v)
    x = x0[:c_len] + _dot(ak.astype(bf16), v_bd)
    t_inv = jnp.where(t_s == i_s, 1.0, 0.0) + jnp.where((t_s >> 1) == (i_s >> 1), n_mat, 0.0)
    for lb in range(1, int(math.log2(c_len))):
        off_diag = ((t_s >> (lb + 1)) == (i_s >> (lb + 1))) & ((t_s >> lb) != (i_s >> lb))
        half = _dot(t_inv.astype(bf16), bd(jnp.where(off_diag, n_mat, 0.0)))
        t_inv = t_inv + _dot(half.astype(bf16), bd(t_inv))
    u = _dot(t_inv.astype(bf16), bd(x))
    u_bd = bd(u)
    y_ref[...] = x0[c_len:] + _dot(rbm.astype(bf16), u_bd) + _dot(rkm.astype(bf16), v_bd)

    s_new = s0 * jnp.exp(total) + _dot_tn(u_bd, bd(beta * rem)) + _dot_tn(v_bd, bd(kd * rem))
    s_scr[...] = s_new

    @pl.when(c == pl.num_programs(3) - 1)
    def _():
        sout_ref[...] = s_new


def _wkv_scan(r, k, v, a2, lw2, p, s0_bd, row0, n_seq, t_len):
    d = D_MODEL
    c_len = WKV_CHUNK
    nc = t_len // c_len
    b0 = row0 // c_len
    n_pair = d // LANES
    has_init = s0_bd is not None

    def blk(s, pr, dr, c):
        return s * nc + jnp.where(dr == 0, c, nc - 1 - c)

    tok = pl.BlockSpec((c_len, LANES), lambda s, pr, dr, c: (b0 + blk(s, pr, dr, c), pr))
    tok_z = pl.BlockSpec((None, c_len, LANES), lambda s, pr, dr, c: (dr, b0 + blk(s, pr, dr, c), pr))
    vec = pl.BlockSpec((1, LANES), lambda s, pr, dr, c: (0, pr))
    st = pl.BlockSpec((None, None, None, LANES, LANES), lambda s, pr, dr, c: (s, dr, pr, 0, 0))
    in_specs = [tok, tok, tok, tok_z, tok_z, vec, vec]
    args = [r, k, v, a2, lw2, p['k_k'].reshape(1, d), p['k_a'].reshape(1, d)]
    if has_init:
        in_specs.append(st)
        args.append(s0_bd)
    return pl.pallas_call(
        functools.partial(_wkv_kernel, has_init=has_init),
        grid=(n_seq, n_pair, 2, nc),
        in_specs=in_specs,
        out_specs=[
            pl.BlockSpec((None, c_len, LANES), lambda s, pr, dr, c: (dr, blk(s, pr, dr, c), pr)),
            st,
        ],
        out_shape=[jax.ShapeDtypeStruct((2, n_seq * t_len, d), f32),
                   jax.ShapeDtypeStruct((n_seq, 2, n_pair, LANES, LANES), f32)],
        scratch_shapes=[pltpu.VMEM((LANES, LANES), f32)],
        compiler_params=_params(("parallel", "parallel", "arbitrary", "arbitrary")),
        name="wkv_scan",
    )(*args)


def _rwkv_post_kernel(y_ref, r_ref, k_ref, v_ref, g_ref, a_ref, ka_ref, rk_ref, lnw_ref, lnb_ref, o_ref):
    y = y_ref[0] + y_ref[1]
    mu = _head_sum(y) * (1.0 / HEAD)
    yc = y - mu
    var = _head_sum(yc * yc) * (1.0 / HEAD)
    yn = yc * lax.rsqrt(var + RWKV_GN_EPS) * lnw_ref[...] + lnb_ref[...]
    k = k_ref[...]
    ka = ka_ref[...]
    kd_sum = k * (1.0 + (a_ref[0] - 1.0) * ka) + k * (1.0 + (a_ref[1] - 1.0) * ka)
    bonus = _head_sum(r_ref[...] * kd_sum * rk_ref[...]) * v_ref[...]
    o_ref[...] = (yn + bonus) * g_ref[...]


def _rwkv_post(y2, r, k, v, g, a2, p, tm=512):
    n, d = r.shape
    tok = pl.BlockSpec((tm, LANES), lambda i, pr: (i, pr))
    tok_z = pl.BlockSpec((2, tm, LANES), lambda i, pr: (0, i, pr))
    vec = pl.BlockSpec((1, LANES), lambda i, pr: (0, pr))
    row = lambda a: a.reshape(1, d)
    return pl.pallas_call(
        _rwkv_post_kernel,
        grid=(n // tm, d // LANES),
        in_specs=[tok_z, tok, tok, tok, tok, tok_z, vec, vec, vec, vec],
        out_specs=tok,
        out_shape=jax.ShapeDtypeStruct((n, d), f32),
        compiler_params=_params(("parallel", "parallel")),
        name="rwkv_post",
    )(y2, r, k, v, g, a2, row(p['k_a']), row(p['r_k']), row(p['ln_w']), row(p['ln_b']))


def _block_diag_states(s):
    b, two, h, nv, nk = s.shape
    s = s.reshape(b, two, h // 2, 2, nv, nk)
    z = jnp.zeros_like(s[:, :, :, 0])
    top = jnp.concatenate([s[:, :, :, 0], z], axis=-1)
    bot = jnp.concatenate([z, s[:, :, :, 1]], axis=-1)
    return jnp.concatenate([top, bot], axis=-2)


def _diag_states(s_bd):
    b, two, hp, _, _ = s_bd.shape
    s0 = s_bd[:, :, :, :HEAD, :HEAD]
    s1 = s_bd[:, :, :, HEAD:, HEAD:]
    return jnp.stack([s0, s1], axis=3).reshape(b, two, 2 * hp, HEAD, HEAD)


def _rwkv_mixer(x, mod, state0, p, lay):
    r, k, v, g, a2, lw2 = _rwkv_proj(x, mod, p, lay)
    y_ctx, s_ctx = _wkv_scan(r, k, v, a2, lw2, p, None, 0, lay.b_ctx, lay.t_ctx)
    y_lat, _ = _wkv_scan(r, k, v, a2, lw2, p, _block_diag_states(state0), lay.n_ctx, lay.b_lat, lay.t_lat)
    y2 = jnp.concatenate([y_ctx, y_lat], axis=1)
    return _rwkv_post(y2, r, k, v, g, a2, p), _diag_states(s_ctx)


def _rope_table(n_tok, width):
    n_rows = n_tok // GRID_W
    rows = jnp.broadcast_to(jnp.arange(n_rows, dtype=f32)[:, None], (n_rows, GRID_W)).reshape(-1)
    cols = jnp.broadcast_to(jnp.arange(GRID_W, dtype=f32)[None, :], (n_rows, GRID_W)).reshape(-1)
    n_freq = HEAD // 4
    inv = ROPE_BASE ** (-jnp.arange(n_freq, dtype=f32) / n_freq)
    ang = jnp.concatenate([rows[:, None] * inv, cols[:, None] * inv], axis=-1)
    reps = width // (HEAD // 2)
    return jnp.tile(jnp.cos(ang), (1, reps)), jnp.tile(jnp.sin(ang), (1, reps))


def kernel(x_prompt, x_sample, cache_da_k, cache_da_v, state_rwkv, cache_swa_k, cache_swa_v, c, c_ctx, ada_w, ada_b, mlp_w1, mlp_w2, da_wqkv, da_q_norm, da_k_norm, da_lambda, da_subln, da_wo, rwkv_mu, rwkv_wrkv, rwkv_w0, rwkv_w1, rwkv_w2, rwkv_a0, rwkv_a1, rwkv_a2, rwkv_g1, rwkv_g2, rwkv_k_k, rwkv_k_a, rwkv_r_k, rwkv_ln_w, rwkv_ln_b, rwkv_wo, swa_wqkv, swa_q_norm, swa_k_norm, swa_sink, swa_wo):
    b_ctx, t_ctx, d = x_prompt.shape
    b_lat, t_lat, _ = x_sample.shape
    lay = Layout(b_ctx, t_ctx, b_lat, t_lat)
    depth = ada_w.shape[0]
    assert d == D_MODEL and lay.n_ctx % t_lat == 0

    x = jnp.concatenate([x_prompt.reshape(lay.n_ctx, d), x_sample.reshape(b_lat * t_lat, d)], axis=0)
    cond = jnp.concatenate([c_ctx[None, :], c, jnp.zeros((8 - 1 - b_lat, d), f32)], axis=0)
    mods = _modulation(cond, ada_w, ada_b)
    mods = mods[:, :1 + b_lat].reshape(depth, 1 + b_lat, 6, 1, d)

    rope_da = _rope_table(t_lat, LANES)
    rope_swa = _rope_table(t_lat, d)

    da_k, da_v, rwkv_s, swa_k, swa_v = [], [], [], [], []
    for i in range(depth):
        kind = i % N_MIXERS
        j = i // N_MIXERS
        mod = mods[i]
        if kind == 0:
            p = {'q_norm': da_q_norm[j], 'k_norm': da_k_norm[j], 'lam': da_lambda[j], 'subln': da_subln[j]}
            lam_init = 0.8 - 0.6 * math.exp(-0.3 * i)
            qkv = _ln_matmul(x, mod, da_wqkv[j].astype(bf16), lay)
            o, k_new = _da_mixer(qkv, cache_da_k, cache_da_v, j, p, lam_init, rope_da, lay)
            da_k.append(k_new.reshape(b_ctx, t_ctx, DA_HEADS, LANES))
            da_v.append(qkv[:lay.n_ctx, 2 * d:].reshape(b_ctx, t_ctx, DA_HEADS, LANES))
            wo = da_wo[j]
        elif kind == 1:
            p = {'mu': rwkv_mu[j], 'wrkv': rwkv_wrkv[j], 'w0': rwkv_w0[j], 'w1': rwkv_w1[j],
                 'w2': rwkv_w2[j], 'a0': rwkv_a0[j], 'a1': rwkv_a1[j], 'a2': rwkv_a2[j],
                 'g1': rwkv_g1[j], 'g2': rwkv_g2[j], 'k_k': rwkv_k_k[j], 'k_a': rwkv_k_a[j],
                 'r_k': rwkv_r_k[j], 'ln_w': rwkv_ln_w[j], 'ln_b': rwkv_ln_b[j]}
            o, s_new = _rwkv_mixer(x, mod, state_rwkv[:, j], p, lay)
            rwkv_s.append(s_new)
            wo = rwkv_wo[j]
        else:
            p = {'q_norm': swa_q_norm[j], 'k_norm': swa_k_norm[j], 'sink': swa_sink[j]}
            qkv = _ln_matmul(x, mod, swa_wqkv[j].astype(bf16), lay)
            o, k_new = _swa_mixer(qkv, cache_swa_k, cache_swa_v, j, p, rope_swa, lay)
            swa_k.append(k_new.reshape(b_ctx, t_ctx, SWA_KV_HEADS, HEAD))
            swa_v.append(qkv[:lay.n_ctx, d + SWA_KV:].reshape(b_ctx, t_ctx, SWA_KV_HEADS, HEAD))
            wo = swa_wo[j]
        x = _matmul_residual(x, o, mod, wo.astype(bf16), lay)
        x = _mlp(x, mod, mlp_w1[i].astype(bf16), mlp_w2[i].astype(bf16), lay)

    y_prompt = x[:lay.n_ctx].reshape(b_ctx, t_ctx, d)
    y_sample = x[lay.n_ctx:].reshape(b_lat, t_lat, d)
    return (y_prompt, y_sample, jnp.stack(da_k, axis=1), jnp.stack(da_v, axis=1),
            jnp.stack(rwkv_s, axis=1), jnp.stack(swa_k, axis=1), jnp.stack(swa_v, axis=1))
```

```python
import functools
import math
from typing import NamedTuple

import jax
import jax.numpy as jnp
from jax import lax
from jax.experimental import pallas as pl
from jax.experimental.pallas import tpu as pltpu

f32 = jnp.float32
bf16 = jnp.bfloat16

D_MODEL = 1024
N_MIXERS = 3
GRID_W = 64
ROPE_BASE = 10000.0
NORM_EPS = 1e-6
NEG_INF = -1e30
HEAD = 64
LANES = 128
DA_HEADS = D_MODEL // LANES
RWKV_HEADS = D_MODEL // HEAD
RWKV_GN_EPS = 64e-5
SWA_HEADS = D_MODEL // HEAD
SWA_KV_HEADS = SWA_HEADS // 4
SWA_GROUP = SWA_HEADS // SWA_KV_HEADS
SWA_KV = SWA_KV_HEADS * HEAD
WINDOW = 128
WKV_CHUNK = 64
VMEM_LIMIT = 56 * 1024 * 1024


class Layout(NamedTuple):
    b_ctx: int
    t_ctx: int
    b_lat: int
    t_lat: int

    @property
    def n_ctx(self):
        return self.b_ctx * self.t_ctx

    @property
    def n(self):
        return self.n_ctx + self.b_lat * self.t_lat


def _params(sem):
    return pltpu.CompilerParams(dimension_semantics=sem, vmem_limit_bytes=VMEM_LIMIT)


def _group_of_tile(i, tm, lay):
    r = i * tm
    return jnp.where(r < lay.n_ctx, 0, 1 + (r - lay.n_ctx) // lay.t_lat)


def _mod_spec(chunk, tm, lay):
    return pl.BlockSpec((None, None, 1, D_MODEL),
                        lambda i, *_: (_group_of_tile(i, tm, lay), chunk, 0, 0))


def _dot(a, b):
    return jnp.dot(a, b, preferred_element_type=f32)


def _dot_nt(a, b):
    return lax.dot_general(a, b, (((1,), (1,)), ((), ())), preferred_element_type=f32)


def _dot_tn(a, b):
    return lax.dot_general(a, b, (((0,), (0,)), ((), ())), preferred_element_type=f32)


def _split3(x):
    hi = x.astype(bf16)
    r1 = x - hi.astype(f32)
    mid = r1.astype(bf16)
    lo = (r1 - mid.astype(f32)).astype(bf16)
    return hi, mid, lo


def _dot_exact_rhs(a_bf16, x):
    hi, mid, lo = _split3(x)
    return _dot(a_bf16, hi) + (_dot(a_bf16, mid) + _dot(a_bf16, lo))


def _head_ones():
    r = lax.broadcasted_iota(jnp.int32, (LANES, LANES), 0)
    c = lax.broadcasted_iota(jnp.int32, (LANES, LANES), 1)
    return ((r // HEAD) == (c // HEAD)).astype(bf16)


def _head_sum(x):
    g = _head_ones()
    hi = x.astype(bf16)
    lo = (x - hi.astype(f32)).astype(bf16)
    cols = []
    for p in range(x.shape[1] // LANES):
        cs = slice(p * LANES, (p + 1) * LANES)
        cols.append(_dot(hi[:, cs], g) + _dot(lo[:, cs], g))
    return cols[0] if len(cols) == 1 else jnp.concatenate(cols, axis=1)


def _head_rms(x, gain):
    ms = _head_sum(x * x) * (1.0 / HEAD)
    return x * lax.rsqrt(ms + NORM_EPS) * gain


def _adaln(x, shift, scale):
    ms = jnp.mean(x * x, axis=-1, keepdims=True)
    return x * lax.rsqrt(ms + NORM_EPS) * (1.0 + scale) + shift


def _rope(x, cos, sin):
    w = x.shape[1]
    lane = lax.broadcasted_iota(jnp.int32, x.shape, 1)
    low = (lane & (HEAD - 1)) < HEAD // 2
    up = pltpu.roll(x, w - HEAD // 2, 1)
    dn = pltpu.roll(x, HEAD // 2, 1)
    return x * cos + jnp.where(low, -up, dn) * sin


def _softplus(y):
    return jnp.maximum(y, 0.0) + jnp.log(1.0 + jnp.exp(-jnp.abs(y)))


def _mod_kernel(c_ref, w_ref, b_ref, o_ref):
    c = c_ref[...]
    s = c * jax.nn.sigmoid(c)
    sh, sm, sl = _split3(s)
    wh, wm, wl = _split3(w_ref[...])
    acc = _dot(sh, wh) + (_dot(sh, wm) + _dot(sm, wh)) + (_dot(sh, wl) + _dot(sm, wm) + _dot(sl, wh))
    o_ref[...] = acc + b_ref[...]


def _modulation(cond8, ada_w, ada_b):
    depth, d, d6 = ada_w.shape
    tn = 1536
    return pl.pallas_call(
        _mod_kernel,
        grid=(depth, d6 // tn),
        in_specs=[
            pl.BlockSpec((8, d), lambda l, j: (0, 0)),
            pl.BlockSpec((None, d, tn), lambda l, j: (l, 0, j)),
            pl.BlockSpec((None, 1, tn), lambda l, j: (l, 0, j)),
        ],
        out_specs=pl.BlockSpec((None, 8, tn), lambda l, j: (l, 0, j)),
        out_shape=jax.ShapeDtypeStruct((depth, 8, d6), f32),
        compiler_params=_params(("parallel", "parallel")),
        name="modulation",
    )(cond8, ada_w, ada_b.reshape(depth, 1, d6))


def _ln_mm_kernel(x_ref, sh_ref, sc_ref, w_ref, o_ref, *, tn):
    h = _adaln(x_ref[...], sh_ref[...], sc_ref[...]).astype(bf16)
    for j in range(w_ref.shape[1] // tn):
        o_ref[:, j * tn:(j + 1) * tn] = _dot(h, w_ref[:, j * tn:(j + 1) * tn])


def _ln_matmul(x, mod, w_bf16, lay, tm=512, tn=512):
    n, d = x.shape
    n_out = w_bf16.shape[1]
    return pl.pallas_call(
        functools.partial(_ln_mm_kernel, tn=tn),
        grid=(n // tm,),
        in_specs=[
            pl.BlockSpec((tm, d), lambda i: (i, 0)),
            _mod_spec(0, tm, lay),
            _mod_spec(1, tm, lay),
            pl.BlockSpec((d, n_out), lambda i: (0, 0)),
        ],
        out_specs=pl.BlockSpec((tm, n_out), lambda i: (i, 0)),
        out_shape=jax.ShapeDtypeStruct((n, n_out), f32),
        compiler_params=_params(("parallel",)),
        name="adaln_proj",
    )(x, mod, mod, w_bf16)


def _split_row_specs(block, tm, lay):
    nb_ctx = lay.n_ctx // tm
    lead = (0,) * (len(block) - 2)
    ctx = pl.BlockSpec(block, lambda i, *_: lead + (jnp.minimum(i, nb_ctx - 1), 0))
    lat = pl.BlockSpec(block, lambda i, *_: lead + (jnp.maximum(i - nb_ctx, 0), 0))
    return ctx, lat


def _mm_res_kernel(x_ref, ac_ref, al_ref, g_ref, w_ref, o_ref, *, nb_ctx):
    a = jnp.where(pl.program_id(0) < nb_ctx, ac_ref[...], al_ref[...])
    o_ref[...] = x_ref[...] + g_ref[...] * _dot(a.astype(bf16), w_ref[...])


def _matmul_residual(x, a_ctx, a_lat, mod, w_bf16, lay, tm=512):
    n, d = x.shape
    k_in = a_ctx.shape[1]
    ctx_spec, lat_spec = _split_row_specs((tm, k_in), tm, lay)
    return pl.pallas_call(
        functools.partial(_mm_res_kernel, nb_ctx=lay.n_ctx // tm),
        grid=(n // tm,),
        in_specs=[
            pl.BlockSpec((tm, d), lambda i: (i, 0)),
            ctx_spec, lat_spec,
            _mod_spec(2, tm, lay),
            pl.BlockSpec((k_in, d), lambda i: (0, 0)),
        ],
        out_specs=pl.BlockSpec((tm, d), lambda i: (i, 0)),
        out_shape=jax.ShapeDtypeStruct((n, d), f32),
        compiler_params=_params(("parallel",)),
        name="out_proj_residual",
    )(x, a_ctx, a_lat, mod, w_bf16)


def _mlp_kernel(x_ref, sh_ref, sc_ref, g_ref, w1_ref, w2_ref, o_ref, acc_ref, *, fc):
    x = x_ref[...]
    h = _adaln(x, sh_ref[...], sc_ref[...]).astype(bf16)
    d_ff = w1_ref.shape[1]
    for c in range(d_ff // fc):
        a = _dot(h, w1_ref[:, c * fc:(c + 1) * fc])
        a = jnp.square(jnp.maximum(a, 0.0)).astype(bf16)
        part = _dot(a, w2_ref[c * fc:(c + 1) * fc, :])
        if c == 0:
            acc_ref[...] = part
        else:
            acc_ref[...] += part
    o_ref[...] = x + g_ref[...] * acc_ref[...]


def _mlp(x, mod, w1_bf16, w2_bf16, lay, tm=512, fc=1024):
    n, d = x.shape
    d_ff = w1_bf16.shape[1]
    return pl.pallas_call(
        functools.partial(_mlp_kernel, fc=fc),
        grid=(n // tm,),
        in_specs=[
            pl.BlockSpec((tm, d), lambda i: (i, 0)),
            _mod_spec(3, tm, lay),
            _mod_spec(4, tm, lay),
            _mod_spec(5, tm, lay),
            pl.BlockSpec((d, d_ff), lambda i: (0, 0)),
            pl.BlockSpec((d_ff, d), lambda i: (0, 0)),
        ],
        out_specs=pl.BlockSpec((tm, d), lambda i: (i, 0)),
        out_shape=jax.ShapeDtypeStruct((n, d), f32),
        scratch_shapes=[pltpu.VMEM((tm, d), f32)],
        compiler_params=_params(("parallel",)),
        name="mlp",
    )(x, mod, mod, mod, w1_bf16, w2_bf16)


def _da_lambda(lam_ref, lam_init):
    lp = lam_ref[...]
    e1 = jnp.exp(jnp.sum(lp[0:1] * lp[1:2], axis=-1, keepdims=True))
    e2 = jnp.exp(jnp.sum(lp[2:3] * lp[3:4], axis=-1, keepdims=True))
    return e1 - e2 + lam_init


def _da_qk(q, k_bf16):
    lane = lax.broadcasted_iota(jnp.int32, q.shape, 1)
    return [_dot_nt(jnp.where((lane < HEAD) == (m == 0), q, 0.0).astype(bf16), k_bf16) for m in range(2)]


def _da_pv(scores, v_bf16, lam):
    outs = []
    for s in scores:
        e = jnp.exp(s - jnp.max(s, axis=-1, keepdims=True))
        inv = 1.0 / jnp.sum(e, axis=-1, keepdims=True)
        outs.append(_dot(e.astype(bf16), v_bf16) * inv)
    return outs[0] - lam * outs[1]


def _da_finish(o, sub_ref, lam_init):
    ms = jnp.mean(o * o, axis=-1, keepdims=True)
    return o * lax.rsqrt(ms + NORM_EPS) * sub_ref[...] * (1.0 - lam_init)


def _da_ctx_kernel(lam_ref, q_ref, k_ref, v_ref, qn_ref, kn_ref, sub_ref, o_ref, knew_ref, *, lam_init):
    lam = _da_lambda(lam_ref, lam_init)
    q = _head_rms(q_ref[...], qn_ref[...]) * (HEAD ** -0.5)
    k = _head_rms(k_ref[...], kn_ref[...])
    knew_ref[...] = k
    kb = k.astype(bf16)
    vb = v_ref[...].astype(bf16)
    cols = [slice(h * LANES, (h + 1) * LANES) for h in range(q.shape[1] // LANES)]
    s_next = _da_qk(q[:, cols[0]], kb[:, cols[0]])
    for h, cs in enumerate(cols):
        s_cur = s_next
        if h + 1 < len(cols):
            s_next = _da_qk(q[:, cols[h + 1]], kb[:, cols[h + 1]])
        o_ref[:, cs] = _da_finish(_da_pv(s_cur, vb[:, cs], lam), sub_ref, lam_init)


def _da_lat_kernel(lam_ref, q_ref, k_ref, v_ref, ck_ref, cv_ref, qn_ref, kn_ref, sub_ref,
                   cq_ref, sq_ref, ckk_ref, skk_ref, o_ref, kf_scr, vf_scr, *, lam_init, past):
    @pl.when(pl.program_id(2) == 0)
    def _():
        k = _rope(_head_rms(k_ref[...], kn_ref[...]), ckk_ref[...], skk_ref[...])
        kf_scr[0:past, :] = ck_ref[...].astype(bf16)
        kf_scr[past:, :] = k.astype(bf16)
        vf_scr[0:past, :] = cv_ref[...].astype(bf16)
        vf_scr[past:, :] = v_ref[...].astype(bf16)

    lam = _da_lambda(lam_ref, lam_init)
    q = _rope(_head_rms(q_ref[...], qn_ref[...]), cq_ref[...], sq_ref[...]) * (HEAD ** -0.5)
    o = _da_pv(_da_qk(q, kf_scr[...]), vf_scr[...], lam)
    o_ref[...] = _da_finish(o, sub_ref, lam_init)


def _da_mixer(qkv, cache_k, cache_v, j, p, lam_init, rope_tab, lay, tq=256):
    n = qkv.shape[0]
    d = D_MODEL
    nh = DA_HEADS
    qn = jnp.tile(p['q_norm'], 2).reshape(1, LANES)
    kn = jnp.tile(p['k_norm'], 2).reshape(1, LANES)
    qn_all = jnp.tile(qn, (1, nh))
    kn_all = jnp.tile(kn, (1, nh))
    sub = p['subln'].reshape(1, LANES)
    small = lambda shape: pl.BlockSpec(shape, lambda *_: (0,) * len(shape))
    t = lay.t_ctx
    o_ctx, k_new = pl.pallas_call(
        functools.partial(_da_ctx_kernel, lam_init=lam_init),
        grid=(lay.b_ctx,),
        in_specs=[
            small((4, HEAD)),
            pl.BlockSpec((t, d), lambda b: (b, 0)),
            pl.BlockSpec((t, d), lambda b: (b, 1)),
            pl.BlockSpec((t, d), lambda b: (b, 2)),
            small((1, d)), small((1, d)), small((1, LANES)),
        ],
        out_specs=[
            pl.BlockSpec((t, d), lambda b: (b, 0)),
            pl.BlockSpec((t, d), lambda b: (b, 0)),
        ],
        out_shape=[jax.ShapeDtypeStruct((lay.n_ctx, d), f32)] * 2,
        compiler_params=_params(("parallel",)),
        name="da_ctx",
    )(p['lam'], qkv, qkv, qkv, qn_all, kn_all, sub)

    tl = lay.t_lat
    past = cache_k.shape[2]
    row0 = lay.n_ctx // tl
    q0 = lay.n_ctx // tq
    nq = tl // tq
    cos, sin = rope_tab
    ck = cache_k.reshape(lay.b_lat, cache_k.shape[1], past, d)
    cv = cache_v.reshape(lay.b_lat, cache_v.shape[1], past, d)
    o_lat = pl.pallas_call(
        functools.partial(_da_lat_kernel, lam_init=lam_init, past=past),
        grid=(lay.b_lat, nh, nq),
        in_specs=[
            small((4, HEAD)),
            pl.BlockSpec((tq, LANES), lambda b, h, i: (q0 + b * nq + i, h)),
            pl.BlockSpec((tl, LANES), lambda b, h, i: (row0 + b, nh + h)),
            pl.BlockSpec((tl, LANES), lambda b, h, i: (row0 + b, 2 * nh + h)),
            pl.BlockSpec((None, None, past, LANES), lambda b, h, i: (b, j, 0, h)),
            pl.BlockSpec((None, None, past, LANES), lambda b, h, i: (b, j, 0, h)),
            small((1, LANES)), small((1, LANES)), small((1, LANES)),
            pl.BlockSpec((tq, LANES), lambda b, h, i: (i, 0)),
            pl.BlockSpec((tq, LANES), lambda b, h, i: (i, 0)),
            small((tl, LANES)), small((tl, LANES)),
        ],
        out_specs=pl.BlockSpec((tq, LANES), lambda b, h, i: (b * nq + i, h)),
        out_shape=jax.ShapeDtypeStruct((lay.b_lat * tl, d), f32),
        scratch_shapes=[pltpu.VMEM((past + tl, LANES), bf16), pltpu.VMEM((past + tl, LANES), bf16)],
        compiler_params=_params(("parallel", "parallel", "arbitrary")),
        name="da_lat",
    )(p['lam'], qkv, qkv, qkv, ck, cv, qn, kn, sub, cos, sin, cos, sin)
    return o_ctx, o_lat, k_new


def _sink_column(sink_ref, j, rows):
    parts = [jnp.full((rows, 1), sink_ref[j * SWA_GROUP + g], f32) for g in range(SWA_GROUP)]
    return jnp.concatenate(parts, axis=0)


def _stack_group(x, j):
    return jnp.concatenate(
        [x[:, (j * SWA_GROUP + g) * HEAD:(j * SWA_GROUP + g + 1) * HEAD] for g in range(SWA_GROUP)],
        axis=0).astype(bf16)


def _swa_ctx_kernel(sink_ref, q_ref, k_ref, v_ref, qn_ref, kn_ref, o_ref, knew_ref):
    t = q_ref.shape[0]
    q = _head_rms(q_ref[...], qn_ref[...]) * (HEAD ** -0.5)
    k = _head_rms(k_ref[...], kn_ref[...])
    knew_ref[...] = k
    v = v_ref[...]
    for j in range(SWA_KV_HEADS):
        hs = slice(j * HEAD, (j + 1) * HEAD)
        qs = _stack_group(q, j)
        s = _dot_nt(qs, k[:, hs].astype(bf16))
        sk = _sink_column(sink_ref, j, t)
        m = jnp.maximum(jnp.max(s, axis=-1, keepdims=True), sk)
        e = jnp.exp(s - m)
        den = jnp.sum(e, axis=-1, keepdims=True) + jnp.exp(sk - m)
        o = _dot(e.astype(bf16), v[:, hs].astype(bf16)) * (1.0 / den)
        for g in range(SWA_GROUP):
            h = j * SWA_GROUP + g
            o_ref[:, h * HEAD:(h + 1) * HEAD] = o[g * t:(g + 1) * t]


def _swa_lat_kernel(sink_ref, q_ref, k_ref, v_ref, ck_ref, cv_ref, qn_ref, kn_ref,
                    cq_ref, sq_ref, ckk_ref, skk_ref, o_ref, kp_scr, vp_scr, *, t_lat):
    n = pl.program_id(1)
    w = WINDOW

    @pl.when(n == 0)
    def _():
        zeros = jnp.zeros((w, SWA_KV), f32)
        kp_scr[0:w, :] = zeros
        kp_scr[w:w + t_lat, :] = _rope(_head_rms(k_ref[...], kn_ref[...]), ckk_ref[...], skk_ref[...])
        kp_scr[w + t_lat:, :] = zeros
        vp_scr[0:w, :] = zeros
        vp_scr[w:w + t_lat, :] = v_ref[...]
        vp_scr[w + t_lat:, :] = zeros

    q = _rope(_head_rms(q_ref[...], qn_ref[...]), cq_ref[...], sq_ref[...]) * (HEAD ** -0.5)
    start = pl.multiple_of(n * w, w)
    kband = kp_scr[pl.ds(start, 3 * w), :]
    vband = vp_scr[pl.ds(start, 3 * w), :]
    ckb = ck_ref[...]
    cvb = cv_ref[...]
    rows = SWA_GROUP * w
    qi = lax.broadcasted_iota(jnp.int32, (rows, 3 * w), 0) & (w - 1)
    si = lax.broadcasted_iota(jnp.int32, (rows, 3 * w), 1)
    rel = si - qi
    key_pos = (n - 1) * w + si
    valid = (rel >= 0) & (rel <= 2 * w) & (key_pos >= 0) & (key_pos < t_lat)
    for j in range(SWA_KV_HEADS):
        hs = slice(j * HEAD, (j + 1) * HEAD)
        qs = _stack_group(q, j)
        s_loc = jnp.where(valid, _dot_nt(qs, kband[:, hs].astype(bf16)), NEG_INF)
        s_ctx = _dot_nt(qs, ckb[:, hs].astype(bf16))
        sk = _sink_column(sink_ref, j, w)
        m = jnp.maximum(jnp.maximum(jnp.max(s_loc, axis=-1, keepdims=True),
                                    jnp.max(s_ctx, axis=-1, keepdims=True)), sk)
        e_loc = jnp.exp(s_loc - m)
        e_ctx = jnp.exp(s_ctx - m)
        den = (jnp.sum(e_loc, axis=-1, keepdims=True) + jnp.sum(e_ctx, axis=-1, keepdims=True)
               + jnp.exp(sk - m))
        o = (_dot(e_ctx.astype(bf16), cvb[:, hs].astype(bf16))
             + _dot(e_loc.astype(bf16), vband[:, hs].astype(bf16))) * (1.0 / den)
        for g in range(SWA_GROUP):
            h = j * SWA_GROUP + g
            o_ref[:, h * HEAD:(h + 1) * HEAD] = o[g * w:(g + 1) * w]


def _swa_mixer(qkv, cache_k, cache_v, j, p, rope_tab, lay):
    d = D_MODEL
    qn = jnp.tile(p['q_norm'], d // HEAD).reshape(1, d)
    kn = jnp.tile(p['k_norm'], SWA_KV_HEADS).reshape(1, SWA_KV)
    small = lambda shape: pl.BlockSpec(shape, lambda *_: (0,) * len(shape))
    smem = pl.BlockSpec(memory_space=pltpu.SMEM)
    t = lay.t_ctx
    kcol = d // SWA_KV
    o_ctx, k_new = pl.pallas_call(
        _swa_ctx_kernel,
        grid=(lay.b_ctx,),
        in_specs=[
            smem,
            pl.BlockSpec((t, d), lambda b: (b, 0)),
            pl.BlockSpec((t, SWA_KV), lambda b: (b, kcol)),
            pl.BlockSpec((t, SWA_KV), lambda b: (b, kcol + 1)),
            small((1, d)), small((1, SWA_KV)),
        ],
        out_specs=[
            pl.BlockSpec((t, d), lambda b: (b, 0)),
            pl.BlockSpec((t, SWA_KV), lambda b: (b, 0)),
        ],
        out_shape=[jax.ShapeDtypeStruct((lay.n_ctx, d), f32),
                   jax.ShapeDtypeStruct((lay.n_ctx, SWA_KV), f32)],
        compiler_params=_params(("parallel",)),
        name="swa_ctx",
    )(p['sink'], qkv, qkv, qkv, qn, kn)

    tl = lay.t_lat
    w = WINDOW
    past = cache_k.shape[2]
    row0 = lay.n_ctx // tl
    q0 = lay.n_ctx // w
    nq = tl // w
    cos, sin = rope_tab
    ck = cache_k.reshape(lay.b_lat, cache_k.shape[1], past, SWA_KV)
    cv = cache_v.reshape(lay.b_lat, cache_v.shape[1], past, SWA_KV)
    o_lat = pl.pallas_call(
        functools.partial(_swa_lat_kernel, t_lat=tl),
        grid=(lay.b_lat, nq),
        in_specs=[
            smem,
            pl.BlockSpec((w, d), lambda b, i: (q0 + b * nq + i, 0)),
            pl.BlockSpec((tl, SWA_KV), lambda b, i: (row0 + b, kcol)),
            pl.BlockSpec((tl, SWA_KV), lambda b, i: (row0 + b, kcol + 1)),
            pl.BlockSpec((None, None, past, SWA_KV), lambda b, i: (b, j, 0, 0)),
            pl.BlockSpec((None, None, past, SWA_KV), lambda b, i: (b, j, 0, 0)),
            small((1, d)), small((1, SWA_KV)),
            pl.BlockSpec((w, d), lambda b, i: (i, 0)),
            pl.BlockSpec((w, d), lambda b, i: (i, 0)),
            pl.BlockSpec((tl, SWA_KV), lambda b, i: (0, 0)),
            pl.BlockSpec((tl, SWA_KV), lambda b, i: (0, 0)),
        ],
        out_specs=pl.BlockSpec((w, d), lambda b, i: (b * nq + i, 0)),
        out_shape=jax.ShapeDtypeStruct((lay.b_lat * tl, d), f32),
        scratch_shapes=[pltpu.VMEM((tl + 2 * w, SWA_KV), f32), pltpu.VMEM((tl + 2 * w, SWA_KV), f32)],
        compiler_params=_params(("parallel", "arbitrary")),
        name="swa_lat",
    )(p['sink'], qkv, qkv, qkv, ck, cv, qn, kn, cos, sin, cos, sin)
    return o_ctx, o_lat, k_new


def _rwkv_proj_kernel(x_ref, xp_ref, xn_ref, sh_ref, sc_ref, mu_ref, wrkv_ref, w1_ref, w2_ref, w0_ref,
                      a1_ref, a2_ref, a0_ref, g1_ref, g2_ref,
                      r_ref, k_ref, v_ref, g_ref, a_ref, lw_ref, *, lay, tm):
    i = pl.program_id(0)
    r0 = i * tm
    in_ctx = r0 < lay.n_ctx
    pos = jnp.where(in_ctx, r0 % lay.t_ctx, (r0 - lay.n_ctx) % lay.t_lat)
    seq_len = jnp.where(in_ctx, lay.t_ctx, lay.t_lat)
    keep_prev = jnp.where(pos == 0, 0.0, 1.0)
    keep_next = jnp.where(pos + tm == seq_len, 0.0, 1.0)

    sh = sh_ref[...]
    sc = sc_ref[...]
    h = _adaln(x_ref[...], sh, sc)
    h_prev_row = _adaln(xp_ref[7:8, :], sh, sc) * keep_prev
    h_next_row = _adaln(xn_ref[0:1, :], sh, sc) * keep_next
    row = lax.broadcasted_iota(jnp.int32, h.shape, 0)
    prev = jnp.where(row == 0, h_prev_row, pltpu.roll(h, 1, 0))
    nxt = jnp.where(row == tm - 1, h_next_row, pltpu.roll(h, tm - 1, 0))
    xx = 0.5 * (prev + nxt) - h

    def mix(m):
        return (h + xx * mu_ref[m:m + 1, :]).astype(bf16)

    r_ref[...] = _dot(mix(0), wrkv_ref[0])
    k_ref[...] = _dot(mix(1), wrkv_ref[1])
    v_ref[...] = _dot(mix(2), wrkv_ref[2])

    lane = lax.broadcasted_iota(jnp.int32, (tm, LANES), 1)
    first = lane < HEAD

    tw = jnp.tanh(_dot(mix(3), w1_ref[...]))
    la = _dot(mix(4), a1_ref[...])
    for z in range(2):
        sel = first if z == 0 else jnp.logical_not(first)
        w_lora = _dot(jnp.where(sel, tw, 0.0).astype(bf16), w2_ref[...])
        w_log = -_softplus(-(w0_ref[z:z + 1, :] + w_lora)) - 0.5
        lw_ref[z] = -jnp.exp(w_log)
        a_lora = _dot(jnp.where(sel, la, 0.0).astype(bf16), a2_ref[...])
        a_ref[z] = jax.nn.sigmoid(a0_ref[z:z + 1, :] + a_lora)

    g_ref[...] = _dot(jax.nn.sigmoid(_dot(mix(5), g1_ref[...])).astype(bf16), g2_ref[...])


def _rwkv_proj(x, mod, p, lay, tm=256):
    n, d = x.shape
    assert lay.t_ctx % tm == 0 and lay.t_lat % tm == 0
    nb8 = n // 8
    full = lambda a: pl.BlockSpec(a.shape, lambda i: (0,) * a.ndim)
    wrkv = p['wrkv'].astype(bf16)
    w1 = jnp.concatenate([p['w1'][0], p['w1'][1]], axis=1).astype(bf16)
    w2 = p['w2'].reshape(2 * p['w2'].shape[1], d).astype(bf16)
    a1 = jnp.concatenate([p['a1'][0], p['a1'][1]], axis=1).astype(bf16)
    a2 = p['a2'].reshape(2 * p['a2'].shape[1], d).astype(bf16)
    g1 = p['g1'].astype(bf16)
    g2 = p['g2'].astype(bf16)
    row_spec = pl.BlockSpec((tm, d), lambda i: (i, 0))
    z_spec = pl.BlockSpec((2, tm, d), lambda i: (0, i, 0))
    args = (x, x, x, mod, mod, p['mu'], wrkv, w1, w2, p['w0'], a1, a2, p['a0'], g1, g2)
    return pl.pallas_call(
        functools.partial(_rwkv_proj_kernel, lay=lay, tm=tm),
        grid=(n // tm,),
        in_specs=[
            row_spec,
            pl.BlockSpec((8, d), lambda i: (jnp.maximum(i * (tm // 8) - 1, 0), 0)),
            pl.BlockSpec((8, d), lambda i: (jnp.minimum((i + 1) * (tm // 8), nb8 - 1), 0)),
            _mod_spec(0, tm, lay),
            _mod_spec(1, tm, lay),
        ] + [full(a) for a in args[5:]],
        out_specs=[row_spec] * 4 + [z_spec] * 2,
        out_shape=[jax.ShapeDtypeStruct((n, d), f32)] * 4 + [jax.ShapeDtypeStruct((2, n, d), f32)] * 2,
        compiler_params=_params(("parallel",)),
        name="rwkv_proj",
    )(*args)


def _wkv_kernel(*refs, has_init):
    if has_init:
        (r_ref, k_ref, v_ref, a_ref, lw_ref, kk_ref, ka_ref, s0_ref, y_ref, sout_ref, s_scr) = refs
    else:
        (r_ref, k_ref, v_ref, a_ref, lw_ref, kk_ref, ka_ref, y_ref, sout_ref, s_scr) = refs
    c_len = WKV_CHUNK
    n_pair = r_ref.shape[1] // LANES
    direction = pl.program_id(1)
    c = pl.program_id(2)

    @pl.when(c == 0)
    def _():
        if has_init:
            s_scr[...] = s0_ref[...]
        else:
            s_scr[...] = jnp.zeros_like(s_scr)

    sgn = jnp.where(direction == 0, 1, -1)
    t_s = lax.broadcasted_iota(jnp.int32, (c_len, LANES), 0)
    i_s = lax.broadcasted_iota(jnp.int32, (c_len, LANES), 1) & (c_len - 1)
    order = (t_s - i_s) * sgn
    strict = order > 0
    incl = order >= 0
    t_c = lax.broadcasted_iota(jnp.int32, (c_len, c_len), 0)
    i_c = lax.broadcasted_iota(jnp.int32, (c_len, c_len), 1)
    tri = (((t_c - i_c) * sgn) >= 0).astype(bf16)
    rb = lax.broadcasted_iota(jnp.int32, (2 * c_len, LANES), 0) // c_len
    cb = lax.broadcasted_iota(jnp.int32, (2 * c_len, LANES), 1) // HEAD
    same_head = rb == cb

    def bd(x):
        return jnp.where(same_head, jnp.concatenate([x, x], axis=0), 0.0).astype(bf16)

    r = r_ref[...]
    k = k_ref[...]
    v = v_ref[...]
    a = a_ref[...]
    lw = lw_ref[...]

    kkr = k * kk_ref[...]
    kk = kkr * lax.rsqrt(jnp.maximum(_head_sum(kkr * kkr), 1e-24))
    beta = kk * a
    kd = k * (1.0 + (a - 1.0) * ka_ref[...])

    cum = _dot_exact_rhs(tri, lw)
    total = jnp.where(direction == 0, cum[c_len - 1:c_len, :], cum[0:1, :])
    a_t = -kk * jnp.exp(cum - lw)
    r_t = r * jnp.exp(cum)
    inv = jnp.exp(-cum)
    rem = jnp.exp(total - cum)
    b_inv = beta * inv
    k_inv = kd * inv
    b_rem = beta * rem
    k_rem = kd * rem
    p_total = jnp.exp(total)
    eye = jnp.where(t_s == i_s, 1.0, 0.0)
    pair_block = (t_s >> 1) == (i_s >> 1)
    levels = []
    for lb in range(1, int(math.log2(c_len))):
        levels.append(((t_s >> (lb + 1)) == (i_s >> (lb + 1))) & ((t_s >> lb) != (i_s >> lb)))

    pairs = range(n_pair)
    cols = [slice(pr * LANES, (pr + 1) * LANES) for pr in pairs]
    lhs = [jnp.concatenate([a_t[:, cs], r_t[:, cs]], axis=0).astype(bf16) for cs in cols]
    sc_b = [_dot_nt(lhs[p], bd(b_inv[:, cols[p]])) for p in pairs]
    sc_k = [_dot_nt(lhs[p], bd(k_inv[:, cols[p]])) for p in pairs]
    s0 = [s_scr[p] for p in pairs]
    x0 = [_dot_nt(lhs[p], s0[p].astype(bf16)) for p in pairs]
    n_mat = [jnp.where(strict, sc_b[p][:c_len], 0.0) for p in pairs]
    v_bd = [bd(v[:, cs]) for cs in cols]
    x = [x0[p][:c_len] + _dot(jnp.where(strict, sc_k[p][:c_len], 0.0).astype(bf16), v_bd[p]) for p in pairs]
    t_inv = [eye + jnp.where(pair_block, n_mat[p], 0.0) for p in pairs]
    for off_diag in levels:
        half = [_dot(t_inv[p].astype(bf16), bd(jnp.where(off_diag, n_mat[p], 0.0))) for p in pairs]
        t_inv = [t_inv[p] + _dot(half[p].astype(bf16), bd(t_inv[p])) for p in pairs]
    u_bd = [bd(_dot(t_inv[p].astype(bf16), bd(x[p]))) for p in pairs]
    for p in pairs:
        rbm = jnp.where(incl, sc_b[p][c_len:], 0.0).astype(bf16)
        rkm = jnp.where(incl, sc_k[p][c_len:], 0.0).astype(bf16)
        y_ref[:, cols[p]] = x0[p][c_len:] + _dot(rbm, u_bd[p]) + _dot(rkm, v_bd[p])
    for p in pairs:
        s_scr[p] = (s0[p] * p_total[:, cols[p]] + _dot_tn(u_bd[p], bd(b_rem[:, cols[p]]))
                    + _dot_tn(v_bd[p], bd(k_rem[:, cols[p]])))

    @pl.when(c == pl.num_programs(2) - 1)
    def _():
        sout_ref[...] = s_scr[...]


def _wkv_scan(r, k, v, a2, lw2, p, s0_bd, row0, n_seq, t_len):
    d = D_MODEL
    c_len = WKV_CHUNK
    nc = t_len // c_len
    b0 = row0 // c_len
    n_pair = d // LANES
    has_init = s0_bd is not None

    def blk(s, dr, c):
        return s * nc + jnp.where(dr == 0, c, nc - 1 - c)

    tok = pl.BlockSpec((c_len, d), lambda s, dr, c: (b0 + blk(s, dr, c), 0))
    tok_z = pl.BlockSpec((None, c_len, d), lambda s, dr, c: (dr, b0 + blk(s, dr, c), 0))
    vec = pl.BlockSpec((1, d), lambda s, dr, c: (0, 0))
    st = pl.BlockSpec((None, None, n_pair, LANES, LANES), lambda s, dr, c: (s, dr, 0, 0, 0))
    in_specs = [tok, tok, tok, tok_z, tok_z, vec, vec]
    args = [r, k, v, a2, lw2, p['k_k'].reshape(1, d), p['k_a'].reshape(1, d)]
    if has_init:
        in_specs.append(st)
        args.append(s0_bd)
    return pl.pallas_call(
        functools.partial(_wkv_kernel, has_init=has_init),
        grid=(n_seq, 2, nc),
        in_specs=in_specs,
        out_specs=[
            pl.BlockSpec((None, c_len, d), lambda s, dr, c: (dr, blk(s, dr, c), 0)),
            st,
        ],
        out_shape=[jax.ShapeDtypeStruct((2, n_seq * t_len, d), f32),
                   jax.ShapeDtypeStruct((n_seq, 2, n_pair, LANES, LANES), f32)],
        scratch_shapes=[pltpu.VMEM((n_pair, LANES, LANES), f32)],
        compiler_params=_params(("parallel", "arbitrary", "arbitrary")),
        name="wkv_scan",
    )(*args)


def _rwkv_post_kernel(x_ref, yc_ref, yl_ref, r_ref, k_ref, v_ref, g_ref, a_ref, ka_ref, rk_ref,
                      lnw_ref, lnb_ref, gate_ref, wo_ref, o_ref, *, nb_ctx):
    y2 = jnp.where(pl.program_id(0) < nb_ctx, yc_ref[...], yl_ref[...])
    y = y2[0] + y2[1]
    mu = _head_sum(y) * (1.0 / HEAD)
    yc = y - mu
    var = _head_sum(yc * yc) * (1.0 / HEAD)
    yn = yc * lax.rsqrt(var + RWKV_GN_EPS) * lnw_ref[...] + lnb_ref[...]
    k = k_ref[...]
    ka = ka_ref[...]
    kd_sum = k * (1.0 + (a_ref[0] - 1.0) * ka) + k * (1.0 + (a_ref[1] - 1.0) * ka)
    bonus = _head_sum(r_ref[...] * kd_sum * rk_ref[...]) * v_ref[...]
    o = ((yn + bonus) * g_ref[...]).astype(bf16)
    o_ref[...] = x_ref[...] + gate_ref[...] * _dot(o, wo_ref[...])


def _rwkv_post_residual(x, mod, y_ctx, y_lat, r, k, v, g, a2, p, wo_bf16, lay, tm=256):
    n, d = r.shape
    tok = pl.BlockSpec((tm, d), lambda i: (i, 0))
    tok_z = pl.BlockSpec((2, tm, d), lambda i: (0, i, 0))
    vec = pl.BlockSpec((1, d), lambda i: (0, 0))
    yc_spec, yl_spec = _split_row_specs((2, tm, d), tm, lay)
    row = lambda a: a.reshape(1, d)
    return pl.pallas_call(
        functools.partial(_rwkv_post_kernel, nb_ctx=lay.n_ctx // tm),
        grid=(n // tm,),
        in_specs=[tok, yc_spec, yl_spec, tok, tok, tok, tok, tok_z, vec, vec, vec, vec,
                  _mod_spec(2, tm, lay), pl.BlockSpec((d, d), lambda i: (0, 0))],
        out_specs=tok,
        out_shape=jax.ShapeDtypeStruct((n, d), f32),
        compiler_params=_params(("parallel",)),
        name="rwkv_post",
    )(x, y_ctx, y_lat, r, k, v, g, a2, row(p['k_a']), row(p['r_k']), row(p['ln_w']), row(p['ln_b']),
      mod, wo_bf16)


def _block_diag_states(s):
    b, two, h, nv, nk = s.shape
    s = s.reshape(b, two, h // 2, 2, nv, nk)
    z = jnp.zeros_like(s[:, :, :, 0])
    top = jnp.concatenate([s[:, :, :, 0], z], axis=-1)
    bot = jnp.concatenate([z, s[:, :, :, 1]], axis=-1)
    return jnp.concatenate([top, bot], axis=-2)


def _diag_states(s_bd):
    b, two, hp, _, _ = s_bd.shape
    s0 = s_bd[:, :, :, :HEAD, :HEAD]
    s1 = s_bd[:, :, :, HEAD:, HEAD:]
    return jnp.stack([s0, s1], axis=3).reshape(b, two, 2 * hp, HEAD, HEAD)


def _rwkv_layer(x, mod, state0, p, wo_bf16, lay):
    r, k, v, g, a2, lw2 = _rwkv_proj(x, mod, p, lay)
    y_ctx, s_ctx = _wkv_scan(r, k, v, a2, lw2, p, None, 0, lay.b_ctx, lay.t_ctx)
    y_lat, _ = _wkv_scan(r, k, v, a2, lw2, p, _block_diag_states(state0), lay.n_ctx, lay.b_lat, lay.t_lat)
    x = _rwkv_post_residual(x, mod, y_ctx, y_lat, r, k, v, g, a2, p, wo_bf16, lay)
    return x, _diag_states(s_ctx)


def _rope_table(n_tok, width):
    n_rows = n_tok // GRID_W
    rows = jnp.broadcast_to(jnp.arange(n_rows, dtype=f32)[:, None], (n_rows, GRID_W)).reshape(-1)
    cols = jnp.broadcast_to(jnp.arange(GRID_W, dtype=f32)[None, :], (n_rows, GRID_W)).reshape(-1)
    n_freq = HEAD // 4
    inv = ROPE_BASE ** (-jnp.arange(n_freq, dtype=f32) / n_freq)
    ang = jnp.concatenate([rows[:, None] * inv, cols[:, None] * inv], axis=-1)
    reps = width // (HEAD // 2)
    return jnp.tile(jnp.cos(ang), (1, reps)), jnp.tile(jnp.sin(ang), (1, reps))


def kernel(x_prompt, x_sample, cache_da_k, cache_da_v, state_rwkv, cache_swa_k, cache_swa_v, c, c_ctx, ada_w, ada_b, mlp_w1, mlp_w2, da_wqkv, da_q_norm, da_k_norm, da_lambda, da_subln, da_wo, rwkv_mu, rwkv_wrkv, rwkv_w0, rwkv_w1, rwkv_w2, rwkv_a0, rwkv_a1, rwkv_a2, rwkv_g1, rwkv_g2, rwkv_k_k, rwkv_k_a, rwkv_r_k, rwkv_ln_w, rwkv_ln_b, rwkv_wo, swa_wqkv, swa_q_norm, swa_k_norm, swa_sink, swa_wo):
    b_ctx, t_ctx, d = x_prompt.shape
    b_lat, t_lat, _ = x_sample.shape
    lay = Layout(b_ctx, t_ctx, b_lat, t_lat)
    depth = ada_w.shape[0]
    assert d == D_MODEL and lay.n_ctx % t_lat == 0

    x = jnp.concatenate([x_prompt.reshape(lay.n_ctx, d), x_sample.reshape(b_lat * t_lat, d)], axis=0)
    cond = jnp.concatenate([c_ctx[None, :], c, jnp.zeros((8 - 1 - b_lat, d), f32)], axis=0)
    mods = _modulation(cond, ada_w, ada_b)
    mods = mods[:, :1 + b_lat].reshape(depth, 1 + b_lat, 6, 1, d)

    rope_da = _rope_table(t_lat, LANES)
    rope_swa = _rope_table(t_lat, d)

    da_k, da_v, rwkv_s, swa_k, swa_v = [], [], [], [], []
    for i in range(depth):
        kind = i % N_MIXERS
        j = i // N_MIXERS
        mod = mods[i]
        if kind == 0:
            p = {'q_norm': da_q_norm[j], 'k_norm': da_k_norm[j], 'lam': da_lambda[j], 'subln': da_subln[j]}
            lam_init = 0.8 - 0.6 * math.exp(-0.3 * i)
            qkv = _ln_matmul(x, mod, da_wqkv[j].astype(bf16), lay)
            o_ctx, o_lat, k_new = _da_mixer(qkv, cache_da_k, cache_da_v, j, p, lam_init, rope_da, lay)
            da_k.append(k_new.reshape(b_ctx, t_ctx, DA_HEADS, LANES))
            da_v.append(qkv[:lay.n_ctx, 2 * d:].reshape(b_ctx, t_ctx, DA_HEADS, LANES))
            x = _matmul_residual(x, o_ctx, o_lat, mod, da_wo[j].astype(bf16), lay)
        elif kind == 1:
            p = {'mu': rwkv_mu[j], 'wrkv': rwkv_wrkv[j], 'w0': rwkv_w0[j], 'w1': rwkv_w1[j],
                 'w2': rwkv_w2[j], 'a0': rwkv_a0[j], 'a1': rwkv_a1[j], 'a2': rwkv_a2[j],
                 'g1': rwkv_g1[j], 'g2': rwkv_g2[j], 'k_k': rwkv_k_k[j], 'k_a': rwkv_k_a[j],
                 'r_k': rwkv_r_k[j], 'ln_w': rwkv_ln_w[j], 'ln_b': rwkv_ln_b[j]}
            x, s_new = _rwkv_layer(x, mod, state_rwkv[:, j], p, rwkv_wo[j].astype(bf16), lay)
            rwkv_s.append(s_new)
        else:
            p = {'q_norm': swa_q_norm[j], 'k_norm': swa_k_norm[j], 'sink': swa_sink[j]}
            qkv = _ln_matmul(x, mod, swa_wqkv[j].astype(bf16), lay)
            o_ctx, o_lat, k_new = _swa_mixer(qkv, cache_swa_k, cache_swa_v, j, p, rope_swa, lay)
            swa_k.append(k_new.reshape(b_ctx, t_ctx, SWA_KV_HEADS, HEAD))
            swa_v.append(qkv[:lay.n_ctx, d + SWA_KV:].reshape(b_ctx, t_ctx, SWA_KV_HEADS, HEAD))
            x = _matmul_residual(x, o_ctx, o_lat, mod, swa_wo[j].astype(bf16), lay)
        x = _mlp(x, mod, mlp_w1[i].astype(bf16), mlp_w2[i].astype(bf16), lay)

    y_prompt = x[:lay.n_ctx].reshape(b_ctx, t_ctx, d)
    y_sample = x[lay.n_ctx:].reshape(b_lat, t_lat, d)
    return (y_prompt, y_sample, jnp.stack(da_k, axis=1), jnp.stack(da_v, axis=1),
            jnp.stack(rwkv_s, axis=1), jnp.stack(swa_k, axis=1), jnp.stack(swa_v, axis=1))
```

```python
import functools
import math
from typing import NamedTuple

import jax
import jax.numpy as jnp
from jax import lax
from jax.experimental import pallas as pl
from jax.experimental.pallas import tpu as pltpu

f32 = jnp.float32
bf16 = jnp.bfloat16

D_MODEL = 1024
N_MIXERS = 3
GRID_W = 64
ROPE_BASE = 10000.0
NORM_EPS = 1e-6
NEG_INF = -1e30
LOG2E = math.log2(math.e)
HEAD = 64
LANES = 128
DA_HEADS = D_MODEL // LANES
RWKV_HEADS = D_MODEL // HEAD
RWKV_GN_EPS = 64e-5
SWA_HEADS = D_MODEL // HEAD
SWA_KV_HEADS = SWA_HEADS // 4
SWA_GROUP = SWA_HEADS // SWA_KV_HEADS
SWA_KV = SWA_KV_HEADS * HEAD
WINDOW = 128
WKV_CHUNK = 64
VMEM_LIMIT = 56 * 1024 * 1024
QK_SCALE = HEAD ** -0.5 * LOG2E


class Layout(NamedTuple):
    b_ctx: int
    t_ctx: int
    b_lat: int
    t_lat: int

    @property
    def n_ctx(self):
        return self.b_ctx * self.t_ctx

    @property
    def n_lat(self):
        return self.b_lat * self.t_lat

    @property
    def n(self):
        return self.n_ctx + self.n_lat


def _params(sem):
    return pltpu.CompilerParams(dimension_semantics=sem, vmem_limit_bytes=VMEM_LIMIT)


def _group_of_tile(i, tm, lay):
    r = i * tm
    return jnp.where(r < lay.n_ctx, 0, 1 + (r - lay.n_ctx) // lay.t_lat)


def _mod_spec(chunk, tm, lay, tile0=0):
    return pl.BlockSpec((None, None, 1, D_MODEL),
                        lambda i, *_: (_group_of_tile(i + tile0, tm, lay), chunk, 0, 0))


def _resident(shape):
    return pl.BlockSpec(shape, lambda *_: (0,) * len(shape), pipeline_mode=pl.Buffered(1))


def _split_row_specs(block, tm, lay):
    nb_ctx = lay.n_ctx // tm
    lead = (0,) * (len(block) - 2)
    ctx = pl.BlockSpec(block, lambda i, *_: lead + (jnp.minimum(i, nb_ctx - 1), 0))
    lat = pl.BlockSpec(block, lambda i, *_: lead + (jnp.maximum(i - nb_ctx, 0), 0))
    return ctx, lat


def _dot(a, b):
    return jnp.dot(a, b, preferred_element_type=f32)


def _dot_nt(a, b):
    return lax.dot_general(a, b, (((1,), (1,)), ((), ())), preferred_element_type=f32)


def _dot_tn(a, b):
    return lax.dot_general(a, b, (((0,), (0,)), ((), ())), preferred_element_type=f32)


def _split3(x):
    hi = x.astype(bf16)
    r1 = x - hi.astype(f32)
    mid = r1.astype(bf16)
    lo = (r1 - mid.astype(f32)).astype(bf16)
    return hi, mid, lo


def _dot_exact_rhs(a_bf16, x):
    hi, mid, lo = _split3(x)
    return _dot(a_bf16, hi) + (_dot(a_bf16, mid) + _dot(a_bf16, lo))


def _head_ones():
    r = lax.broadcasted_iota(jnp.int32, (LANES, LANES), 0)
    c = lax.broadcasted_iota(jnp.int32, (LANES, LANES), 1)
    return ((r // HEAD) == (c // HEAD)).astype(bf16)


def _head_sum(x):
    g = _head_ones()
    hi = x.astype(bf16)
    lo = (x - hi.astype(f32)).astype(bf16)
    cols = []
    for p in range(x.shape[1] // LANES):
        cs = slice(p * LANES, (p + 1) * LANES)
        cols.append(_dot(hi[:, cs], g) + _dot(lo[:, cs], g))
    return cols[0] if len(cols) == 1 else jnp.concatenate(cols, axis=1)


def _head_rms(x, gain):
    ms = _head_sum(x * x) * (1.0 / HEAD)
    return x * lax.rsqrt(ms + NORM_EPS) * gain


def _adaln(x, shift, scale):
    ms = jnp.mean(x * x, axis=-1, keepdims=True)
    return x * lax.rsqrt(ms + NORM_EPS) * (1.0 + scale) + shift


def _rope(x, cos, sin):
    w = x.shape[1]
    lane = lax.broadcasted_iota(jnp.int32, x.shape, 1)
    low = (lane & (HEAD - 1)) < HEAD // 2
    up = pltpu.roll(x, w - HEAD // 2, 1)
    dn = pltpu.roll(x, HEAD // 2, 1)
    return x * cos + jnp.where(low, -up, dn) * sin


def _softplus(y):
    return jnp.maximum(y, 0.0) + jnp.log(1.0 + jnp.exp(-jnp.abs(y)))


def _pick_rows(i, nb_ctx, ctx_ref, lat_ref):
    return jnp.where(i < nb_ctx, ctx_ref[...], lat_ref[...])


def _mod_kernel(c_ref, w_ref, b_ref, o_ref):
    c = c_ref[...]
    s = c * jax.nn.sigmoid(c)
    sh, sm, sl = _split3(s)
    wh, wm, wl = _split3(w_ref[...])
    acc = _dot(sh, wh) + (_dot(sh, wm) + _dot(sm, wh)) + (_dot(sh, wl) + _dot(sm, wm) + _dot(sl, wh))
    o_ref[...] = acc + b_ref[...]


def _modulation(cond8, ada_w, ada_b):
    depth, d, d6 = ada_w.shape
    tn = 1536
    return pl.pallas_call(
        _mod_kernel,
        grid=(depth, d6 // tn),
        in_specs=[
            pl.BlockSpec((8, d), lambda l, j: (0, 0)),
            pl.BlockSpec((None, d, tn), lambda l, j: (l, 0, j)),
            pl.BlockSpec((None, 1, tn), lambda l, j: (l, 0, j)),
        ],
        out_specs=pl.BlockSpec((None, 8, tn), lambda l, j: (l, 0, j)),
        out_shape=jax.ShapeDtypeStruct((depth, 8, d6), f32),
        compiler_params=_params(("parallel", "parallel")),
        name="modulation",
    )(cond8, ada_w, ada_b.reshape(depth, 1, d6))


def _ln_mm_kernel(*refs, tn, nb_ctx):
    if nb_ctx is None:
        x_ref, sh_ref, sc_ref, w_ref, o_ref = refs
        x = x_ref[...]
    else:
        xc_ref, xl_ref, sh_ref, sc_ref, w_ref, o_ref = refs
        x = _pick_rows(pl.program_id(0), nb_ctx, xc_ref, xl_ref)
    h = _adaln(x, sh_ref[...], sc_ref[...]).astype(bf16)
    for j in range(w_ref.shape[1] // tn):
        o_ref[:, j * tn:(j + 1) * tn] = _dot(h, w_ref[:, j * tn:(j + 1) * tn])


def _ln_matmul(xs, mod, w_bf16, lay, tm=512, tn=512):
    d = xs[0].shape[1]
    n_out = w_bf16.shape[1]
    if len(xs) == 1:
        x_specs = [pl.BlockSpec((tm, d), lambda i: (i, 0))]
        nb_ctx = None
    else:
        x_specs = list(_split_row_specs((tm, d), tm, lay))
        nb_ctx = lay.n_ctx // tm
    return pl.pallas_call(
        functools.partial(_ln_mm_kernel, tn=tn, nb_ctx=nb_ctx),
        grid=(lay.n // tm,),
        in_specs=x_specs + [_mod_spec(0, tm, lay), _mod_spec(1, tm, lay), _resident((d, n_out))],
        out_specs=pl.BlockSpec((tm, n_out), lambda i: (i, 0)),
        out_shape=jax.ShapeDtypeStruct((lay.n, n_out), f32),
        compiler_params=_params(("parallel",)),
        name="adaln_proj",
    )(*xs, mod, mod, w_bf16)


def _mlp_body(x, sh_ref, sc_ref, g_ref, w1_ref, w2_ref, acc_ref, o_ref, fc):
    h = _adaln(x, sh_ref[...], sc_ref[...]).astype(bf16)
    d_ff = w1_ref.shape[1]
    for c in range(d_ff // fc):
        a = _dot(h, w1_ref[:, c * fc:(c + 1) * fc])
        a = jnp.square(jnp.maximum(a, 0.0)).astype(bf16)
        part = _dot(a, w2_ref[c * fc:(c + 1) * fc, :])
        if c == 0:
            acc_ref[...] = part
        else:
            acc_ref[...] += part
    o_ref[...] = x + g_ref[...] * acc_ref[...]


def _mlp_kernel(x_ref, sh_ref, sc_ref, g_ref, w1_ref, w2_ref, o_ref, acc_ref, *, fc):
    _mlp_body(x_ref[...], sh_ref, sc_ref, g_ref, w1_ref, w2_ref, acc_ref, o_ref, fc)


def _mlp(x, mod, w1_bf16, w2_bf16, lay, tm=512, fc=1024):
    n, d = x.shape
    d_ff = w1_bf16.shape[1]
    return pl.pallas_call(
        functools.partial(_mlp_kernel, fc=fc),
        grid=(n // tm,),
        in_specs=[
            pl.BlockSpec((tm, d), lambda i: (i, 0)),
            _mod_spec(3, tm, lay), _mod_spec(4, tm, lay), _mod_spec(5, tm, lay),
            _resident((d, d_ff)), _resident((d_ff, d)),
        ],
        out_specs=pl.BlockSpec((tm, d), lambda i: (i, 0)),
        out_shape=jax.ShapeDtypeStruct((n, d), f32),
        scratch_shapes=[pltpu.VMEM((tm, d), f32)],
        compiler_params=_params(("parallel",)),
        name="mlp",
    )(x, mod, mod, mod, w1_bf16, w2_bf16)


def _out_mlp_kernel(*refs, fc, nb_ctx, split_x, split_a):
    refs = list(refs)
    i = pl.program_id(0)
    x = _pick_rows(i, nb_ctx, refs.pop(0), refs.pop(0)) if split_x else refs.pop(0)[...]
    a = _pick_rows(i, nb_ctx, refs.pop(0), refs.pop(0)) if split_a else refs.pop(0)[...]
    gm_ref, wo_ref, sh_ref, sc_ref, gf_ref, w1_ref, w2_ref, o_ref, acc_ref = refs
    x1 = x + gm_ref[...] * _dot(a.astype(bf16), wo_ref[...])
    _mlp_body(x1, sh_ref, sc_ref, gf_ref, w1_ref, w2_ref, acc_ref, o_ref, fc)


def _out_proj_mlp(xs, a_parts, mod, wo_bf16, w1_bf16, w2_bf16, lay, rows=None, tm=512, fc=1024):
    d = D_MODEL
    k_in = a_parts[0].shape[1]
    d_ff = w1_bf16.shape[1]
    tile0, n_tiles = rows if rows is not None else (0, lay.n // tm)
    split_x, split_a = len(xs) == 2, len(a_parts) == 2
    assert rows is None or not (split_x or split_a)
    x_specs = (list(_split_row_specs((tm, d), tm, lay)) if split_x
               else [pl.BlockSpec((tm, d), lambda i: (i + tile0, 0))])
    a_specs = (list(_split_row_specs((tm, k_in), tm, lay)) if split_a
               else [pl.BlockSpec((tm, k_in), lambda i: (i, 0))])
    return pl.pallas_call(
        functools.partial(_out_mlp_kernel, fc=fc, nb_ctx=lay.n_ctx // tm, split_x=split_x, split_a=split_a),
        grid=(n_tiles,),
        in_specs=x_specs + a_specs + [
            _mod_spec(2, tm, lay, tile0), _resident((k_in, d)),
            _mod_spec(3, tm, lay, tile0), _mod_spec(4, tm, lay, tile0), _mod_spec(5, tm, lay, tile0),
            _resident((d, d_ff)), _resident((d_ff, d)),
        ],
        out_specs=pl.BlockSpec((tm, d), lambda i: (i, 0)),
        out_shape=jax.ShapeDtypeStruct((n_tiles * tm, d), f32),
        scratch_shapes=[pltpu.VMEM((tm, d), f32)],
        compiler_params=_params(("parallel",)),
        name="out_proj_mlp",
    )(*xs, *a_parts, mod, wo_bf16, mod, mod, mod, w1_bf16, w2_bf16)


def _da_lambda(lam_ref, lam_init):
    lp = lam_ref[...]
    e1 = jnp.exp(jnp.sum(lp[0:1] * lp[1:2], axis=-1, keepdims=True))
    e2 = jnp.exp(jnp.sum(lp[2:3] * lp[3:4], axis=-1, keepdims=True))
    return e1 - e2 + lam_init


def _da_qk(q, k_bf16):
    lane = lax.broadcasted_iota(jnp.int32, q.shape, 1)
    return [_dot_nt(jnp.where((lane < HEAD) == (m == 0), q, 0.0).astype(bf16), k_bf16) for m in range(2)]


def _with_ones(v_bf16):
    return jnp.concatenate([v_bf16, jnp.ones_like(v_bf16)], axis=1)


def _da_pv(scores, v_ones, lam):
    outs = []
    for s in scores:
        e = jnp.exp2(s - jnp.max(s, axis=-1, keepdims=True))
        pv = _dot(e.astype(bf16), v_ones)
        outs.append(pv[:, :LANES] * (1.0 / pv[:, LANES:]))
    return outs[0] - lam * outs[1]


def _da_finish(o, sub_ref, lam_init):
    ms = jnp.mean(o * o, axis=-1, keepdims=True)
    return o * lax.rsqrt(ms + NORM_EPS) * sub_ref[...] * (1.0 - lam_init)


def _da_ctx_kernel(lam_ref, q_ref, k_ref, v_ref, qn_ref, kn_ref, sub_ref, o_ref, knew_ref, vnew_ref,
                   *, lam_init):
    lam = _da_lambda(lam_ref, lam_init)
    q = _head_rms(q_ref[...], qn_ref[...]) * QK_SCALE
    k = _head_rms(k_ref[...], kn_ref[...])
    v = v_ref[...]
    knew_ref[...] = k
    vnew_ref[...] = v
    kb = k.astype(bf16)
    vb = v.astype(bf16)
    cols = [slice(h * LANES, (h + 1) * LANES) for h in range(q.shape[1] // LANES)]
    s_next = _da_qk(q[:, cols[0]], kb[:, cols[0]])
    for h, cs in enumerate(cols):
        s_cur = s_next
        if h + 1 < len(cols):
            s_next = _da_qk(q[:, cols[h + 1]], kb[:, cols[h + 1]])
        o_ref[:, cs] = _da_finish(_da_pv(s_cur, _with_ones(vb[:, cs]), lam), sub_ref, lam_init)


def _da_lat_kernel(lam_ref, q_ref, k_ref, v_ref, ck_ref, cv_ref, qn_ref, kn_ref, sub_ref,
                   cq_ref, sq_ref, ckk_ref, skk_ref, o_ref, kf_scr, vf_scr, *, lam_init, past):
    cols = [slice(h * LANES, (h + 1) * LANES) for h in range(q_ref.shape[1] // LANES)]

    @pl.when(pl.program_id(2) == 0)
    def _():
        k = _rope(_head_rms(k_ref[...], kn_ref[...]), ckk_ref[...], skk_ref[...])
        kf_scr[0:past, :] = ck_ref[...].astype(bf16)
        kf_scr[past:, :] = k.astype(bf16)
        for h, cs in enumerate(cols):
            vf_scr[h, 0:past, :] = _with_ones(cv_ref[:, cs].astype(bf16))
            vf_scr[h, past:, :] = _with_ones(v_ref[:, cs].astype(bf16))

    lam = _da_lambda(lam_ref, lam_init)
    q = _rope(_head_rms(q_ref[...], qn_ref[...]), cq_ref[...], sq_ref[...]) * QK_SCALE
    s_next = _da_qk(q[:, cols[0]], kf_scr[:, cols[0]])
    for h, cs in enumerate(cols):
        s_cur = s_next
        if h + 1 < len(cols):
            s_next = _da_qk(q[:, cols[h + 1]], kf_scr[:, cols[h + 1]])
        o_ref[:, cs] = _da_finish(_da_pv(s_cur, vf_scr[h], lam), sub_ref, lam_init)


def _da_mixer(qkv, cache_k, cache_v, j, p, lam_init, rope_tab, lay, tq=256):
    d = D_MODEL
    nh = DA_HEADS
    qn = jnp.tile(p['q_norm'], 2).reshape(1, LANES)
    kn = jnp.tile(p['k_norm'], 2).reshape(1, LANES)
    qn_all = jnp.tile(qn, (1, nh))
    kn_all = jnp.tile(kn, (1, nh))
    sub = p['subln'].reshape(1, LANES)
    small = lambda shape: pl.BlockSpec(shape, lambda *_: (0,) * len(shape))
    t = lay.t_ctx
    ctx_rows = pl.BlockSpec((t, d), lambda b: (b, 0))
    o_ctx, k_new, v_new = pl.pallas_call(
        functools.partial(_da_ctx_kernel, lam_init=lam_init),
        grid=(lay.b_ctx,),
        in_specs=[
            small((4, HEAD)),
            ctx_rows,
            pl.BlockSpec((t, d), lambda b: (b, 1)),
            pl.BlockSpec((t, d), lambda b: (b, 2)),
            small((1, d)), small((1, d)), small((1, LANES)),
        ],
        out_specs=[ctx_rows] * 3,
        out_shape=[jax.ShapeDtypeStruct((lay.n_ctx, d), f32)] * 3,
        compiler_params=_params(("parallel",)),
        name="da_ctx",
    )(p['lam'], qkv, qkv, qkv, qn_all, kn_all, sub)

    tl = lay.t_lat
    past = cache_k.shape[2]
    row0 = lay.n_ctx // tl
    q0 = lay.n_ctx // tq
    nq = tl // tq
    cos, sin = rope_tab
    ck = cache_k.reshape(lay.b_lat, cache_k.shape[1], past, d)
    cv = cache_v.reshape(lay.b_lat, cache_v.shape[1], past, d)
    hg = 2
    gw = hg * LANES
    ng = nh // hg
    cos_g, sin_g = jnp.tile(cos, (1, hg)), jnp.tile(sin, (1, hg))
    o_lat = pl.pallas_call(
        functools.partial(_da_lat_kernel, lam_init=lam_init, past=past),
        grid=(lay.b_lat, ng, nq),
        in_specs=[
            small((4, HEAD)),
            pl.BlockSpec((tq, gw), lambda b, h, i: (q0 + b * nq + i, h)),
            pl.BlockSpec((tl, gw), lambda b, h, i: (row0 + b, ng + h)),
            pl.BlockSpec((tl, gw), lambda b, h, i: (row0 + b, 2 * ng + h)),
            pl.BlockSpec((None, None, past, gw), lambda b, h, i: (b, j, 0, h)),
            pl.BlockSpec((None, None, past, gw), lambda b, h, i: (b, j, 0, h)),
            small((1, gw)), small((1, gw)), small((1, LANES)),
            pl.BlockSpec((tq, gw), lambda b, h, i: (i, 0)),
            pl.BlockSpec((tq, gw), lambda b, h, i: (i, 0)),
            small((tl, gw)), small((tl, gw)),
        ],
        out_specs=pl.BlockSpec((tq, gw), lambda b, h, i: (b * nq + i, h)),
        out_shape=jax.ShapeDtypeStruct((lay.n_lat, d), f32),
        scratch_shapes=[pltpu.VMEM((past + tl, gw), bf16), pltpu.VMEM((hg, past + tl, 2 * LANES), bf16)],
        compiler_params=_params(("parallel", "parallel", "arbitrary")),
        name="da_lat",
    )(p['lam'], qkv, qkv, qkv, ck, cv, qn_all[:, :gw], kn_all[:, :gw], sub, cos_g, sin_g, cos_g, sin_g)
    return o_ctx, o_lat, k_new, v_new


def _swa_head_mask(shape, j):
    lane = lax.broadcasted_iota(jnp.int32, shape, 1)
    return (lane >= j * HEAD) & (lane < (j + 1) * HEAD)


def _swa_heads(q, sink_ref, score_fn, pv_fn, o_ref):
    heads = [(g, j) for g in range(SWA_GROUP) for j in range(SWA_KV_HEADS)]

    def scores(g, j):
        qg = q[:, g * SWA_KV:(g + 1) * SWA_KV]
        return score_fn(jnp.where(_swa_head_mask(qg.shape, j), qg, 0.0).astype(bf16))

    s_next = scores(*heads[0])
    o = None
    for idx, (g, j) in enumerate(heads):
        s_cur = s_next
        if idx + 1 < len(heads):
            s_next = scores(*heads[idx + 1])
        sink = sink_ref[j * SWA_GROUP + g] * LOG2E
        m = sink
        for s in s_cur:
            m = jnp.maximum(jnp.max(s, axis=-1, keepdims=True), m)
        exps = [jnp.exp2(s - m) for s in s_cur]
        den = jnp.exp2(sink - m)
        for e in exps:
            den = den + jnp.sum(e, axis=-1, keepdims=True)
        oj = pv_fn(j, [e.astype(bf16) for e in exps]) * (1.0 / den)
        o = oj if j == 0 else o + oj
        if j == SWA_KV_HEADS - 1:
            o_ref[:, g * SWA_KV:(g + 1) * SWA_KV] = o


def _swa_ctx_kernel(sink_ref, q_ref, k_ref, v_ref, qn_ref, kn_ref, o_ref, knew_ref, vnew_ref):
    q = _head_rms(q_ref[...], qn_ref[...]) * QK_SCALE
    k = _head_rms(k_ref[...], kn_ref[...])
    v = v_ref[...]
    knew_ref[...] = k
    vnew_ref[...] = v
    kb = k.astype(bf16)
    v_heads = [jnp.where(_swa_head_mask(v.shape, j), v, 0.0).astype(bf16) for j in range(SWA_KV_HEADS)]
    _swa_heads(q, sink_ref, lambda qm: [_dot_nt(qm, kb)], lambda j, e: _dot(e[0], v_heads[j]), o_ref)


def _swa_lat_kernel(sink_ref, q_ref, k_ref, v_ref, ck_ref, cv_ref, qn_ref, kn_ref,
                    cq_ref, sq_ref, ckk_ref, skk_ref, o_ref, kp_scr, vp_scr, *, t_lat):
    n = pl.program_id(1)
    w = WINDOW

    @pl.when(n == 0)
    def _():
        zeros = jnp.zeros((w, SWA_KV), bf16)
        k = _rope(_head_rms(k_ref[...], kn_ref[...]), ckk_ref[...], skk_ref[...])
        kp_scr[0:w, :] = zeros
        kp_scr[w:w + t_lat, :] = k.astype(bf16)
        kp_scr[w + t_lat:, :] = zeros
        vp_scr[0:w, :] = zeros
        vp_scr[w:w + t_lat, :] = v_ref[...].astype(bf16)
        vp_scr[w + t_lat:, :] = zeros

    q = _rope(_head_rms(q_ref[...], qn_ref[...]), cq_ref[...], sq_ref[...]) * QK_SCALE
    start = pl.multiple_of(n * w, w)
    kband = kp_scr[pl.ds(start, 3 * w), :]
    vband = vp_scr[pl.ds(start, 3 * w), :]
    ckb = ck_ref[...].astype(bf16)
    cv = cv_ref[...]
    qi = lax.broadcasted_iota(jnp.int32, (w, 3 * w), 0)
    si = lax.broadcasted_iota(jnp.int32, (w, 3 * w), 1)
    rel = si - qi
    key_pos = (n - 1) * w + si
    valid = (rel >= 0) & (rel <= 2 * w) & (key_pos >= 0) & (key_pos < t_lat)
    cv_heads = [jnp.where(_swa_head_mask(cv.shape, j), cv, 0.0).astype(bf16) for j in range(SWA_KV_HEADS)]
    vb_heads = [jnp.where(_swa_head_mask(vband.shape, j), vband, jnp.zeros_like(vband))
                for j in range(SWA_KV_HEADS)]
    _swa_heads(q, sink_ref,
               lambda qm: [_dot_nt(qm, ckb), jnp.where(valid, _dot_nt(qm, kband), NEG_INF)],
               lambda j, e: _dot(e[0], cv_heads[j]) + _dot(e[1], vb_heads[j]),
               o_ref)


def _swa_group_major_weights(wqkv, wo):
    d = D_MODEL
    wq = wqkv[:, :d].reshape(d, SWA_KV_HEADS, SWA_GROUP, HEAD).transpose(0, 2, 1, 3).reshape(d, d)
    wo = wo.reshape(SWA_KV_HEADS, SWA_GROUP, HEAD, d).transpose(1, 0, 2, 3).reshape(d, d)
    return jnp.concatenate([wq, wqkv[:, d:]], axis=1).astype(bf16), wo.astype(bf16)


def _swa_mixer(qkv, cache_k, cache_v, j, p, rope_tab, lay):
    d = D_MODEL
    qn = jnp.tile(p['q_norm'], d // HEAD).reshape(1, d)
    kn = jnp.tile(p['k_norm'], SWA_KV_HEADS).reshape(1, SWA_KV)
    small = lambda shape: pl.BlockSpec(shape, lambda *_: (0,) * len(shape))
    smem = pl.BlockSpec(memory_space=pltpu.SMEM)
    t = lay.t_ctx
    kcol = d // SWA_KV
    kv_rows = pl.BlockSpec((t, SWA_KV), lambda b: (b, 0))
    o_ctx, k_new, v_new = pl.pallas_call(
        _swa_ctx_kernel,
        grid=(lay.b_ctx,),
        in_specs=[
            smem,
            pl.BlockSpec((t, d), lambda b: (b, 0)),
            pl.BlockSpec((t, SWA_KV), lambda b: (b, kcol)),
            pl.BlockSpec((t, SWA_KV), lambda b: (b, kcol + 1)),
            small((1, d)), small((1, SWA_KV)),
        ],
        out_specs=[pl.BlockSpec((t, d), lambda b: (b, 0)), kv_rows, kv_rows],
        out_shape=[jax.ShapeDtypeStruct((lay.n_ctx, d), f32),
                   jax.ShapeDtypeStruct((lay.n_ctx, SWA_KV), f32),
                   jax.ShapeDtypeStruct((lay.n_ctx, SWA_KV), f32)],
        compiler_params=_params(("parallel",)),
        name="swa_ctx",
    )(p['sink'], qkv, qkv, qkv, qn, kn)

    tl = lay.t_lat
    w = WINDOW
    past = cache_k.shape[2]
    row0 = lay.n_ctx // tl
    q0 = lay.n_ctx // w
    nq = tl // w
    cos, sin = rope_tab
    ck = cache_k.reshape(lay.b_lat, cache_k.shape[1], past, SWA_KV)
    cv = cache_v.reshape(lay.b_lat, cache_v.shape[1], past, SWA_KV)
    o_lat = pl.pallas_call(
        functools.partial(_swa_lat_kernel, t_lat=tl),
        grid=(lay.b_lat, nq),
        in_specs=[
            smem,
            pl.BlockSpec((w, d), lambda b, i: (q0 + b * nq + i, 0)),
            pl.BlockSpec((tl, SWA_KV), lambda b, i: (row0 + b, kcol)),
            pl.BlockSpec((tl, SWA_KV), lambda b, i: (row0 + b, kcol + 1)),
            pl.BlockSpec((None, None, past, SWA_KV), lambda b, i: (b, j, 0, 0)),
            pl.BlockSpec((None, None, past, SWA_KV), lambda b, i: (b, j, 0, 0)),
            small((1, d)), small((1, SWA_KV)),
            pl.BlockSpec((w, d), lambda b, i: (i, 0)),
            pl.BlockSpec((w, d), lambda b, i: (i, 0)),
            pl.BlockSpec((tl, SWA_KV), lambda b, i: (0, 0)),
            pl.BlockSpec((tl, SWA_KV), lambda b, i: (0, 0)),
        ],
        out_specs=pl.BlockSpec((w, d), lambda b, i: (b * nq + i, 0)),
        out_shape=jax.ShapeDtypeStruct((lay.n_lat, d), f32),
        scratch_shapes=[pltpu.VMEM((tl + 2 * w, SWA_KV), bf16), pltpu.VMEM((tl + 2 * w, SWA_KV), bf16)],
        compiler_params=_params(("parallel", "arbitrary")),
        name="swa_lat",
    )(p['sink'], qkv, qkv, qkv, ck, cv, qn, kn, cos, sin, cos, sin)
    return o_ctx, o_lat, k_new, v_new


def _rwkv_proj_kernel(x_ref, xp_ref, xn_ref, sh_ref, sc_ref, mu_ref, wrkv_ref, w1_ref, w2_ref, w0_ref,
                      a1_ref, a2_ref, a0_ref, g1_ref, g2_ref,
                      r_ref, k_ref, v_ref, g_ref, a_ref, lw_ref, *, lay, tm):
    i = pl.program_id(0)
    r0 = i * tm
    in_ctx = r0 < lay.n_ctx
    pos = jnp.where(in_ctx, r0 % lay.t_ctx, (r0 - lay.n_ctx) % lay.t_lat)
    seq_len = jnp.where(in_ctx, lay.t_ctx, lay.t_lat)
    keep_prev = jnp.where(pos == 0, 0.0, 1.0)
    keep_next = jnp.where(pos + tm == seq_len, 0.0, 1.0)

    sh = sh_ref[...]
    sc = sc_ref[...]
    h = _adaln(x_ref[...], sh, sc)
    h_prev_row = _adaln(xp_ref[7:8, :], sh, sc) * keep_prev
    h_next_row = _adaln(xn_ref[0:1, :], sh, sc) * keep_next
    row = lax.broadcasted_iota(jnp.int32, h.shape, 0)
    prev = jnp.where(row == 0, h_prev_row, pltpu.roll(h, 1, 0))
    nxt = jnp.where(row == tm - 1, h_next_row, pltpu.roll(h, tm - 1, 0))
    xx = 0.5 * (prev + nxt) - h

    def mix(m):
        return (h + xx * mu_ref[m:m + 1, :]).astype(bf16)

    r_ref[...] = _dot(mix(0), wrkv_ref[0])
    k_ref[...] = _dot(mix(1), wrkv_ref[1])
    v_ref[...] = _dot(mix(2), wrkv_ref[2])

    lane = lax.broadcasted_iota(jnp.int32, (tm, LANES), 1)
    first = lane < HEAD

    tw = jnp.tanh(_dot(mix(3), w1_ref[...]))
    la = _dot(mix(4), a1_ref[...])
    for z in range(2):
        sel = first if z == 0 else jnp.logical_not(first)
        w_lora = _dot(jnp.where(sel, tw, 0.0).astype(bf16), w2_ref[...])
        w_log = -_softplus(-(w0_ref[z:z + 1, :] + w_lora)) - 0.5
        lw_ref[z] = -jnp.exp(w_log)
        a_lora = _dot(jnp.where(sel, la, 0.0).astype(bf16), a2_ref[...])
        a_ref[z] = jax.nn.sigmoid(a0_ref[z:z + 1, :] + a_lora)

    g_ref[...] = _dot(jax.nn.sigmoid(_dot(mix(5), g1_ref[...])).astype(bf16), g2_ref[...])


def _rwkv_proj(x, mod, p, lay, tm=256):
    n, d = x.shape
    assert lay.t_ctx % tm == 0 and lay.t_lat % tm == 0
    nb8 = n // 8
    wrkv = p['wrkv'].astype(bf16)
    w1 = jnp.concatenate([p['w1'][0], p['w1'][1]], axis=1).astype(bf16)
    w2 = p['w2'].reshape(2 * p['w2'].shape[1], d).astype(bf16)
    a1 = jnp.concatenate([p['a1'][0], p['a1'][1]], axis=1).astype(bf16)
    a2 = p['a2'].reshape(2 * p['a2'].shape[1], d).astype(bf16)
    g1 = p['g1'].astype(bf16)
    g2 = p['g2'].astype(bf16)
    row_spec = pl.BlockSpec((tm, d), lambda i: (i, 0))
    z_spec = pl.BlockSpec((2, tm, d), lambda i: (0, i, 0))
    args = (x, x, x, mod, mod, p['mu'], wrkv, w1, w2, p['w0'], a1, a2, p['a0'], g1, g2)
    return pl.pallas_call(
        functools.partial(_rwkv_proj_kernel, lay=lay, tm=tm),
        grid=(n // tm,),
        in_specs=[
            row_spec,
            pl.BlockSpec((8, d), lambda i: (jnp.maximum(i * (tm // 8) - 1, 0), 0)),
            pl.BlockSpec((8, d), lambda i: (jnp.minimum((i + 1) * (tm // 8), nb8 - 1), 0)),
            _mod_spec(0, tm, lay),
            _mod_spec(1, tm, lay),
        ] + [_resident(a.shape) for a in args[5:]],
        out_specs=[row_spec] * 4 + [z_spec] * 2,
        out_shape=[jax.ShapeDtypeStruct((n, d), f32)] * 4 + [jax.ShapeDtypeStruct((2, n, d), f32)] * 2,
        compiler_params=_params(("parallel",)),
        name="rwkv_proj",
    )(*args)


def _wkv_prep(r, k, v, a, lw, kk_gain, ka, forward):
    c_len = r.shape[0]
    t_c = lax.broadcasted_iota(jnp.int32, (c_len, c_len), 0)
    i_c = lax.broadcasted_iota(jnp.int32, (c_len, c_len), 1)
    tri = ((i_c <= t_c) if forward else (i_c >= t_c)).astype(bf16)
    kkr = k * kk_gain
    kk = kkr * lax.rsqrt(jnp.maximum(_head_sum(kkr * kkr), 1e-24))
    beta = kk * a
    kd = k * (1.0 + (a - 1.0) * ka)
    cum = _dot_exact_rhs(tri, lw)
    total = cum[c_len - 1:c_len, :] if forward else cum[0:1, :]
    inv = jnp.exp(-cum)
    rem = jnp.exp(total - cum)
    return dict(a_t=-kk * jnp.exp(cum - lw), r_t=r * jnp.exp(cum), b_inv=beta * inv, k_inv=kd * inv,
                b_rem=beta * rem, k_rem=kd * rem, p_total=jnp.exp(total), v=v)


def _wkv_kernel(*refs, has_init):
    fwd_refs, bwd_refs = refs[0:5], refs[5:10]
    kk_ref, ka_ref = refs[10:12]
    n_in = 13 if has_init else 12
    yf_ref, yb_ref, sout_ref, s_scr = refs[n_in:]
    c_len = WKV_CHUNK
    n_pair = yf_ref.shape[1] // LANES
    c = pl.program_id(1)

    @pl.when(c == 0)
    def _():
        if has_init:
            s_scr[...] = refs[12][...]
        else:
            s_scr[...] = jnp.zeros_like(s_scr)

    t_s = lax.broadcasted_iota(jnp.int32, (c_len, LANES), 0)
    i_s = lax.broadcasted_iota(jnp.int32, (c_len, LANES), 1) & (c_len - 1)
    strict = (i_s < t_s, i_s > t_s)
    incl = (i_s <= t_s, i_s >= t_s)
    eye = jnp.where(t_s == i_s, 1.0, 0.0)
    pair_block = (t_s >> 1) == (i_s >> 1)
    levels = []
    for lb in range(1, int(math.log2(c_len))):
        levels.append(((t_s >> (lb + 1)) == (i_s >> (lb + 1))) & ((t_s >> lb) != (i_s >> lb)))
    rb = lax.broadcasted_iota(jnp.int32, (2 * c_len, LANES), 0) // c_len
    cb = lax.broadcasted_iota(jnp.int32, (2 * c_len, LANES), 1) // HEAD
    same_head = rb == cb

    def bd(x):
        return jnp.where(same_head, jnp.concatenate([x, x], axis=0), 0.0).astype(bf16)

    ops = [_wkv_prep(*(ref[...] for ref in d_refs), kk_ref[...], ka_ref[...], forward=(dr == 0))
           for dr, d_refs in enumerate((fwd_refs, bwd_refs))]

    chains = [(dr, pr, slice(pr * LANES, (pr + 1) * LANES)) for dr in range(2) for pr in range(n_pair)]
    ids = range(len(chains))
    lhs = [jnp.concatenate([ops[dr]['a_t'][:, cs], ops[dr]['r_t'][:, cs]], axis=0).astype(bf16)
           for dr, pr, cs in chains]
    s0 = [s_scr[dr, pr] for dr, pr, cs in chains]
    sc = [_dot_nt(lhs[i], jnp.concatenate([bd(ops[dr]['b_inv'][:, cs]), bd(ops[dr]['k_inv'][:, cs]),
                                           s0[i].astype(bf16)], axis=0))
          for i, (dr, pr, cs) in enumerate(chains)]
    sc_b = [s[:, :LANES] for s in sc]
    sc_k = [s[:, LANES:2 * LANES] for s in sc]
    x0 = [s[:, 2 * LANES:] for s in sc]
    n_mat = [jnp.where(strict[dr], sc_b[i][:c_len], 0.0) for i, (dr, pr, cs) in enumerate(chains)]
    v_bd = [bd(ops[dr]['v'][:, cs]) for dr, pr, cs in chains]
    x = [x0[i][:c_len] + _dot(jnp.where(strict[dr], sc_k[i][:c_len], 0.0).astype(bf16), v_bd[i])
         for i, (dr, pr, cs) in enumerate(chains)]
    t_inv = [eye + jnp.where(pair_block, n_mat[i], 0.0) for i in ids]
    for off_diag in levels:
        half = [_dot(t_inv[i].astype(bf16), bd(jnp.where(off_diag, n_mat[i], 0.0))) for i in ids]
        t_inv = [t_inv[i] + _dot(half[i].astype(bf16), bd(t_inv[i])) for i in ids]
    uv_bd = [jnp.concatenate([bd(_dot(t_inv[i].astype(bf16), bd(x[i]))), v_bd[i]], axis=0) for i in ids]
    for i, (dr, pr, cs) in enumerate(chains):
        rbk = jnp.concatenate([jnp.where(incl[dr], sc_b[i][c_len:], 0.0),
                               jnp.where(incl[dr], sc_k[i][c_len:], 0.0)], axis=1).astype(bf16)
        y_ref = yf_ref if dr == 0 else yb_ref
        y_ref[:, cs] = x0[i][c_len:] + _dot(rbk, uv_bd[i])
    for i, (dr, pr, cs) in enumerate(chains):
        bk_rem = jnp.concatenate([bd(ops[dr]['b_rem'][:, cs]), bd(ops[dr]['k_rem'][:, cs])], axis=0)
        s_scr[dr, pr] = s0[i] * ops[dr]['p_total'][:, cs] + _dot_tn(uv_bd[i], bk_rem)

    @pl.when(c == pl.num_programs(1) - 1)
    def _():
        sout_ref[...] = s_scr[...]


def _wkv_scan(r, k, v, a2, lw2, p, s0_bd, row0, n_seq, t_len):
    d = D_MODEL
    c_len = WKV_CHUNK
    nc = t_len // c_len
    b0 = row0 // c_len
    n_pair = d // LANES
    has_init = s0_bd is not None

    fwd_blk = lambda s, c: s * nc + c
    bwd_blk = lambda s, c: s * nc + (nc - 1 - c)
    tok_f = pl.BlockSpec((c_len, d), lambda s, c: (b0 + fwd_blk(s, c), 0))
    tok_b = pl.BlockSpec((c_len, d), lambda s, c: (b0 + bwd_blk(s, c), 0))
    z_f = pl.BlockSpec((None, c_len, d), lambda s, c: (0, b0 + fwd_blk(s, c), 0))
    z_b = pl.BlockSpec((None, c_len, d), lambda s, c: (1, b0 + bwd_blk(s, c), 0))
    vec = pl.BlockSpec((1, d), lambda s, c: (0, 0))
    st = pl.BlockSpec((None, 2, n_pair, LANES, LANES), lambda s, c: (s, 0, 0, 0, 0))
    in_specs = [tok_f, tok_f, tok_f, z_f, z_f, tok_b, tok_b, tok_b, z_b, z_b, vec, vec]
    args = [r, k, v, a2, lw2, r, k, v, a2, lw2, p['k_k'].reshape(1, d), p['k_a'].reshape(1, d)]
    if has_init:
        in_specs.append(st)
        args.append(s0_bd)
    return pl.pallas_call(
        functools.partial(_wkv_kernel, has_init=has_init),
        grid=(n_seq, nc),
        in_specs=in_specs,
        out_specs=[
            pl.BlockSpec((c_len, d), lambda s, c: (fwd_blk(s, c), 0)),
            pl.BlockSpec((c_len, d), lambda s, c: (bwd_blk(s, c), 0)),
            st,
        ],
        out_shape=[jax.ShapeDtypeStruct((n_seq * t_len, d), f32)] * 2
        + [jax.ShapeDtypeStruct((n_seq, 2, n_pair, LANES, LANES), f32)],
        scratch_shapes=[pltpu.VMEM((2, n_pair, LANES, LANES), f32)],
        compiler_params=_params(("parallel", "arbitrary")),
        name="wkv_scan",
    )(*args)


def _rwkv_post_kernel(x_ref, yfc_ref, ybc_ref, yfl_ref, ybl_ref, r_ref, k_ref, v_ref, g_ref, a_ref,
                      ka_ref, rk_ref, lnw_ref, lnb_ref, gate_ref, wo_ref, o_ref, *, nb_ctx):
    i = pl.program_id(0)
    y = _pick_rows(i, nb_ctx, yfc_ref, yfl_ref) + _pick_rows(i, nb_ctx, ybc_ref, ybl_ref)
    mu = _head_sum(y) * (1.0 / HEAD)
    yc = y - mu
    var = _head_sum(yc * yc) * (1.0 / HEAD)
    yn = yc * lax.rsqrt(var + RWKV_GN_EPS) * lnw_ref[...] + lnb_ref[...]
    k = k_ref[...]
    ka = ka_ref[...]
    kd_sum = k * (1.0 + (a_ref[0] - 1.0) * ka) + k * (1.0 + (a_ref[1] - 1.0) * ka)
    bonus = _head_sum(r_ref[...] * kd_sum * rk_ref[...]) * v_ref[...]
    o = ((yn + bonus) * g_ref[...]).astype(bf16)
    o_ref[...] = x_ref[...] + gate_ref[...] * _dot(o, wo_ref[...])


def _rwkv_post_residual(x, mod, y_ctx, y_lat, r, k, v, g, a2, p, wo_bf16, lay, tm=256):
    n, d = r.shape
    tok = pl.BlockSpec((tm, d), lambda i: (i, 0))
    tok_z = pl.BlockSpec((2, tm, d), lambda i: (0, i, 0))
    vec = pl.BlockSpec((1, d), lambda i: (0, 0))
    yc_spec, yl_spec = _split_row_specs((tm, d), tm, lay)
    row = lambda a: a.reshape(1, d)
    return pl.pallas_call(
        functools.partial(_rwkv_post_kernel, nb_ctx=lay.n_ctx // tm),
        grid=(n // tm,),
        in_specs=[tok, yc_spec, yc_spec, yl_spec, yl_spec, tok, tok, tok, tok, tok_z, vec, vec, vec, vec,
                  _mod_spec(2, tm, lay), _resident((d, d))],
        out_specs=tok,
        out_shape=jax.ShapeDtypeStruct((n, d), f32),
        compiler_params=_params(("parallel",)),
        name="rwkv_post",
    )(x, y_ctx[0], y_ctx[1], y_lat[0], y_lat[1], r, k, v, g, a2,
      row(p['k_a']), row(p['r_k']), row(p['ln_w']), row(p['ln_b']), mod, wo_bf16)


def _block_diag_states(s):
    b, two, h, nv, nk = s.shape
    s = s.reshape(b, two, h // 2, 2, nv, nk)
    z = jnp.zeros_like(s[:, :, :, 0])
    top = jnp.concatenate([s[:, :, :, 0], z], axis=-1)
    bot = jnp.concatenate([z, s[:, :, :, 1]], axis=-1)
    return jnp.concatenate([top, bot], axis=-2)


def _diag_states(s_bd):
    b, two, hp, _, _ = s_bd.shape
    s0 = s_bd[:, :, :, :HEAD, :HEAD]
    s1 = s_bd[:, :, :, HEAD:, HEAD:]
    return jnp.stack([s0, s1], axis=3).reshape(b, two, 2 * hp, HEAD, HEAD)


def _rwkv_layer(x, mod, state0, p, wo_bf16, lay):
    r, k, v, g, a2, lw2 = _rwkv_proj(x, mod, p, lay)
    yf_c, yb_c, s_ctx = _wkv_scan(r, k, v, a2, lw2, p, None, 0, lay.b_ctx, lay.t_ctx)
    yf_l, yb_l, _ = _wkv_scan(r, k, v, a2, lw2, p, _block_diag_states(state0), lay.n_ctx, lay.b_lat, lay.t_lat)
    x = _rwkv_post_residual(x, mod, (yf_c, yb_c), (yf_l, yb_l), r, k, v, g, a2, p, wo_bf16, lay)
    return x, _diag_states(s_ctx)


def _rope_table(n_tok, width):
    n_rows = n_tok // GRID_W
    rows = jnp.broadcast_to(jnp.arange(n_rows, dtype=f32)[:, None], (n_rows, GRID_W)).reshape(-1)
    cols = jnp.broadcast_to(jnp.arange(GRID_W, dtype=f32)[None, :], (n_rows, GRID_W)).reshape(-1)
    n_freq = HEAD // 4
    inv = ROPE_BASE ** (-jnp.arange(n_freq, dtype=f32) / n_freq)
    ang = jnp.concatenate([rows[:, None] * inv, cols[:, None] * inv], axis=-1)
    reps = width // (HEAD // 2)
    return jnp.tile(jnp.cos(ang), (1, reps)), jnp.tile(jnp.sin(ang), (1, reps))


def kernel(x_prompt, x_sample, cache_da_k, cache_da_v, state_rwkv, cache_swa_k, cache_swa_v, c, c_ctx, ada_w, ada_b, mlp_w1, mlp_w2, da_wqkv, da_q_norm, da_k_norm, da_lambda, da_subln, da_wo, rwkv_mu, rwkv_wrkv, rwkv_w0, rwkv_w1, rwkv_w2, rwkv_a0, rwkv_a1, rwkv_a2, rwkv_g1, rwkv_g2, rwkv_k_k, rwkv_k_a, rwkv_r_k, rwkv_ln_w, rwkv_ln_b, rwkv_wo, swa_wqkv, swa_q_norm, swa_k_norm, swa_sink, swa_wo):
    b_ctx, t_ctx, d = x_prompt.shape
    b_lat, t_lat, _ = x_sample.shape
    lay = Layout(b_ctx, t_ctx, b_lat, t_lat)
    depth = ada_w.shape[0]
    tm = 512
    assert d == D_MODEL and lay.n_ctx % t_lat == 0

    xs = (x_prompt.reshape(lay.n_ctx, d), x_sample.reshape(lay.n_lat, d))
    cond = jnp.concatenate([c_ctx[None, :], c, jnp.zeros((8 - 1 - b_lat, d), f32)], axis=0)
    mods = _modulation(cond, ada_w, ada_b)
    mods = mods[:, :1 + b_lat].reshape(depth, 1 + b_lat, 6, 1, d)

    rope_da = _rope_table(t_lat, LANES)
    rope_swa = _rope_table(t_lat, d)

    da_k, da_v, rwkv_s, swa_k, swa_v = [], [], [], [], []
    for i in range(depth):
        kind = i % N_MIXERS
        j = i // N_MIXERS
        mod = mods[i]
        w1 = mlp_w1[i].astype(bf16)
        w2 = mlp_w2[i].astype(bf16)
        if kind == 1:
            p = {'mu': rwkv_mu[j], 'wrkv': rwkv_wrkv[j], 'w0': rwkv_w0[j], 'w1': rwkv_w1[j],
                 'w2': rwkv_w2[j], 'a0': rwkv_a0[j], 'a1': rwkv_a1[j], 'a2': rwkv_a2[j],
                 'g1': rwkv_g1[j], 'g2': rwkv_g2[j], 'k_k': rwkv_k_k[j], 'k_a': rwkv_k_a[j],
                 'r_k': rwkv_r_k[j], 'ln_w': rwkv_ln_w[j], 'ln_b': rwkv_ln_b[j]}
            x = xs[0] if len(xs) == 1 else jnp.concatenate(xs, axis=0)
            x, s_new = _rwkv_layer(x, mod, state_rwkv[:, j], p, rwkv_wo[j].astype(bf16), lay)
            rwkv_s.append(s_new)
            xs = (_mlp(x, mod, w1, w2, lay),)
            continue
        if kind == 0:
            p = {'q_norm': da_q_norm[j], 'k_norm': da_k_norm[j], 'lam': da_lambda[j], 'subln': da_subln[j]}
            lam_init = 0.8 - 0.6 * math.exp(-0.3 * i)
            qkv = _ln_matmul(xs, mod, da_wqkv[j].astype(bf16), lay)
            o_ctx, o_lat, k_new, v_new = _da_mixer(qkv, cache_da_k, cache_da_v, j, p, lam_init, rope_da, lay)
            da_k.append(k_new.reshape(b_ctx, t_ctx, DA_HEADS, LANES))
            da_v.append(v_new.reshape(b_ctx, t_ctx, DA_HEADS, LANES))
            wo = da_wo[j].astype(bf16)
        else:
            p = {'q_norm': swa_q_norm[j], 'k_norm': swa_k_norm[j], 'sink': swa_sink[j]}
            wqkv, wo = _swa_group_major_weights(swa_wqkv[j], swa_wo[j])
            qkv = _ln_matmul(xs, mod, wqkv, lay)
            o_ctx, o_lat, k_new, v_new = _swa_mixer(qkv, cache_swa_k, cache_swa_v, j, p, rope_swa, lay)
            swa_k.append(k_new.reshape(b_ctx, t_ctx, SWA_KV_HEADS, HEAD))
            swa_v.append(v_new.reshape(b_ctx, t_ctx, SWA_KV_HEADS, HEAD))
        if i == depth - 1 and len(xs) == 1:
            nb_ctx = lay.n_ctx // tm
            xs = (_out_proj_mlp(xs, (o_ctx,), mod, wo, w1, w2, lay, rows=(0, nb_ctx), tm=tm),
                  _out_proj_mlp(xs, (o_lat,), mod, wo, w1, w2, lay, rows=(nb_ctx, lay.n_lat // tm), tm=tm))
        else:
            xs = (_out_proj_mlp(xs, (o_ctx, o_lat), mod, wo, w1, w2, lay, tm=tm),)

    if len(xs) == 1:
        xs = (xs[0][:lay.n_ctx], xs[0][lay.n_ctx:])
    y_prompt = xs[0].reshape(b_ctx, t_ctx, d)
    y_sample = xs[1].reshape(b_lat, t_lat, d)
    return (y_prompt, y_sample, jnp.stack(da_k, axis=1), jnp.stack(da_v, axis=1),
            jnp.stack(rwkv_s, axis=1), jnp.stack(swa_k, axis=1), jnp.stack(swa_v, axis=1))
```

```python
import functools
import math
from typing import NamedTuple

import jax
import jax.numpy as jnp
from jax import lax
from jax.experimental import pallas as pl
from jax.experimental.pallas import tpu as pltpu

f32 = jnp.float32
bf16 = jnp.bfloat16

D_MODEL = 1024
N_MIXERS = 3
GRID_W = 64
ROPE_BASE = 10000.0
NORM_EPS = 1e-6
NEG_INF = -1e30
LOG2E = math.log2(math.e)
HEAD = 64
LANES = 128
DA_HEADS = D_MODEL // LANES
RWKV_HEADS = D_MODEL // HEAD
RWKV_GN_EPS = 64e-5
SWA_HEADS = D_MODEL // HEAD
SWA_KV_HEADS = SWA_HEADS // 4
SWA_GROUP = SWA_HEADS // SWA_KV_HEADS
SWA_KV = SWA_KV_HEADS * HEAD
WINDOW = 128
WKV_CHUNK = 64
WKV_LANES = 128
VMEM_LIMIT = 56 * 1024 * 1024
QK_SCALE = HEAD ** -0.5 * LOG2E


class Layout(NamedTuple):
    b_ctx: int
    t_ctx: int
    b_lat: int
    t_lat: int

    @property
    def n_ctx(self):
        return self.b_ctx * self.t_ctx

    @property
    def n_lat(self):
        return self.b_lat * self.t_lat

    @property
    def n(self):
        return self.n_ctx + self.n_lat


def _params(sem):
    return pltpu.CompilerParams(dimension_semantics=sem, vmem_limit_bytes=VMEM_LIMIT)


def _group_of_tile(i, tm, lay):
    r = i * tm
    return jnp.where(r < lay.n_ctx, 0, 1 + (r - lay.n_ctx) // lay.t_lat)


def _mod_spec(chunk, tm, lay, tile0=0):
    return pl.BlockSpec((None, None, 1, D_MODEL),
                        lambda i, *_: (_group_of_tile(i + tile0, tm, lay), chunk, 0, 0))


def _resident(shape):
    return pl.BlockSpec(shape, lambda *_: (0,) * len(shape), pipeline_mode=pl.Buffered(1))


class LayerWeight(NamedTuple):
    stack: jax.Array
    layer: int

    @property
    def shape(self):
        return self.stack.shape[1:]

    def spec(self):
        idx = (self.layer,) + (0,) * (self.stack.ndim - 1)
        return pl.BlockSpec((None,) + self.shape, lambda *_: idx, pipeline_mode=pl.Buffered(1))


def _weight_spec(w):
    return w.spec() if isinstance(w, LayerWeight) else _resident(w.shape)


def _weight_arg(w):
    return w.stack if isinstance(w, LayerWeight) else w


def _split_row_specs(block, tm, lay):
    nb_ctx = lay.n_ctx // tm
    lead = (0,) * (len(block) - 2)
    ctx = pl.BlockSpec(block, lambda i, *_: lead + (jnp.minimum(i, nb_ctx - 1), 0))
    lat = pl.BlockSpec(block, lambda i, *_: lead + (jnp.maximum(i - nb_ctx, 0), 0))
    return ctx, lat


def _dot(a, b):
    return jnp.dot(a, b, preferred_element_type=f32)


def _dot_nt(a, b):
    return lax.dot_general(a, b, (((1,), (1,)), ((), ())), preferred_element_type=f32)


def _dot_tn(a, b):
    return lax.dot_general(a, b, (((0,), (0,)), ((), ())), preferred_element_type=f32)


def _split3(x):
    hi = x.astype(bf16)
    r1 = x - hi.astype(f32)
    mid = r1.astype(bf16)
    lo = (r1 - mid.astype(f32)).astype(bf16)
    return hi, mid, lo


def _head_ones():
    r = lax.broadcasted_iota(jnp.int32, (LANES, LANES), 0)
    c = lax.broadcasted_iota(jnp.int32, (LANES, LANES), 1)
    return ((r // HEAD) == (c // HEAD)).astype(bf16)


def _head_sum(x):
    g = _head_ones()
    hi = x.astype(bf16)
    lo = (x - hi.astype(f32)).astype(bf16)
    cols = []
    for p in range(x.shape[1] // LANES):
        cs = slice(p * LANES, (p + 1) * LANES)
        cols.append(_dot(hi[:, cs], g) + _dot(lo[:, cs], g))
    return cols[0] if len(cols) == 1 else jnp.concatenate(cols, axis=1)


def _head_rms(x, gain):
    ms = _head_sum(x * x) * (1.0 / HEAD)
    return x * lax.rsqrt(ms + NORM_EPS) * gain


def _adaln(x, shift, scale):
    ms = jnp.mean(x * x, axis=-1, keepdims=True)
    return x * lax.rsqrt(ms + NORM_EPS) * (1.0 + scale) + shift


def _rope(x, cos, sin):
    w = x.shape[1]
    lane = lax.broadcasted_iota(jnp.int32, x.shape, 1)
    low = (lane & (HEAD - 1)) < HEAD // 2
    up = pltpu.roll(x, w - HEAD // 2, 1)
    dn = pltpu.roll(x, HEAD // 2, 1)
    return x * cos + jnp.where(low, -up, dn) * sin


def _softplus(y):
    return jnp.maximum(y, 0.0) + jnp.log(1.0 + jnp.exp(-jnp.abs(y)))


def _pick_rows(i, nb_ctx, ctx_ref, lat_ref):
    return jnp.where(i < nb_ctx, ctx_ref[...], lat_ref[...])


def _mod_kernel(c_ref, w_ref, b_ref, o_ref):
    c = c_ref[...]
    s = c * jax.nn.sigmoid(c)
    sh, sm, sl = _split3(s)
    wh, wm, wl = _split3(w_ref[...])
    acc = _dot(sh, wh) + (_dot(sh, wm) + _dot(sm, wh)) + (_dot(sh, wl) + _dot(sm, wm) + _dot(sl, wh))
    o_ref[...] = acc + b_ref[...]


def _modulation(cond8, ada_w, ada_b):
    depth, d, d6 = ada_w.shape
    tn = 1536
    return pl.pallas_call(
        _mod_kernel,
        grid=(depth, d6 // tn),
        in_specs=[
            pl.BlockSpec((8, d), lambda l, j: (0, 0)),
            pl.BlockSpec((None, d, tn), lambda l, j: (l, 0, j)),
            pl.BlockSpec((None, 1, tn), lambda l, j: (l, 0, j)),
        ],
        out_specs=pl.BlockSpec((None, 8, tn), lambda l, j: (l, 0, j)),
        out_shape=jax.ShapeDtypeStruct((depth, 8, d6), f32),
        compiler_params=_params(("parallel", "parallel")),
        name="modulation",
    )(cond8, ada_w, ada_b.reshape(depth, 1, d6))


def _ln_mm_kernel(*refs, tn, nb_ctx):
    if nb_ctx is None:
        x_ref, sh_ref, sc_ref, w_ref, o_ref = refs
        x = x_ref[...]
    else:
        xc_ref, xl_ref, sh_ref, sc_ref, w_ref, o_ref = refs
        x = _pick_rows(pl.program_id(0), nb_ctx, xc_ref, xl_ref)
    h = _adaln(x, sh_ref[...], sc_ref[...]).astype(bf16)
    for j in range(w_ref.shape[1] // tn):
        o_ref[:, j * tn:(j + 1) * tn] = _dot(h, w_ref[:, j * tn:(j + 1) * tn])


def _ln_matmul(xs, mod, w, lay, tm=512, tn=512):
    d = xs[0].shape[1]
    n_out = w.shape[1]
    if len(xs) == 1:
        x_specs = [pl.BlockSpec((tm, d), lambda i: (i, 0))]
        nb_ctx = None
    else:
        x_specs = list(_split_row_specs((tm, d), tm, lay))
        nb_ctx = lay.n_ctx // tm
    return pl.pallas_call(
        functools.partial(_ln_mm_kernel, tn=tn, nb_ctx=nb_ctx),
        grid=(lay.n // tm,),
        in_specs=x_specs + [_mod_spec(0, tm, lay), _mod_spec(1, tm, lay), _weight_spec(w)],
        out_specs=pl.BlockSpec((tm, n_out), lambda i: (i, 0)),
        out_shape=jax.ShapeDtypeStruct((lay.n, n_out), f32),
        compiler_params=_params(("parallel",)),
        name="adaln_proj",
    )(*xs, mod, mod, _weight_arg(w))


def _mlp_body(x, sh_ref, sc_ref, g_ref, w1_ref, w2_ref, acc_ref, o_ref, fc):
    h = _adaln(x, sh_ref[...], sc_ref[...]).astype(bf16)
    d_ff = w1_ref.shape[1]
    for c in range(d_ff // fc):
        a = _dot(h, w1_ref[:, c * fc:(c + 1) * fc])
        a = jnp.square(jnp.maximum(a, 0.0)).astype(bf16)
        part = _dot(a, w2_ref[c * fc:(c + 1) * fc, :])
        if c == 0:
            acc_ref[...] = part
        else:
            acc_ref[...] += part
    o_ref[...] = x + g_ref[...] * acc_ref[...]


def _mlp_kernel(x_ref, sh_ref, sc_ref, g_ref, w1_ref, w2_ref, o_ref, acc_ref, *, fc):
    _mlp_body(x_ref[...], sh_ref, sc_ref, g_ref, w1_ref, w2_ref, acc_ref, o_ref, fc)


def _mlp(x, mod, w1, w2, lay, tm=512, fc=1024):
    n, d = x.shape
    return pl.pallas_call(
        functools.partial(_mlp_kernel, fc=fc),
        grid=(n // tm,),
        in_specs=[
            pl.BlockSpec((tm, d), lambda i: (i, 0)),
            _mod_spec(3, tm, lay), _mod_spec(4, tm, lay), _mod_spec(5, tm, lay),
            _weight_spec(w1), _weight_spec(w2),
        ],
        out_specs=pl.BlockSpec((tm, d), lambda i: (i, 0)),
        out_shape=jax.ShapeDtypeStruct((n, d), f32),
        scratch_shapes=[pltpu.VMEM((tm, d), f32)],
        compiler_params=_params(("parallel",)),
        name="mlp",
    )(x, mod, mod, mod, _weight_arg(w1), _weight_arg(w2))


def _out_mlp_kernel(*refs, fc, nb_ctx, split_x, split_a):
    refs = list(refs)
    i = pl.program_id(0)
    x = _pick_rows(i, nb_ctx, refs.pop(0), refs.pop(0)) if split_x else refs.pop(0)[...]
    a = _pick_rows(i, nb_ctx, refs.pop(0), refs.pop(0)) if split_a else refs.pop(0)[...]
    gm_ref, wo_ref, sh_ref, sc_ref, gf_ref, w1_ref, w2_ref, o_ref, acc_ref = refs
    x1 = x + gm_ref[...] * _dot(a.astype(bf16), wo_ref[...])
    _mlp_body(x1, sh_ref, sc_ref, gf_ref, w1_ref, w2_ref, acc_ref, o_ref, fc)


def _out_proj_mlp(xs, a_parts, mod, wo, w1, w2, lay, rows=None, tm=512, fc=1024):
    d = D_MODEL
    k_in = a_parts[0].shape[1]
    tile0, n_tiles = rows if rows is not None else (0, lay.n // tm)
    split_x, split_a = len(xs) == 2, len(a_parts) == 2
    assert rows is None or not (split_x or split_a)
    x_specs = (list(_split_row_specs((tm, d), tm, lay)) if split_x
               else [pl.BlockSpec((tm, d), lambda i: (i + tile0, 0))])
    a_specs = (list(_split_row_specs((tm, k_in), tm, lay)) if split_a
               else [pl.BlockSpec((tm, k_in), lambda i: (i, 0))])
    return pl.pallas_call(
        functools.partial(_out_mlp_kernel, fc=fc, nb_ctx=lay.n_ctx // tm, split_x=split_x, split_a=split_a),
        grid=(n_tiles,),
        in_specs=x_specs + a_specs + [
            _mod_spec(2, tm, lay, tile0), _weight_spec(wo),
            _mod_spec(3, tm, lay, tile0), _mod_spec(4, tm, lay, tile0), _mod_spec(5, tm, lay, tile0),
            _weight_spec(w1), _weight_spec(w2),
        ],
        out_specs=pl.BlockSpec((tm, d), lambda i: (i, 0)),
        out_shape=jax.ShapeDtypeStruct((n_tiles * tm, d), f32),
        scratch_shapes=[pltpu.VMEM((tm, d), f32)],
        compiler_params=_params(("parallel",)),
        name="out_proj_mlp",
    )(*xs, *a_parts, mod, _weight_arg(wo), mod, mod, mod, _weight_arg(w1), _weight_arg(w2))


def _da_lambda(lam_ref, lam_init):
    lp = lam_ref[...]
    e1 = jnp.exp(jnp.sum(lp[0:1] * lp[1:2], axis=-1, keepdims=True))
    e2 = jnp.exp(jnp.sum(lp[2:3] * lp[3:4], axis=-1, keepdims=True))
    return e1 - e2 + lam_init


def _da_qk(q, k_bf16):
    lane = lax.broadcasted_iota(jnp.int32, q.shape, 1)
    return [_dot_nt(jnp.where((lane < HEAD) == (m == 0), q, 0.0).astype(bf16), k_bf16) for m in range(2)]


def _with_ones(v_bf16):
    return jnp.concatenate([v_bf16, jnp.ones_like(v_bf16)], axis=1)


def _da_pv(scores, v_ones, lam):
    outs = []
    for s in scores:
        e = jnp.exp2(s - jnp.max(s, axis=-1, keepdims=True))
        pv = _dot(e.astype(bf16), v_ones)
        outs.append(pv[:, :LANES] * (1.0 / pv[:, LANES:]))
    return outs[0] - lam * outs[1]


def _da_finish(o, sub_ref, lam_init):
    ms = jnp.mean(o * o, axis=-1, keepdims=True)
    return o * lax.rsqrt(ms + NORM_EPS) * sub_ref[...] * (1.0 - lam_init)


def _da_ctx_kernel(*refs, lam_init, layer):
    lam_ref, q_ref, k_ref, v_ref, qn_ref, kn_ref, sub_ref = refs[:7]
    o_ref, knew_ref, vnew_ref = refs[-3:]
    lam = _da_lambda(lam_ref, lam_init)
    q = _head_rms(q_ref[...], qn_ref[...]) * QK_SCALE
    k = _head_rms(k_ref[...], kn_ref[...])
    v = v_ref[...]
    if layer == 0:
        for l in range(knew_ref.shape[0]):
            knew_ref[l] = k if l == 0 else jnp.zeros_like(k)
            vnew_ref[l] = v if l == 0 else jnp.zeros_like(v)
    else:
        knew_ref[...] = k
        vnew_ref[...] = v
    kb = k.astype(bf16)
    vb = v.astype(bf16)
    cols = [slice(h * LANES, (h + 1) * LANES) for h in range(q.shape[1] // LANES)]
    s_next = _da_qk(q[:, cols[0]], kb[:, cols[0]])
    for h, cs in enumerate(cols):
        s_cur = s_next
        if h + 1 < len(cols):
            s_next = _da_qk(q[:, cols[h + 1]], kb[:, cols[h + 1]])
        o_ref[:, cs] = _da_finish(_da_pv(s_cur, _with_ones(vb[:, cs]), lam), sub_ref, lam_init)


def _da_lat_kernel(lam_ref, q_ref, k_ref, v_ref, ck_ref, cv_ref, qn_ref, kn_ref, sub_ref,
                   cq_ref, sq_ref, ckk_ref, skk_ref, o_ref, kf_scr, vf_scr, *, lam_init, past):
    cols = [slice(h * LANES, (h + 1) * LANES) for h in range(q_ref.shape[1] // LANES)]

    @pl.when(pl.program_id(2) == 0)
    def _():
        k = _rope(_head_rms(k_ref[...], kn_ref[...]), ckk_ref[...], skk_ref[...])
        kf_scr[0:past, :] = ck_ref[...].astype(bf16)
        kf_scr[past:, :] = k.astype(bf16)
        for h, cs in enumerate(cols):
            vf_scr[h, 0:past, :] = _with_ones(cv_ref[:, cs].astype(bf16))
            vf_scr[h, past:, :] = _with_ones(v_ref[:, cs].astype(bf16))

    lam = _da_lambda(lam_ref, lam_init)
    q = _rope(_head_rms(q_ref[...], qn_ref[...]), cq_ref[...], sq_ref[...]) * QK_SCALE
    s_next = _da_qk(q[:, cols[0]], kf_scr[:, cols[0]])
    for h, cs in enumerate(cols):
        s_cur = s_next
        if h + 1 < len(cols):
            s_next = _da_qk(q[:, cols[h + 1]], kf_scr[:, cols[h + 1]])
        o_ref[:, cs] = _da_finish(_da_pv(s_cur, vf_scr[h], lam), sub_ref, lam_init)


def _da_mixer(qkv, cache_k, cache_v, j, n_layers, kv_prev, p, lam_init, rope_tab, lay, tq=256):
    d = D_MODEL
    nh = DA_HEADS
    assert (kv_prev is None) == (j == 0)
    qn = jnp.tile(p['q_norm'], 2).reshape(1, LANES)
    kn = jnp.tile(p['k_norm'], 2).reshape(1, LANES)
    qn_all = jnp.tile(qn, (1, nh))
    kn_all = jnp.tile(kn, (1, nh))
    sub = p['subln'].reshape(1, LANES)
    small = lambda shape: pl.BlockSpec(shape, lambda *_: (0,) * len(shape))
    t = lay.t_ctx
    ctx_rows = pl.BlockSpec((t, d), lambda b: (b, 0))
    kv_shape = jax.ShapeDtypeStruct((lay.b_ctx, n_layers, t, d), f32)
    if kv_prev is None:
        kv_spec = pl.BlockSpec((None, n_layers, t, d), lambda b: (b, 0, 0, 0))
        kv_in_specs, kv_args, aliases = [], (), {}
    else:
        kv_spec = pl.BlockSpec((None, None, t, d), lambda b: (b, j, 0, 0))
        kv_in_specs = [pl.BlockSpec(memory_space=pl.ANY)] * 2
        kv_args, aliases = tuple(kv_prev), {7: 1, 8: 2}
    o_ctx, k_new, v_new = pl.pallas_call(
        functools.partial(_da_ctx_kernel, lam_init=lam_init, layer=j),
        grid=(lay.b_ctx,),
        in_specs=[
            small((4, HEAD)),
            ctx_rows,
            pl.BlockSpec((t, d), lambda b: (b, 1)),
            pl.BlockSpec((t, d), lambda b: (b, 2)),
            small((1, d)), small((1, d)), small((1, LANES)),
        ] + kv_in_specs,
        out_specs=[ctx_rows, kv_spec, kv_spec],
        out_shape=[jax.ShapeDtypeStruct((lay.n_ctx, d), f32), kv_shape, kv_shape],
        input_output_aliases=aliases,
        compiler_params=_params(("parallel",)),
        name="da_ctx",
    )(p['lam'], qkv, qkv, qkv, qn_all, kn_all, sub, *kv_args)

    tl = lay.t_lat
    past = cache_k.shape[2]
    row0 = lay.n_ctx // tl
    q0 = lay.n_ctx // tq
    nq = tl // tq
    cos, sin = rope_tab
    ck = cache_k.reshape(lay.b_lat, cache_k.shape[1], past, d)
    cv = cache_v.reshape(lay.b_lat, cache_v.shape[1], past, d)
    hg = 2
    gw = hg * LANES
    ng = nh // hg
    cos_g, sin_g = jnp.tile(cos, (1, hg)), jnp.tile(sin, (1, hg))
    once = pl.Buffered(1)
    o_lat = pl.pallas_call(
        functools.partial(_da_lat_kernel, lam_init=lam_init, past=past),
        grid=(lay.b_lat, ng, nq),
        in_specs=[
            small((4, HEAD)),
            pl.BlockSpec((tq, gw), lambda b, h, i: (q0 + b * nq + i, h)),
            pl.BlockSpec((tl, gw), lambda b, h, i: (row0 + b, ng + h), pipeline_mode=once),
            pl.BlockSpec((tl, gw), lambda b, h, i: (row0 + b, 2 * ng + h), pipeline_mode=once),
            pl.BlockSpec((None, None, past, gw), lambda b, h, i: (b, j, 0, h), pipeline_mode=once),
            pl.BlockSpec((None, None, past, gw), lambda b, h, i: (b, j, 0, h), pipeline_mode=once),
            small((1, gw)), small((1, gw)), small((1, LANES)),
            pl.BlockSpec((tq, gw), lambda b, h, i: (i, 0)),
            pl.BlockSpec((tq, gw), lambda b, h, i: (i, 0)),
            _resident((tl, gw)), _resident((tl, gw)),
        ],
        out_specs=pl.BlockSpec((tq, gw), lambda b, h, i: (b * nq + i, h)),
        out_shape=jax.ShapeDtypeStruct((lay.n_lat, d), f32),
        scratch_shapes=[pltpu.VMEM((past + tl, gw), bf16), pltpu.VMEM((hg, past + tl, 2 * LANES), bf16)],
        compiler_params=_params(("parallel", "parallel", "arbitrary")),
        name="da_lat",
    )(p['lam'], qkv, qkv, qkv, ck, cv, qn_all[:, :gw], kn_all[:, :gw], sub, cos_g, sin_g, cos_g, sin_g)
    return o_ctx, o_lat, k_new, v_new


def _swa_head_mask(shape, j):
    lane = lax.broadcasted_iota(jnp.int32, shape, 1)
    return (lane >= j * HEAD) & (lane < (j + 1) * HEAD)


def _swa_heads(q, sink_ref, score_fn, pv_fn, o_ref):
    heads = [(g, j) for g in range(SWA_GROUP) for j in range(SWA_KV_HEADS)]

    def scores(g, j):
        qg = q[:, g * SWA_KV:(g + 1) * SWA_KV]
        return score_fn(jnp.where(_swa_head_mask(qg.shape, j), qg, 0.0).astype(bf16))

    s_next = scores(*heads[0])
    o = None
    for idx, (g, j) in enumerate(heads):
        s_cur = s_next
        if idx + 1 < len(heads):
            s_next = scores(*heads[idx + 1])
        sink = sink_ref[j * SWA_GROUP + g] * LOG2E
        m = sink
        for s in s_cur:
            m = jnp.maximum(jnp.max(s, axis=-1, keepdims=True), m)
        exps = [jnp.exp2(s - m) for s in s_cur]
        den = jnp.exp2(sink - m)
        for e in exps:
            den = den + jnp.sum(e, axis=-1, keepdims=True)
        oj = pv_fn(j, [e.astype(bf16) for e in exps]) * (1.0 / den)
        o = oj if j == 0 else o + oj
        if j == SWA_KV_HEADS - 1:
            o_ref[:, g * SWA_KV:(g + 1) * SWA_KV] = o


def _swa_ctx_kernel(sink_ref, q_ref, k_ref, v_ref, qn_ref, kn_ref, o_ref, knew_ref, vnew_ref):
    q = _head_rms(q_ref[...], qn_ref[...]) * QK_SCALE
    k = _head_rms(k_ref[...], kn_ref[...])
    v = v_ref[...]
    knew_ref[...] = k
    vnew_ref[...] = v
    kb = k.astype(bf16)
    v_heads = [jnp.where(_swa_head_mask(v.shape, j), v, 0.0).astype(bf16) for j in range(SWA_KV_HEADS)]
    _swa_heads(q, sink_ref, lambda qm: [_dot_nt(qm, kb)], lambda j, e: _dot(e[0], v_heads[j]), o_ref)


def _swa_lat_kernel(sink_ref, q_ref, k_ref, v_ref, ck_ref, cv_ref, qn_ref, kn_ref,
                    cq_ref, sq_ref, ckk_ref, skk_ref, o_ref, kp_scr, vp_scr, *, t_lat):
    n = pl.program_id(1)
    w = WINDOW

    @pl.when(n == 0)
    def _():
        zeros = jnp.zeros((w, SWA_KV), bf16)
        k = _rope(_head_rms(k_ref[...], kn_ref[...]), ckk_ref[...], skk_ref[...])
        kp_scr[0:w, :] = zeros
        kp_scr[w:w + t_lat, :] = k.astype(bf16)
        kp_scr[w + t_lat:, :] = zeros
        vp_scr[0:w, :] = zeros
        vp_scr[w:w + t_lat, :] = v_ref[...].astype(bf16)
        vp_scr[w + t_lat:, :] = zeros

    q = _rope(_head_rms(q_ref[...], qn_ref[...]), cq_ref[...], sq_ref[...]) * QK_SCALE
    start = pl.multiple_of(n * w, w)
    kband = kp_scr[pl.ds(start, 3 * w), :]
    vband = vp_scr[pl.ds(start, 3 * w), :]
    ckb = ck_ref[...].astype(bf16)
    cv = cv_ref[...]
    qi = lax.broadcasted_iota(jnp.int32, (w, 3 * w), 0)
    si = lax.broadcasted_iota(jnp.int32, (w, 3 * w), 1)
    rel = si - qi
    key_pos = (n - 1) * w + si
    valid = (rel >= 0) & (rel <= 2 * w) & (key_pos >= 0) & (key_pos < t_lat)
    cv_heads = [jnp.where(_swa_head_mask(cv.shape, j), cv, 0.0).astype(bf16) for j in range(SWA_KV_HEADS)]
    vb_heads = [jnp.where(_swa_head_mask(vband.shape, j), vband, jnp.zeros_like(vband))
                for j in range(SWA_KV_HEADS)]
    _swa_heads(q, sink_ref,
               lambda qm: [_dot_nt(qm, ckb), jnp.where(valid, _dot_nt(qm, kband), NEG_INF)],
               lambda j, e: _dot(e[0], cv_heads[j]) + _dot(e[1], vb_heads[j]),
               o_ref)


def _swa_group_major_weights(wqkv, wo):
    d = D_MODEL
    wq = wqkv[:, :d].reshape(d, SWA_KV_HEADS, SWA_GROUP, HEAD).transpose(0, 2, 1, 3).reshape(d, d)
    wo = wo.reshape(SWA_KV_HEADS, SWA_GROUP, HEAD, d).transpose(1, 0, 2, 3).reshape(d, d)
    return jnp.concatenate([wq, wqkv[:, d:]], axis=1).astype(bf16), wo.astype(bf16)


def _swa_mixer(qkv, cache_k, cache_v, j, p, rope_tab, lay):
    d = D_MODEL
    qn = jnp.tile(p['q_norm'], d // HEAD).reshape(1, d)
    kn = jnp.tile(p['k_norm'], SWA_KV_HEADS).reshape(1, SWA_KV)
    small = lambda shape: pl.BlockSpec(shape, lambda *_: (0,) * len(shape))
    smem = pl.BlockSpec(memory_space=pltpu.SMEM)
    t = lay.t_ctx
    kcol = d // SWA_KV
    kv_rows = pl.BlockSpec((t, SWA_KV), lambda b: (b, 0))
    o_ctx, k_new, v_new = pl.pallas_call(
        _swa_ctx_kernel,
        grid=(lay.b_ctx,),
        in_specs=[
            smem,
            pl.BlockSpec((t, d), lambda b: (b, 0)),
            pl.BlockSpec((t, SWA_KV), lambda b: (b, kcol)),
            pl.BlockSpec((t, SWA_KV), lambda b: (b, kcol + 1)),
            small((1, d)), small((1, SWA_KV)),
        ],
        out_specs=[pl.BlockSpec((t, d), lambda b: (b, 0)), kv_rows, kv_rows],
        out_shape=[jax.ShapeDtypeStruct((lay.n_ctx, d), f32),
                   jax.ShapeDtypeStruct((lay.n_ctx, SWA_KV), f32),
                   jax.ShapeDtypeStruct((lay.n_ctx, SWA_KV), f32)],
        compiler_params=_params(("parallel",)),
        name="swa_ctx",
    )(p['sink'], qkv, qkv, qkv, qn, kn)

    tl = lay.t_lat
    w = WINDOW
    past = cache_k.shape[2]
    row0 = lay.n_ctx // tl
    q0 = lay.n_ctx // w
    nq = tl // w
    cos, sin = rope_tab
    ck = cache_k.reshape(lay.b_lat, cache_k.shape[1], past, SWA_KV)
    cv = cache_v.reshape(lay.b_lat, cache_v.shape[1], past, SWA_KV)
    o_lat = pl.pallas_call(
        functools.partial(_swa_lat_kernel, t_lat=tl),
        grid=(lay.b_lat, nq),
        in_specs=[
            smem,
            pl.BlockSpec((w, d), lambda b, i: (q0 + b * nq + i, 0)),
            pl.BlockSpec((tl, SWA_KV), lambda b, i: (row0 + b, kcol)),
            pl.BlockSpec((tl, SWA_KV), lambda b, i: (row0 + b, kcol + 1)),
            pl.BlockSpec((None, None, past, SWA_KV), lambda b, i: (b, j, 0, 0)),
            pl.BlockSpec((None, None, past, SWA_KV), lambda b, i: (b, j, 0, 0)),
            small((1, d)), small((1, SWA_KV)),
            pl.BlockSpec((w, d), lambda b, i: (i, 0)),
            pl.BlockSpec((w, d), lambda b, i: (i, 0)),
            pl.BlockSpec((tl, SWA_KV), lambda b, i: (0, 0)),
            pl.BlockSpec((tl, SWA_KV), lambda b, i: (0, 0)),
        ],
        out_specs=pl.BlockSpec((w, d), lambda b, i: (b * nq + i, 0)),
        out_shape=jax.ShapeDtypeStruct((lay.n_lat, d), f32),
        scratch_shapes=[pltpu.VMEM((tl + 2 * w, SWA_KV), bf16), pltpu.VMEM((tl + 2 * w, SWA_KV), bf16)],
        compiler_params=_params(("parallel", "arbitrary")),
        name="swa_lat",
    )(p['sink'], qkv, qkv, qkv, ck, cv, qn, kn, cos, sin, cos, sin)
    return o_ctx, o_lat, k_new, v_new


def _rwkv_proj_kernel(x_ref, xp_ref, xn_ref, sh_ref, sc_ref, mu_ref, wrkv_ref, w1_ref, w2_ref, w0_ref,
                      a1_ref, a2_ref, a0_ref, g1_ref, g2_ref,
                      r_ref, k_ref, v_ref, g_ref, a_ref, lw_ref, *, lay, tm):
    i = pl.program_id(0)
    r0 = i * tm
    in_ctx = r0 < lay.n_ctx
    pos = jnp.where(in_ctx, r0 % lay.t_ctx, (r0 - lay.n_ctx) % lay.t_lat)
    seq_len = jnp.where(in_ctx, lay.t_ctx, lay.t_lat)
    keep_prev = jnp.where(pos == 0, 0.0, 1.0)
    keep_next = jnp.where(pos + tm == seq_len, 0.0, 1.0)

    sh = sh_ref[...]
    sc = sc_ref[...]
    h = _adaln(x_ref[...], sh, sc)
    h_prev_row = _adaln(xp_ref[7:8, :], sh, sc) * keep_prev
    h_next_row = _adaln(xn_ref[0:1, :], sh, sc) * keep_next
    row = lax.broadcasted_iota(jnp.int32, h.shape, 0)
    prev = jnp.where(row == 0, h_prev_row, pltpu.roll(h, 1, 0))
    nxt = jnp.where(row == tm - 1, h_next_row, pltpu.roll(h, tm - 1, 0))
    xx = 0.5 * (prev + nxt) - h

    def mix(m):
        return (h + xx * mu_ref[m:m + 1, :]).astype(bf16)

    r_ref[...] = _dot(mix(0), wrkv_ref[0]).astype(r_ref.dtype)
    k_ref[...] = _dot(mix(1), wrkv_ref[1]).astype(k_ref.dtype)
    v_ref[...] = _dot(mix(2), wrkv_ref[2]).astype(v_ref.dtype)

    lane = lax.broadcasted_iota(jnp.int32, (tm, LANES), 1)
    first = lane < HEAD

    tw = jnp.tanh(_dot(mix(3), w1_ref[...]))
    la = _dot(mix(4), a1_ref[...])
    for z in range(2):
        sel = first if z == 0 else jnp.logical_not(first)
        w_lora = _dot(jnp.where(sel, tw, 0.0).astype(bf16), w2_ref[...])
        w_log = -_softplus(-(w0_ref[z:z + 1, :] + w_lora)) - 0.5
        lw_ref[z] = -jnp.exp(w_log)
        a_lora = _dot(jnp.where(sel, la, 0.0).astype(bf16), a2_ref[...])
        a_ref[z] = jax.nn.sigmoid(a0_ref[z:z + 1, :] + a_lora).astype(a_ref.dtype)

    g = _dot(jax.nn.sigmoid(_dot(mix(5), g1_ref[...])).astype(bf16), g2_ref[...])
    g_ref[...] = g.astype(g_ref.dtype)


def _rwkv_proj(x, mod, p, lay, tm=256):
    n, d = x.shape
    assert lay.t_ctx % tm == 0 and lay.t_lat % tm == 0
    nb8 = n // 8
    wrkv = p['wrkv'].astype(bf16)
    w1 = jnp.concatenate([p['w1'][0], p['w1'][1]], axis=1).astype(bf16)
    w2 = p['w2'].reshape(2 * p['w2'].shape[1], d).astype(bf16)
    a1 = jnp.concatenate([p['a1'][0], p['a1'][1]], axis=1).astype(bf16)
    a2 = p['a2'].reshape(2 * p['a2'].shape[1], d).astype(bf16)
    g1 = p['g1'].astype(bf16)
    g2 = p['g2'].astype(bf16)
    row_spec = pl.BlockSpec((tm, d), lambda i: (i, 0))
    z_spec = pl.BlockSpec((2, tm, d), lambda i: (0, i, 0))
    args = (x, x, x, mod, mod, p['mu'], wrkv, w1, w2, p['w0'], a1, a2, p['a0'], g1, g2)
    return pl.pallas_call(
        functools.partial(_rwkv_proj_kernel, lay=lay, tm=tm),
        grid=(n // tm,),
        in_specs=[
            row_spec,
            pl.BlockSpec((8, d), lambda i: (jnp.maximum(i * (tm // 8) - 1, 0), 0)),
            pl.BlockSpec((8, d), lambda i: (jnp.minimum((i + 1) * (tm // 8), nb8 - 1), 0)),
            _mod_spec(0, tm, lay),
            _mod_spec(1, tm, lay),
        ] + [_resident(a.shape) for a in args[5:]],
        out_specs=[row_spec] * 4 + [z_spec] * 2,
        out_shape=[jax.ShapeDtypeStruct((n, d), bf16)] * 4
        + [jax.ShapeDtypeStruct((2, n, d), bf16), jax.ShapeDtypeStruct((2, n, d), f32)],
        compiler_params=_params(("parallel",)),
        name="rwkv_proj",
    )(*args)


def _wkv_prep(r, k, v, a, lw, kk_gain, ka, forward):
    c_len = r.shape[0]
    r, k, v, a = (z.astype(f32) for z in (r, k, v, a))
    kkr = k * kk_gain
    kk = kkr * lax.rsqrt(jnp.maximum(_head_sum(kkr * kkr), 1e-24))
    beta = kk * a
    kd = k * (1.0 + (a - 1.0) * ka)
    row = lax.broadcasted_iota(jnp.int32, lw.shape, 0)
    cum = lw
    step = 1
    while step < c_len:
        if forward:
            cum = cum + jnp.where(row >= step, pltpu.roll(cum, step, 0), 0.0)
        else:
            cum = cum + jnp.where(row < c_len - step, pltpu.roll(cum, c_len - step, 0), 0.0)
        step *= 2
    total = cum[c_len - 1:c_len, :] if forward else cum[0:1, :]
    inv = jnp.exp(-cum)
    rem = jnp.exp(total - cum)
    return dict(a_t=-kk * jnp.exp(cum - lw), r_t=r * jnp.exp(cum), b_inv=beta * inv, k_inv=kd * inv,
                b_rem=beta * rem, k_rem=kd * rem, p_total=jnp.exp(total), v=v)


def _wkv_kernel(*refs, has_init):
    fwd_refs, bwd_refs = refs[0:5], refs[5:10]
    kk_ref, ka_ref = refs[10:12]
    n_in = 13 if has_init else 12
    yf_ref, yb_ref, sout_ref, s_scr = refs[n_in:]
    c_len = WKV_CHUNK
    gw = WKV_LANES
    hpg = gw // HEAD
    n_grp = yf_ref.shape[1] // gw
    c = pl.program_id(1)

    @pl.when(c == 0)
    def _():
        s_scr[...] = jnp.zeros_like(s_scr)
        if has_init:
            s0_ref = refs[12]
            for dr in range(2):
                for h in range(n_grp * hpg):
                    o = (h % hpg) * HEAD
                    s_scr[dr, h // hpg, o:o + HEAD, o:o + HEAD] = s0_ref[dr, h]

    t_s = lax.broadcasted_iota(jnp.int32, (c_len, gw), 0)
    i_s = lax.broadcasted_iota(jnp.int32, (c_len, gw), 1) & (c_len - 1)
    strict = (i_s < t_s, i_s > t_s)
    incl = (i_s <= t_s, i_s >= t_s)
    eye = jnp.where(t_s == i_s, 1.0, 0.0)
    pair_block = (t_s >> 1) == (i_s >> 1)
    levels = []
    for lb in range(1, int(math.log2(c_len))):
        levels.append(((t_s >> (lb + 1)) == (i_s >> (lb + 1))) & ((t_s >> lb) != (i_s >> lb)))
    rb = lax.broadcasted_iota(jnp.int32, (hpg * c_len, gw), 0) // c_len
    cb = lax.broadcasted_iota(jnp.int32, (hpg * c_len, gw), 1) // HEAD
    same_head = rb == cb

    def bd(x):
        return jnp.where(same_head, jnp.concatenate([x] * hpg, axis=0), 0.0).astype(bf16)

    ops = [_wkv_prep(*(ref[...] for ref in d_refs), kk_ref[...], ka_ref[...], forward=(dr == 0))
           for dr, d_refs in enumerate((fwd_refs, bwd_refs))]

    chains = [(dr, pr, slice(pr * gw, (pr + 1) * gw)) for dr in range(2) for pr in range(n_grp)]
    ids = range(len(chains))
    lhs = [jnp.concatenate([ops[dr]['a_t'][:, cs], ops[dr]['r_t'][:, cs]], axis=0).astype(bf16)
           for dr, pr, cs in chains]
    s0 = [s_scr[dr, pr] for dr, pr, cs in chains]
    sc = [_dot_nt(lhs[i], jnp.concatenate([bd(ops[dr]['b_inv'][:, cs]), bd(ops[dr]['k_inv'][:, cs]),
                                           s0[i].astype(bf16)], axis=0))
          for i, (dr, pr, cs) in enumerate(chains)]
    sc_b = [s[:, :gw] for s in sc]
    sc_k = [s[:, gw:2 * gw] for s in sc]
    x0 = [s[:, 2 * gw:] for s in sc]
    n_mat = [jnp.where(strict[dr], sc_b[i][:c_len], 0.0) for i, (dr, pr, cs) in enumerate(chains)]
    v_bd = [bd(ops[dr]['v'][:, cs]) for dr, pr, cs in chains]
    x = [x0[i][:c_len] + _dot(jnp.where(strict[dr], sc_k[i][:c_len], 0.0).astype(bf16), v_bd[i])
         for i, (dr, pr, cs) in enumerate(chains)]
    t_inv = [eye + jnp.where(pair_block, n_mat[i], 0.0) for i in ids]
    for off_diag in levels:
        half = [_dot(t_inv[i].astype(bf16), bd(jnp.where(off_diag, n_mat[i], 0.0))) for i in ids]
        t_inv = [t_inv[i] + _dot(half[i].astype(bf16), bd(t_inv[i])) for i in ids]
    uv_bd = [jnp.concatenate([bd(_dot(t_inv[i].astype(bf16), bd(x[i]))), v_bd[i]], axis=0) for i in ids]
    for i, (dr, pr, cs) in enumerate(chains):
        rbk = jnp.concatenate([jnp.where(incl[dr], sc_b[i][c_len:], 0.0),
                               jnp.where(incl[dr], sc_k[i][c_len:], 0.0)], axis=1).astype(bf16)
        y_ref = yf_ref if dr == 0 else yb_ref
        y_ref[:, cs] = x0[i][c_len:] + _dot(rbk, uv_bd[i])
    for i, (dr, pr, cs) in enumerate(chains):
        bk_rem = jnp.concatenate([bd(ops[dr]['b_rem'][:, cs]), bd(ops[dr]['k_rem'][:, cs])], axis=0)
        s_scr[dr, pr] = s0[i] * ops[dr]['p_total'][:, cs] + _dot_tn(uv_bd[i], bk_rem)

    @pl.when(c == pl.num_programs(1) - 1)
    def _():
        for dr in range(2):
            for h in range(n_grp * hpg):
                o = (h % hpg) * HEAD
                sout_ref[dr, h] = s_scr[dr, h // hpg, o:o + HEAD, o:o + HEAD]


def _wkv_scan(r, k, v, a2, lw2, p, s0, row0, n_seq, t_len):
    d = D_MODEL
    c_len = WKV_CHUNK
    nc = t_len // c_len
    b0 = row0 // c_len
    n_grp = d // WKV_LANES
    has_init = s0 is not None

    fwd_blk = lambda s, c: s * nc + c
    bwd_blk = lambda s, c: s * nc + (nc - 1 - c)
    tok_f = pl.BlockSpec((c_len, d), lambda s, c: (b0 + fwd_blk(s, c), 0))
    tok_b = pl.BlockSpec((c_len, d), lambda s, c: (b0 + bwd_blk(s, c), 0))
    z_f = pl.BlockSpec((None, c_len, d), lambda s, c: (0, b0 + fwd_blk(s, c), 0))
    z_b = pl.BlockSpec((None, c_len, d), lambda s, c: (1, b0 + bwd_blk(s, c), 0))
    vec = pl.BlockSpec((1, d), lambda s, c: (0, 0))
    st = pl.BlockSpec((None, 2, RWKV_HEADS, HEAD, HEAD), lambda s, c: (s, 0, 0, 0, 0))
    in_specs = [tok_f, tok_f, tok_f, z_f, z_f, tok_b, tok_b, tok_b, z_b, z_b, vec, vec]
    args = [r, k, v, a2, lw2, r, k, v, a2, lw2, p['k_k'].reshape(1, d), p['k_a'].reshape(1, d)]
    if has_init:
        in_specs.append(st)
        args.append(s0)
    return pl.pallas_call(
        functools.partial(_wkv_kernel, has_init=has_init),
        grid=(n_seq, nc),
        in_specs=in_specs,
        out_specs=[
            pl.BlockSpec((c_len, d), lambda s, c: (fwd_blk(s, c), 0)),
            pl.BlockSpec((c_len, d), lambda s, c: (bwd_blk(s, c), 0)),
            st,
        ],
        out_shape=[jax.ShapeDtypeStruct((n_seq * t_len, d), f32)] * 2
        + [jax.ShapeDtypeStruct((n_seq, 2, RWKV_HEADS, HEAD, HEAD), f32)],
        scratch_shapes=[pltpu.VMEM((2, n_grp, WKV_LANES, WKV_LANES), f32)],
        compiler_params=_params(("parallel", "arbitrary")),
        name="wkv_scan",
    )(*args)


def _rwkv_post_kernel(x_ref, yfc_ref, ybc_ref, yfl_ref, ybl_ref, r_ref, k_ref, v_ref, g_ref, a_ref,
                      ka_ref, rk_ref, lnw_ref, lnb_ref, gate_ref, wo_ref, o_ref, *, nb_ctx):
    i = pl.program_id(0)
    y = _pick_rows(i, nb_ctx, yfc_ref, yfl_ref) + _pick_rows(i, nb_ctx, ybc_ref, ybl_ref)
    mu = _head_sum(y) * (1.0 / HEAD)
    yc = y - mu
    var = _head_sum(yc * yc) * (1.0 / HEAD)
    yn = yc * lax.rsqrt(var + RWKV_GN_EPS) * lnw_ref[...] + lnb_ref[...]
    k = k_ref[...].astype(f32)
    ka = ka_ref[...]
    a0 = a_ref[0].astype(f32)
    a1 = a_ref[1].astype(f32)
    kd_sum = k * (1.0 + (a0 - 1.0) * ka) + k * (1.0 + (a1 - 1.0) * ka)
    bonus = _head_sum(r_ref[...].astype(f32) * kd_sum * rk_ref[...]) * v_ref[...].astype(f32)
    o = ((yn + bonus) * g_ref[...].astype(f32)).astype(bf16)
    o_ref[...] = x_ref[...] + gate_ref[...] * _dot(o, wo_ref[...])


def _rwkv_post_residual(x, mod, y_ctx, y_lat, r, k, v, g, a2, p, wo_bf16, lay, tm=256):
    n, d = r.shape
    tok = pl.BlockSpec((tm, d), lambda i: (i, 0))
    tok_z = pl.BlockSpec((2, tm, d), lambda i: (0, i, 0))
    vec = pl.BlockSpec((1, d), lambda i: (0, 0))
    yc_spec, yl_spec = _split_row_specs((tm, d), tm, lay)
    row = lambda a: a.reshape(1, d)
    return pl.pallas_call(
        functools.partial(_rwkv_post_kernel, nb_ctx=lay.n_ctx // tm),
        grid=(n // tm,),
        in_specs=[tok, yc_spec, yc_spec, yl_spec, yl_spec, tok, tok, tok, tok, tok_z, vec, vec, vec, vec,
                  _mod_spec(2, tm, lay), _resident((d, d))],
        out_specs=tok,
        out_shape=jax.ShapeDtypeStruct((n, d), f32),
        compiler_params=_params(("parallel",)),
        name="rwkv_post",
    )(x, y_ctx[0], y_ctx[1], y_lat[0], y_lat[1], r, k, v, g, a2,
      row(p['k_a']), row(p['r_k']), row(p['ln_w']), row(p['ln_b']), mod, wo_bf16)


def _rwkv_layer(x, mod, state0, p, wo_bf16, lay):
    r, k, v, g, a2, lw2 = _rwkv_proj(x, mod, p, lay)
    yf_c, yb_c, s_ctx = _wkv_scan(r, k, v, a2, lw2, p, None, 0, lay.b_ctx, lay.t_ctx)
    yf_l, yb_l, _ = _wkv_scan(r, k, v, a2, lw2, p, state0, lay.n_ctx, lay.b_lat, lay.t_lat)
    x = _rwkv_post_residual(x, mod, (yf_c, yb_c), (yf_l, yb_l), r, k, v, g, a2, p, wo_bf16, lay)
    return x, s_ctx


def _rope_table(n_tok, width):
    n_rows = n_tok // GRID_W
    rows = jnp.broadcast_to(jnp.arange(n_rows, dtype=f32)[:, None], (n_rows, GRID_W)).reshape(-1)
    cols = jnp.broadcast_to(jnp.arange(GRID_W, dtype=f32)[None, :], (n_rows, GRID_W)).reshape(-1)
    n_freq = HEAD // 4
    inv = ROPE_BASE ** (-jnp.arange(n_freq, dtype=f32) / n_freq)
    ang = jnp.concatenate([rows[:, None] * inv, cols[:, None] * inv], axis=-1)
    reps = width // (HEAD // 2)
    return jnp.tile(jnp.cos(ang), (1, reps)), jnp.tile(jnp.sin(ang), (1, reps))


def kernel(x_prompt, x_sample, cache_da_k, cache_da_v, state_rwkv, cache_swa_k, cache_swa_v, c, c_ctx, ada_w, ada_b, mlp_w1, mlp_w2, da_wqkv, da_q_norm, da_k_norm, da_lambda, da_subln, da_wo, rwkv_mu, rwkv_wrkv, rwkv_w0, rwkv_w1, rwkv_w2, rwkv_a0, rwkv_a1, rwkv_a2, rwkv_g1, rwkv_g2, rwkv_k_k, rwkv_k_a, rwkv_r_k, rwkv_ln_w, rwkv_ln_b, rwkv_wo, swa_wqkv, swa_q_norm, swa_k_norm, swa_sink, swa_wo):
    b_ctx, t_ctx, d = x_prompt.shape
    b_lat, t_lat, _ = x_sample.shape
    lay = Layout(b_ctx, t_ctx, b_lat, t_lat)
    depth = ada_w.shape[0]
    tm = 512
    assert d == D_MODEL and lay.n_ctx % t_lat == 0

    xs = (x_prompt.reshape(lay.n_ctx, d), x_sample.reshape(lay.n_lat, d))
    cond = jnp.concatenate([c_ctx[None, :], c, jnp.zeros((8 - 1 - b_lat, d), f32)], axis=0)
    mods = _modulation(cond, ada_w, ada_b)
    mods = mods[:, :1 + b_lat].reshape(depth, 1 + b_lat, 6, 1, d)

    rope_da = _rope_table(t_lat, LANES)
    rope_swa = _rope_table(t_lat, d)

    mlp_w1_b, mlp_w2_b = mlp_w1.astype(bf16), mlp_w2.astype(bf16)
    da_wqkv_b, da_wo_b = da_wqkv.astype(bf16), da_wo.astype(bf16)
    n_da = da_wqkv.shape[0]
    da_kv = None
    rwkv_s, swa_k, swa_v = [], [], []
    for i in range(depth):
        kind = i % N_MIXERS
        j = i // N_MIXERS
        mod = mods[i]
        w1 = LayerWeight(mlp_w1_b, i)
        w2 = LayerWeight(mlp_w2_b, i)
        if kind == 1:
            p = {'mu': rwkv_mu[j], 'wrkv': rwkv_wrkv[j], 'w0': rwkv_w0[j], 'w1': rwkv_w1[j],
                 'w2': rwkv_w2[j], 'a0': rwkv_a0[j], 'a1': rwkv_a1[j], 'a2': rwkv_a2[j],
                 'g1': rwkv_g1[j], 'g2': rwkv_g2[j], 'k_k': rwkv_k_k[j], 'k_a': rwkv_k_a[j],
                 'r_k': rwkv_r_k[j], 'ln_w': rwkv_ln_w[j], 'ln_b': rwkv_ln_b[j]}
            x = xs[0] if len(xs) == 1 else jnp.concatenate(xs, axis=0)
            x, s_new = _rwkv_layer(x, mod, state_rwkv[:, j], p, rwkv_wo[j].astype(bf16), lay)
            rwkv_s.append(s_new)
            xs = (_mlp(x, mod, w1, w2, lay),)
            continue
        if kind == 0:
            p = {'q_norm': da_q_norm[j], 'k_norm': da_k_norm[j], 'lam': da_lambda[j], 'subln': da_subln[j]}
            lam_init = 0.8 - 0.6 * math.exp(-0.3 * i)
            qkv = _ln_matmul(xs, mod, LayerWeight(da_wqkv_b, j), lay)
            o_ctx, o_lat, *da_kv = _da_mixer(qkv, cache_da_k, cache_da_v, j, n_da, da_kv, p, lam_init,
                                             rope_da, lay)
            wo = LayerWeight(da_wo_b, j)
        else:
            p = {'q_norm': swa_q_norm[j], 'k_norm': swa_k_norm[j], 'sink': swa_sink[j]}
            wqkv, wo = _swa_group_major_weights(swa_wqkv[j], swa_wo[j])
            qkv = _ln_matmul(xs, mod, wqkv, lay)
            o_ctx, o_lat, k_new, v_new = _swa_mixer(qkv, cache_swa_k, cache_swa_v, j, p, rope_swa, lay)
            swa_k.append(k_new.reshape(b_ctx, t_ctx, SWA_KV_HEADS, HEAD))
            swa_v.append(v_new.reshape(b_ctx, t_ctx, SWA_KV_HEADS, HEAD))
        if i == depth - 1 and len(xs) == 1:
            nb_ctx = lay.n_ctx // tm
            xs = (_out_proj_mlp(xs, (o_ctx,), mod, wo, w1, w2, lay, rows=(0, nb_ctx), tm=tm),
                  _out_proj_mlp(xs, (o_lat,), mod, wo, w1, w2, lay, rows=(nb_ctx, lay.n_lat // tm), tm=tm))
        else:
            xs = (_out_proj_mlp(xs, (o_ctx, o_lat), mod, wo, w1, w2, lay, tm=tm),)

    if len(xs) == 1:
        xs = (xs[0][:lay.n_ctx], xs[0][lay.n_ctx:])
    y_prompt = xs[0].reshape(b_ctx, t_ctx, d)
    y_sample = xs[1].reshape(b_lat, t_lat, d)
    new_da_k, new_da_v = (a.reshape(b_ctx, n_da, t_ctx, DA_HEADS, LANES) for a in da_kv)
    return (y_prompt, y_sample, new_da_k, new_da_v,
            jnp.stack(rwkv_s, axis=1), jnp.stack(swa_k, axis=1), jnp.stack(swa_v, axis=1))
```

```python
import functools
import math
from typing import NamedTuple

import jax
import jax.numpy as jnp
from jax import lax
from jax.experimental import pallas as pl
from jax.experimental.pallas import tpu as pltpu

f32 = jnp.float32
bf16 = jnp.bfloat16

D_MODEL = 1024
N_MIXERS = 3
GRID_W = 64
ROPE_BASE = 10000.0
NORM_EPS = 1e-6
NEG_INF = -1e30
LOG2E = math.log2(math.e)
HEAD = 64
LANES = 128
DA_HEADS = D_MODEL // LANES
RWKV_HEADS = D_MODEL // HEAD
RWKV_GN_EPS = 64e-5
SWA_HEADS = D_MODEL // HEAD
SWA_KV_HEADS = SWA_HEADS // 4
SWA_GROUP = SWA_HEADS // SWA_KV_HEADS
SWA_KV = SWA_KV_HEADS * HEAD
WINDOW = 128
WKV_CHUNK = 64
WKV_LANES = 128
VMEM_LIMIT = 56 * 1024 * 1024
QK_SCALE = HEAD ** -0.5 * LOG2E


class Layout(NamedTuple):
    b_ctx: int
    t_ctx: int
    b_lat: int
    t_lat: int

    @property
    def n_ctx(self):
        return self.b_ctx * self.t_ctx

    @property
    def n_lat(self):
        return self.b_lat * self.t_lat

    @property
    def n(self):
        return self.n_ctx + self.n_lat


def _params(sem):
    return pltpu.CompilerParams(dimension_semantics=sem, vmem_limit_bytes=VMEM_LIMIT)


def _group_of_tile(i, tm, lay):
    r = i * tm
    return jnp.where(r < lay.n_ctx, 0, 1 + (r - lay.n_ctx) // lay.t_lat)


def _mod_spec(chunk, tm, lay, tile0=0):
    return pl.BlockSpec((None, None, 1, D_MODEL),
                        lambda i, *_: (_group_of_tile(i + tile0, tm, lay), chunk, 0, 0))


def _resident(shape):
    return pl.BlockSpec(shape, lambda *_: (0,) * len(shape), pipeline_mode=pl.Buffered(1))


class LayerWeight(NamedTuple):
    stack: jax.Array
    layer: int

    @property
    def shape(self):
        return self.stack.shape[1:]

    def spec(self):
        idx = (self.layer,) + (0,) * (self.stack.ndim - 1)
        return pl.BlockSpec((None,) + self.shape, lambda *_: idx, pipeline_mode=pl.Buffered(1))


def _weight_spec(w):
    return w.spec() if isinstance(w, LayerWeight) else _resident(w.shape)


def _weight_arg(w):
    return w.stack if isinstance(w, LayerWeight) else w


def _split_row_specs(block, tm, lay):
    nb_ctx = lay.n_ctx // tm
    lead = (0,) * (len(block) - 2)
    ctx = pl.BlockSpec(block, lambda i, *_: lead + (jnp.minimum(i, nb_ctx - 1), 0))
    lat = pl.BlockSpec(block, lambda i, *_: lead + (jnp.maximum(i - nb_ctx, 0), 0))
    return ctx, lat


def _dot(a, b):
    return jnp.dot(a, b, preferred_element_type=f32)


def _dot_nt(a, b):
    return lax.dot_general(a, b, (((1,), (1,)), ((), ())), preferred_element_type=f32)


def _dot_tn(a, b):
    return lax.dot_general(a, b, (((0,), (0,)), ((), ())), preferred_element_type=f32)


def _split3(x):
    hi = x.astype(bf16)
    r1 = x - hi.astype(f32)
    mid = r1.astype(bf16)
    lo = (r1 - mid.astype(f32)).astype(bf16)
    return hi, mid, lo


def _head_ones():
    r = lax.broadcasted_iota(jnp.int32, (LANES, LANES), 0)
    c = lax.broadcasted_iota(jnp.int32, (LANES, LANES), 1)
    return ((r // HEAD) == (c // HEAD)).astype(bf16)


def _head_sum(x):
    g = _head_ones()
    hi = x.astype(bf16)
    lo = (x - hi.astype(f32)).astype(bf16)
    cols = []
    for p in range(x.shape[1] // LANES):
        cs = slice(p * LANES, (p + 1) * LANES)
        cols.append(_dot(hi[:, cs], g) + _dot(lo[:, cs], g))
    return cols[0] if len(cols) == 1 else jnp.concatenate(cols, axis=1)


def _head_rms(x, gain):
    ms = _head_sum(x * x) * (1.0 / HEAD)
    return x * lax.rsqrt(ms + NORM_EPS) * gain


def _adaln(x, shift, scale):
    ms = jnp.mean(x * x, axis=-1, keepdims=True)
    return x * lax.rsqrt(ms + NORM_EPS) * (1.0 + scale) + shift


def _rope(x, cos, sin):
    w = x.shape[1]
    lane = lax.broadcasted_iota(jnp.int32, x.shape, 1)
    low = (lane & (HEAD - 1)) < HEAD // 2
    up = pltpu.roll(x, w - HEAD // 2, 1)
    dn = pltpu.roll(x, HEAD // 2, 1)
    return x * cos + jnp.where(low, -up, dn) * sin


def _softplus(y):
    return jnp.maximum(y, 0.0) + jnp.log(1.0 + jnp.exp(-jnp.abs(y)))


def _pick_rows(i, nb_ctx, ctx_ref, lat_ref):
    return jnp.where(i < nb_ctx, ctx_ref[...], lat_ref[...])


def _mod_kernel(c_ref, w_ref, b_ref, o_ref):
    c = c_ref[...]
    s = c * jax.nn.sigmoid(c)
    sh, sm, sl = _split3(s)
    wh, wm, wl = _split3(w_ref[...])
    acc = _dot(sh, wh) + (_dot(sh, wm) + _dot(sm, wh)) + (_dot(sh, wl) + _dot(sm, wm) + _dot(sl, wh))
    o_ref[...] = acc + b_ref[...]


def _modulation(cond8, ada_w, ada_b):
    depth, d, d6 = ada_w.shape
    tn = 1536
    return pl.pallas_call(
        _mod_kernel,
        grid=(depth, d6 // tn),
        in_specs=[
            pl.BlockSpec((8, d), lambda l, j: (0, 0)),
            pl.BlockSpec((None, d, tn), lambda l, j: (l, 0, j)),
            pl.BlockSpec((None, 1, tn), lambda l, j: (l, 0, j)),
        ],
        out_specs=pl.BlockSpec((None, 8, tn), lambda l, j: (l, 0, j)),
        out_shape=jax.ShapeDtypeStruct((depth, 8, d6), f32),
        compiler_params=_params(("parallel", "parallel")),
        name="modulation",
    )(cond8, ada_w, ada_b.reshape(depth, 1, d6))


def _ln_mm_kernel(*refs, tn, nb_ctx):
    if nb_ctx is None:
        x_ref, sh_ref, sc_ref, w_ref, q_ref, kv_ref = refs
        x = x_ref[...]
    else:
        xc_ref, xl_ref, sh_ref, sc_ref, w_ref, q_ref, kv_ref = refs
        x = _pick_rows(pl.program_id(0), nb_ctx, xc_ref, xl_ref)
    h = _adaln(x, sh_ref[...], sc_ref[...]).astype(bf16)
    n_q = q_ref.shape[1]
    for j in range(w_ref.shape[1] // tn):
        cols = slice(j * tn, (j + 1) * tn)
        part = _dot(h, w_ref[:, cols])
        if j * tn < n_q:
            q_ref[:, cols] = part.astype(q_ref.dtype)
        else:
            kv_ref[:, j * tn - n_q:(j + 1) * tn - n_q] = part


def _ln_matmul(xs, mod, w, lay, tm=512, tn=512):
    d = xs[0].shape[1]
    n_out = w.shape[1]
    assert d % tn == 0
    if len(xs) == 1:
        x_specs = [pl.BlockSpec((tm, d), lambda i: (i, 0))]
        nb_ctx = None
    else:
        x_specs = list(_split_row_specs((tm, d), tm, lay))
        nb_ctx = lay.n_ctx // tm
    return pl.pallas_call(
        functools.partial(_ln_mm_kernel, tn=tn, nb_ctx=nb_ctx),
        grid=(lay.n // tm,),
        in_specs=x_specs + [_mod_spec(0, tm, lay), _mod_spec(1, tm, lay), _weight_spec(w)],
        out_specs=[pl.BlockSpec((tm, d), lambda i: (i, 0)), pl.BlockSpec((tm, n_out - d), lambda i: (i, 0))],
        out_shape=[jax.ShapeDtypeStruct((lay.n, d), bf16), jax.ShapeDtypeStruct((lay.n, n_out - d), f32)],
        compiler_params=_params(("parallel",)),
        name="adaln_proj",
    )(*xs, mod, mod, _weight_arg(w))


def _mlp_body(x, sh_ref, sc_ref, g_ref, w1_ref, w2_ref, acc_ref, o_ref, fc):
    h = _adaln(x, sh_ref[...], sc_ref[...]).astype(bf16)
    d_ff = w1_ref.shape[1]
    for c in range(d_ff // fc):
        a = _dot(h, w1_ref[:, c * fc:(c + 1) * fc])
        a = jnp.square(jnp.maximum(a, 0.0)).astype(bf16)
        part = _dot(a, w2_ref[c * fc:(c + 1) * fc, :])
        if c == 0:
            acc_ref[...] = part
        else:
            acc_ref[...] += part
    o_ref[...] = x + g_ref[...] * acc_ref[...]


def _mlp_kernel(x_ref, sh_ref, sc_ref, g_ref, w1_ref, w2_ref, o_ref, acc_ref, *, fc):
    _mlp_body(x_ref[...], sh_ref, sc_ref, g_ref, w1_ref, w2_ref, acc_ref, o_ref, fc)


def _mlp(x, mod, w1, w2, lay, tm=512, fc=1024):
    n, d = x.shape
    return pl.pallas_call(
        functools.partial(_mlp_kernel, fc=fc),
        grid=(n // tm,),
        in_specs=[
            pl.BlockSpec((tm, d), lambda i: (i, 0)),
            _mod_spec(3, tm, lay), _mod_spec(4, tm, lay), _mod_spec(5, tm, lay),
            _weight_spec(w1), _weight_spec(w2),
        ],
        out_specs=pl.BlockSpec((tm, d), lambda i: (i, 0)),
        out_shape=jax.ShapeDtypeStruct((n, d), f32),
        scratch_shapes=[pltpu.VMEM((tm, d), f32)],
        compiler_params=_params(("parallel",)),
        name="mlp",
    )(x, mod, mod, mod, _weight_arg(w1), _weight_arg(w2))


def _out_mlp_kernel(*refs, fc, nb_ctx, split_x, split_a):
    refs = list(refs)
    i = pl.program_id(0)
    x = _pick_rows(i, nb_ctx, refs.pop(0), refs.pop(0)) if split_x else refs.pop(0)[...]
    a = _pick_rows(i, nb_ctx, refs.pop(0), refs.pop(0)) if split_a else refs.pop(0)[...]
    gm_ref, wo_ref, sh_ref, sc_ref, gf_ref, w1_ref, w2_ref, o_ref, acc_ref = refs
    x1 = x + gm_ref[...] * _dot(a.astype(bf16), wo_ref[...])
    _mlp_body(x1, sh_ref, sc_ref, gf_ref, w1_ref, w2_ref, acc_ref, o_ref, fc)


def _out_proj_mlp(xs, a_parts, mod, wo, w1, w2, lay, rows=None, tm=512, fc=1024):
    d = D_MODEL
    k_in = a_parts[0].shape[1]
    tile0, n_tiles = rows if rows is not None else (0, lay.n // tm)
    split_x, split_a = len(xs) == 2, len(a_parts) == 2
    assert rows is None or not (split_x or split_a)
    x_specs = (list(_split_row_specs((tm, d), tm, lay)) if split_x
               else [pl.BlockSpec((tm, d), lambda i: (i + tile0, 0))])
    a_specs = (list(_split_row_specs((tm, k_in), tm, lay)) if split_a
               else [pl.BlockSpec((tm, k_in), lambda i: (i, 0))])
    return pl.pallas_call(
        functools.partial(_out_mlp_kernel, fc=fc, nb_ctx=lay.n_ctx // tm, split_x=split_x, split_a=split_a),
        grid=(n_tiles,),
        in_specs=x_specs + a_specs + [
            _mod_spec(2, tm, lay, tile0), _weight_spec(wo),
            _mod_spec(3, tm, lay, tile0), _mod_spec(4, tm, lay, tile0), _mod_spec(5, tm, lay, tile0),
            _weight_spec(w1), _weight_spec(w2),
        ],
        out_specs=pl.BlockSpec((tm, d), lambda i: (i, 0)),
        out_shape=jax.ShapeDtypeStruct((n_tiles * tm, d), f32),
        scratch_shapes=[pltpu.VMEM((tm, d), f32)],
        compiler_params=_params(("parallel",)),
        name="out_proj_mlp",
    )(*xs, *a_parts, mod, _weight_arg(wo), mod, mod, mod, _weight_arg(w1), _weight_arg(w2))


def _da_lambda(lam_ref, lam_init):
    lp = lam_ref[...]
    e1 = jnp.exp(jnp.sum(lp[0:1] * lp[1:2], axis=-1, keepdims=True))
    e2 = jnp.exp(jnp.sum(lp[2:3] * lp[3:4], axis=-1, keepdims=True))
    return e1 - e2 + lam_init


def _da_qk(q, k_bf16):
    lane = lax.broadcasted_iota(jnp.int32, q.shape, 1)
    return [_dot_nt(jnp.where((lane < HEAD) == (m == 0), q, 0.0).astype(bf16), k_bf16) for m in range(2)]


def _with_ones(v_bf16):
    return jnp.concatenate([v_bf16, jnp.ones_like(v_bf16)], axis=1)


def _da_pv(scores, v_ones, lam):
    outs = []
    for s in scores:
        e = jnp.exp2(s - jnp.max(s, axis=-1, keepdims=True))
        pv = _dot(e.astype(bf16), v_ones)
        outs.append(pv[:, :LANES] * (1.0 / pv[:, LANES:]))
    return outs[0] - lam * outs[1]


def _da_finish(o, sub_ref, lam_init):
    ms = jnp.mean(o * o, axis=-1, keepdims=True)
    return o * lax.rsqrt(ms + NORM_EPS) * sub_ref[...] * (1.0 - lam_init)


def _da_ctx_kernel(*refs, lam_init, layer):
    lam_ref, q_ref, k_ref, v_ref, qn_ref, kn_ref, sub_ref = refs[:7]
    o_ref, knew_ref, vnew_ref = refs[-3:]
    lam = _da_lambda(lam_ref, lam_init)
    q = _head_rms(q_ref[...].astype(f32), qn_ref[...]) * QK_SCALE
    k = _head_rms(k_ref[...], kn_ref[...])
    v = v_ref[...]
    if layer == 0:
        for l in range(knew_ref.shape[0]):
            knew_ref[l] = k if l == 0 else jnp.zeros_like(k)
            vnew_ref[l] = v if l == 0 else jnp.zeros_like(v)
    else:
        knew_ref[...] = k
        vnew_ref[...] = v
    kb = k.astype(bf16)
    vb = v.astype(bf16)
    cols = [slice(h * LANES, (h + 1) * LANES) for h in range(q.shape[1] // LANES)]
    s_next = _da_qk(q[:, cols[0]], kb[:, cols[0]])
    for h, cs in enumerate(cols):
        s_cur = s_next
        if h + 1 < len(cols):
            s_next = _da_qk(q[:, cols[h + 1]], kb[:, cols[h + 1]])
        o = _da_finish(_da_pv(s_cur, _with_ones(vb[:, cs]), lam), sub_ref, lam_init)
        o_ref[:, cs] = o.astype(o_ref.dtype)


def _da_lat_kernel(lam_ref, q_ref, k_ref, v_ref, ck_ref, cv_ref, qn_ref, kn_ref, sub_ref,
                   cq_ref, sq_ref, ckk_ref, skk_ref, o_ref, kf_scr, vf_scr, *, lam_init, past):
    cols = [slice(h * LANES, (h + 1) * LANES) for h in range(q_ref.shape[1] // LANES)]

    @pl.when(pl.program_id(2) == 0)
    def _():
        k = _rope(_head_rms(k_ref[...], kn_ref[...]), ckk_ref[...], skk_ref[...])
        kf_scr[0:past, :] = ck_ref[...].astype(bf16)
        kf_scr[past:, :] = k.astype(bf16)
        for h, cs in enumerate(cols):
            vf_scr[h, 0:past, :] = _with_ones(cv_ref[:, cs].astype(bf16))
            vf_scr[h, past:, :] = _with_ones(v_ref[:, cs].astype(bf16))

    lam = _da_lambda(lam_ref, lam_init)
    q = _rope(_head_rms(q_ref[...].astype(f32), qn_ref[...]), cq_ref[...], sq_ref[...]) * QK_SCALE
    s_next = _da_qk(q[:, cols[0]], kf_scr[:, cols[0]])
    for h, cs in enumerate(cols):
        s_cur = s_next
        if h + 1 < len(cols):
            s_next = _da_qk(q[:, cols[h + 1]], kf_scr[:, cols[h + 1]])
        o_ref[:, cs] = _da_finish(_da_pv(s_cur, vf_scr[h], lam), sub_ref, lam_init).astype(o_ref.dtype)


def _da_mixer(q, kv, cache_k, cache_v, j, n_layers, kv_prev, p, lam_init, rope_tab, lay, tq=256):
    d = D_MODEL
    nh = DA_HEADS
    assert (kv_prev is None) == (j == 0)
    qn = jnp.tile(p['q_norm'], 2).reshape(1, LANES)
    kn = jnp.tile(p['k_norm'], 2).reshape(1, LANES)
    qn_all = jnp.tile(qn, (1, nh))
    kn_all = jnp.tile(kn, (1, nh))
    sub = p['subln'].reshape(1, LANES)
    small = lambda shape: pl.BlockSpec(shape, lambda *_: (0,) * len(shape))
    t = lay.t_ctx
    ctx_rows = pl.BlockSpec((t, d), lambda b: (b, 0))
    kv_shape = jax.ShapeDtypeStruct((lay.b_ctx, n_layers, t, d), f32)
    if kv_prev is None:
        kv_spec = pl.BlockSpec((None, n_layers, t, d), lambda b: (b, 0, 0, 0))
        kv_in_specs, kv_args, aliases = [], (), {}
    else:
        kv_spec = pl.BlockSpec((None, None, t, d), lambda b: (b, j, 0, 0))
        kv_in_specs = [pl.BlockSpec(memory_space=pl.ANY)] * 2
        kv_args, aliases = tuple(kv_prev), {7: 1, 8: 2}
    o_ctx, k_new, v_new = pl.pallas_call(
        functools.partial(_da_ctx_kernel, lam_init=lam_init, layer=j),
        grid=(lay.b_ctx,),
        in_specs=[
            small((4, HEAD)),
            ctx_rows,
            ctx_rows,
            pl.BlockSpec((t, d), lambda b: (b, 1)),
            small((1, d)), small((1, d)), small((1, LANES)),
        ] + kv_in_specs,
        out_specs=[ctx_rows, kv_spec, kv_spec],
        out_shape=[jax.ShapeDtypeStruct((lay.n_ctx, d), bf16), kv_shape, kv_shape],
        input_output_aliases=aliases,
        compiler_params=_params(("parallel",)),
        name="da_ctx",
    )(p['lam'], q, kv, kv, qn_all, kn_all, sub, *kv_args)

    tl = lay.t_lat
    past = cache_k.shape[2]
    row0 = lay.n_ctx // tl
    q0 = lay.n_ctx // tq
    nq = tl // tq
    cos, sin = rope_tab
    ck = cache_k.reshape(lay.b_lat, cache_k.shape[1], past, d)
    cv = cache_v.reshape(lay.b_lat, cache_v.shape[1], past, d)
    hg = 2
    gw = hg * LANES
    ng = nh // hg
    cos_g, sin_g = jnp.tile(cos, (1, hg)), jnp.tile(sin, (1, hg))
    o_lat = pl.pallas_call(
        functools.partial(_da_lat_kernel, lam_init=lam_init, past=past),
        grid=(lay.b_lat, ng, nq),
        in_specs=[
            small((4, HEAD)),
            pl.BlockSpec((tq, gw), lambda b, h, i: (q0 + b * nq + i, h)),
            pl.BlockSpec((tl, gw), lambda b, h, i: (row0 + b, h)),
            pl.BlockSpec((tl, gw), lambda b, h, i: (row0 + b, ng + h)),
            pl.BlockSpec((None, None, past, gw), lambda b, h, i: (b, j, 0, h)),
            pl.BlockSpec((None, None, past, gw), lambda b, h, i: (b, j, 0, h)),
            small((1, gw)), small((1, gw)), small((1, LANES)),
            pl.BlockSpec((tq, gw), lambda b, h, i: (i, 0)),
            pl.BlockSpec((tq, gw), lambda b, h, i: (i, 0)),
            _resident((tl, gw)), _resident((tl, gw)),
        ],
        out_specs=pl.BlockSpec((tq, gw), lambda b, h, i: (b * nq + i, h)),
        out_shape=jax.ShapeDtypeStruct((lay.n_lat, d), bf16),
        scratch_shapes=[pltpu.VMEM((past + tl, gw), bf16), pltpu.VMEM((hg, past + tl, 2 * LANES), bf16)],
        compiler_params=_params(("parallel", "parallel", "arbitrary")),
        name="da_lat",
    )(p['lam'], q, kv, kv, ck, cv, qn_all[:, :gw], kn_all[:, :gw], sub, cos_g, sin_g, cos_g, sin_g)
    return o_ctx, o_lat, k_new, v_new


def _swa_head_mask(shape, j):
    lane = lax.broadcasted_iota(jnp.int32, shape, 1)
    return (lane >= j * HEAD) & (lane < (j + 1) * HEAD)


def _swa_heads(q, sink_ref, score_fn, pv_fn, o_ref):
    heads = [(g, j) for g in range(SWA_GROUP) for j in range(SWA_KV_HEADS)]

    def scores(g, j):
        qg = q[:, g * SWA_KV:(g + 1) * SWA_KV]
        return score_fn(jnp.where(_swa_head_mask(qg.shape, j), qg, 0.0).astype(bf16))

    s_next = scores(*heads[0])
    o = None
    for idx, (g, j) in enumerate(heads):
        s_cur = s_next
        if idx + 1 < len(heads):
            s_next = scores(*heads[idx + 1])
        sink = sink_ref[j * SWA_GROUP + g] * LOG2E
        m = sink
        for s in s_cur:
            m = jnp.maximum(jnp.max(s, axis=-1, keepdims=True), m)
        exps = [jnp.exp2(s - m) for s in s_cur]
        den = jnp.exp2(sink - m)
        for e in exps:
            den = den + jnp.sum(e, axis=-1, keepdims=True)
        oj = pv_fn(j, [e.astype(bf16) for e in exps]) * (1.0 / den)
        o = oj if j == 0 else o + oj
        if j == SWA_KV_HEADS - 1:
            o_ref[:, g * SWA_KV:(g + 1) * SWA_KV] = o.astype(o_ref.dtype)


def _swa_ctx_kernel(sink_ref, q_ref, k_ref, v_ref, qn_ref, kn_ref, o_ref, knew_ref, vnew_ref):
    q = _head_rms(q_ref[...].astype(f32), qn_ref[...]) * QK_SCALE
    k = _head_rms(k_ref[...], kn_ref[...])
    v = v_ref[...]
    knew_ref[...] = k
    vnew_ref[...] = v
    kb = k.astype(bf16)
    v_heads = [jnp.where(_swa_head_mask(v.shape, j), v, 0.0).astype(bf16) for j in range(SWA_KV_HEADS)]
    _swa_heads(q, sink_ref, lambda qm: [_dot_nt(qm, kb)], lambda j, e: _dot(e[0], v_heads[j]), o_ref)


def _swa_lat_kernel(sink_ref, q_ref, k_ref, v_ref, ck_ref, cv_ref, qn_ref, kn_ref,
                    cq_ref, sq_ref, ckk_ref, skk_ref, o_ref, kp_scr, vp_scr, *, t_lat):
    n = pl.program_id(1)
    w = WINDOW

    @pl.when(n == 0)
    def _():
        zeros = jnp.zeros((w, SWA_KV), bf16)
        k = _rope(_head_rms(k_ref[...], kn_ref[...]), ckk_ref[...], skk_ref[...])
        kp_scr[0:w, :] = zeros
        kp_scr[w:w + t_lat, :] = k.astype(bf16)
        kp_scr[w + t_lat:, :] = zeros
        vp_scr[0:w, :] = zeros
        vp_scr[w:w + t_lat, :] = v_ref[...].astype(bf16)
        vp_scr[w + t_lat:, :] = zeros

    q = _rope(_head_rms(q_ref[...].astype(f32), qn_ref[...]), cq_ref[...], sq_ref[...]) * QK_SCALE
    start = pl.multiple_of(n * w, w)
    kband = kp_scr[pl.ds(start, 3 * w), :]
    vband = vp_scr[pl.ds(start, 3 * w), :]
    ckb = ck_ref[...].astype(bf16)
    cv = cv_ref[...]
    qi = lax.broadcasted_iota(jnp.int32, (w, 3 * w), 0)
    si = lax.broadcasted_iota(jnp.int32, (w, 3 * w), 1)
    rel = si - qi
    key_pos = (n - 1) * w + si
    valid = (rel >= 0) & (rel <= 2 * w) & (key_pos >= 0) & (key_pos < t_lat)
    cv_heads = [jnp.where(_swa_head_mask(cv.shape, j), cv, 0.0).astype(bf16) for j in range(SWA_KV_HEADS)]
    vb_heads = [jnp.where(_swa_head_mask(vband.shape, j), vband, jnp.zeros_like(vband))
                for j in range(SWA_KV_HEADS)]
    _swa_heads(q, sink_ref,
               lambda qm: [_dot_nt(qm, ckb), jnp.where(valid, _dot_nt(qm, kband), NEG_INF)],
               lambda j, e: _dot(e[0], cv_heads[j]) + _dot(e[1], vb_heads[j]),
               o_ref)


def _swa_group_major_weights(wqkv, wo):
    d = D_MODEL
    wq = wqkv[:, :d].reshape(d, SWA_KV_HEADS, SWA_GROUP, HEAD).transpose(0, 2, 1, 3).reshape(d, d)
    wo = wo.reshape(SWA_KV_HEADS, SWA_GROUP, HEAD, d).transpose(1, 0, 2, 3).reshape(d, d)
    return jnp.concatenate([wq, wqkv[:, d:]], axis=1).astype(bf16), wo.astype(bf16)


def _swa_mixer(q, kv, cache_k, cache_v, j, p, rope_tab, lay):
    d = D_MODEL
    qn = jnp.tile(p['q_norm'], d // HEAD).reshape(1, d)
    kn = jnp.tile(p['k_norm'], SWA_KV_HEADS).reshape(1, SWA_KV)
    small = lambda shape: pl.BlockSpec(shape, lambda *_: (0,) * len(shape))
    smem = pl.BlockSpec(memory_space=pltpu.SMEM)
    t = lay.t_ctx
    kv_rows = pl.BlockSpec((t, SWA_KV), lambda b: (b, 0))
    o_ctx, k_new, v_new = pl.pallas_call(
        _swa_ctx_kernel,
        grid=(lay.b_ctx,),
        in_specs=[
            smem,
            pl.BlockSpec((t, d), lambda b: (b, 0)),
            kv_rows,
            pl.BlockSpec((t, SWA_KV), lambda b: (b, 1)),
            small((1, d)), small((1, SWA_KV)),
        ],
        out_specs=[pl.BlockSpec((t, d), lambda b: (b, 0)), kv_rows, kv_rows],
        out_shape=[jax.ShapeDtypeStruct((lay.n_ctx, d), bf16),
                   jax.ShapeDtypeStruct((lay.n_ctx, SWA_KV), f32),
                   jax.ShapeDtypeStruct((lay.n_ctx, SWA_KV), f32)],
        compiler_params=_params(("parallel",)),
        name="swa_ctx",
    )(p['sink'], q, kv, kv, qn, kn)

    tl = lay.t_lat
    w = WINDOW
    past = cache_k.shape[2]
    row0 = lay.n_ctx // tl
    q0 = lay.n_ctx // w
    nq = tl // w
    cos, sin = rope_tab
    ck = cache_k.reshape(lay.b_lat, cache_k.shape[1], past, SWA_KV)
    cv = cache_v.reshape(lay.b_lat, cache_v.shape[1], past, SWA_KV)
    o_lat = pl.pallas_call(
        functools.partial(_swa_lat_kernel, t_lat=tl),
        grid=(lay.b_lat, nq),
        in_specs=[
            smem,
            pl.BlockSpec((w, d), lambda b, i: (q0 + b * nq + i, 0)),
            pl.BlockSpec((tl, SWA_KV), lambda b, i: (row0 + b, 0)),
            pl.BlockSpec((tl, SWA_KV), lambda b, i: (row0 + b, 1)),
            pl.BlockSpec((None, None, past, SWA_KV), lambda b, i: (b, j, 0, 0)),
            pl.BlockSpec((None, None, past, SWA_KV), lambda b, i: (b, j, 0, 0)),
            small((1, d)), small((1, SWA_KV)),
            pl.BlockSpec((w, d), lambda b, i: (i, 0)),
            pl.BlockSpec((w, d), lambda b, i: (i, 0)),
            pl.BlockSpec((tl, SWA_KV), lambda b, i: (0, 0)),
            pl.BlockSpec((tl, SWA_KV), lambda b, i: (0, 0)),
        ],
        out_specs=pl.BlockSpec((w, d), lambda b, i: (b * nq + i, 0)),
        out_shape=jax.ShapeDtypeStruct((lay.n_lat, d), bf16),
        scratch_shapes=[pltpu.VMEM((tl + 2 * w, SWA_KV), bf16), pltpu.VMEM((tl + 2 * w, SWA_KV), bf16)],
        compiler_params=_params(("parallel", "arbitrary")),
        name="swa_lat",
    )(p['sink'], q, kv, kv, ck, cv, qn, kn, cos, sin, cos, sin)
    return o_ctx, o_lat, k_new, v_new


def _rwkv_proj_kernel(x_ref, xp_ref, xn_ref, sh_ref, sc_ref, mu_ref, wrkv_ref, w1_ref, w2_ref, w0_ref,
                      a1_ref, a2_ref, a0_ref, g1_ref, g2_ref,
                      r_ref, k_ref, v_ref, g_ref, a_ref, lw_ref, *, lay, tm):
    i = pl.program_id(0)
    r0 = i * tm
    in_ctx = r0 < lay.n_ctx
    pos = jnp.where(in_ctx, r0 % lay.t_ctx, (r0 - lay.n_ctx) % lay.t_lat)
    seq_len = jnp.where(in_ctx, lay.t_ctx, lay.t_lat)
    keep_prev = jnp.where(pos == 0, 0.0, 1.0)
    keep_next = jnp.where(pos + tm == seq_len, 0.0, 1.0)

    sh = sh_ref[...]
    sc = sc_ref[...]
    h = _adaln(x_ref[...], sh, sc)
    h_prev_row = _adaln(xp_ref[7:8, :], sh, sc) * keep_prev
    h_next_row = _adaln(xn_ref[0:1, :], sh, sc) * keep_next
    row = lax.broadcasted_iota(jnp.int32, h.shape, 0)
    prev = jnp.where(row == 0, h_prev_row, pltpu.roll(h, 1, 0))
    nxt = jnp.where(row == tm - 1, h_next_row, pltpu.roll(h, tm - 1, 0))
    xx = 0.5 * (prev + nxt) - h

    def mix(m):
        return (h + xx * mu_ref[m:m + 1, :]).astype(bf16)

    r_ref[...] = _dot(mix(0), wrkv_ref[0]).astype(r_ref.dtype)
    k_ref[...] = _dot(mix(1), wrkv_ref[1]).astype(k_ref.dtype)
    v_ref[...] = _dot(mix(2), wrkv_ref[2]).astype(v_ref.dtype)

    lane = lax.broadcasted_iota(jnp.int32, (tm, LANES), 1)
    first = lane < HEAD

    tw = jnp.tanh(_dot(mix(3), w1_ref[...]))
    la = _dot(mix(4), a1_ref[...])
    for z in range(2):
        sel = first if z == 0 else jnp.logical_not(first)
        w_lora = _dot(jnp.where(sel, tw, 0.0).astype(bf16), w2_ref[...])
        w_log = -_softplus(-(w0_ref[z:z + 1, :] + w_lora)) - 0.5
        lw_ref[z] = -jnp.exp(w_log)
        a_lora = _dot(jnp.where(sel, la, 0.0).astype(bf16), a2_ref[...])
        a_ref[z] = jax.nn.sigmoid(a0_ref[z:z + 1, :] + a_lora).astype(a_ref.dtype)

    g = _dot(jax.nn.sigmoid(_dot(mix(5), g1_ref[...])).astype(bf16), g2_ref[...])
    g_ref[...] = g.astype(g_ref.dtype)


def _rwkv_proj(x, mod, p, lay, tm=256):
    n, d = x.shape
    assert lay.t_ctx % tm == 0 and lay.t_lat % tm == 0
    nb8 = n // 8
    wrkv = p['wrkv'].astype(bf16)
    w1 = jnp.concatenate([p['w1'][0], p['w1'][1]], axis=1).astype(bf16)
    w2 = p['w2'].reshape(2 * p['w2'].shape[1], d).astype(bf16)
    a1 = jnp.concatenate([p['a1'][0], p['a1'][1]], axis=1).astype(bf16)
    a2 = p['a2'].reshape(2 * p['a2'].shape[1], d).astype(bf16)
    g1 = p['g1'].astype(bf16)
    g2 = p['g2'].astype(bf16)
    row_spec = pl.BlockSpec((tm, d), lambda i: (i, 0))
    z_spec = pl.BlockSpec((2, tm, d), lambda i: (0, i, 0))
    args = (x, x, x, mod, mod, p['mu'], wrkv, w1, w2, p['w0'], a1, a2, p['a0'], g1, g2)
    return pl.pallas_call(
        functools.partial(_rwkv_proj_kernel, lay=lay, tm=tm),
        grid=(n // tm,),
        in_specs=[
            row_spec,
            pl.BlockSpec((8, d), lambda i: (jnp.maximum(i * (tm // 8) - 1, 0), 0)),
            pl.BlockSpec((8, d), lambda i: (jnp.minimum((i + 1) * (tm // 8), nb8 - 1), 0)),
            _mod_spec(0, tm, lay),
            _mod_spec(1, tm, lay),
        ] + [_resident(a.shape) for a in args[5:]],
        out_specs=[row_spec] * 4 + [z_spec] * 2,
        out_shape=[jax.ShapeDtypeStruct((n, d), bf16)] * 4
        + [jax.ShapeDtypeStruct((2, n, d), bf16), jax.ShapeDtypeStruct((2, n, d), f32)],
        compiler_params=_params(("parallel",)),
        name="rwkv_proj",
    )(*args)


def _wkv_prep(r, k, v, a, lw, kk_gain, ka, forward):
    c_len = r.shape[0]
    r, k, v, a = (z.astype(f32) for z in (r, k, v, a))
    kkr = k * kk_gain
    kk = kkr * lax.rsqrt(jnp.maximum(_head_sum(kkr * kkr), 1e-24))
    beta = kk * a
    kd = k * (1.0 + (a - 1.0) * ka)
    row = lax.broadcasted_iota(jnp.int32, lw.shape, 0)
    cum = lw
    step = 1
    while step < c_len:
        if forward:
            cum = cum + jnp.where(row >= step, pltpu.roll(cum, step, 0), 0.0)
        else:
            cum = cum + jnp.where(row < c_len - step, pltpu.roll(cum, c_len - step, 0), 0.0)
        step *= 2
    total = cum[c_len - 1:c_len, :] if forward else cum[0:1, :]
    inv = jnp.exp(-cum)
    rem = jnp.exp(total - cum)
    return dict(a_t=-kk * jnp.exp(cum - lw), r_t=r * jnp.exp(cum), b_inv=beta * inv, k_inv=kd * inv,
                b_rem=beta * rem, k_rem=kd * rem, p_total=jnp.exp(total), v=v)


def _wkv_kernel(*refs, has_init):
    fwd_refs, bwd_refs = refs[0:5], refs[5:10]
    kk_ref, ka_ref = refs[10:12]
    n_in = 13 if has_init else 12
    yf_ref, yb_ref, sout_ref, s_scr = refs[n_in:]
    c_len = WKV_CHUNK
    gw = WKV_LANES
    hpg = gw // HEAD
    n_grp = yf_ref.shape[1] // gw
    c = pl.program_id(1)

    @pl.when(c == 0)
    def _():
        s_scr[...] = jnp.zeros_like(s_scr)
        if has_init:
            s0_ref = refs[12]
            for dr in range(2):
                for h in range(n_grp * hpg):
                    o = (h % hpg) * HEAD
                    s_scr[dr, h // hpg, o:o + HEAD, o:o + HEAD] = s0_ref[dr, h]

    t_s = lax.broadcasted_iota(jnp.int32, (c_len, gw), 0)
    i_s = lax.broadcasted_iota(jnp.int32, (c_len, gw), 1) & (c_len - 1)
    strict = (i_s < t_s, i_s > t_s)
    incl = (i_s <= t_s, i_s >= t_s)
    eye = jnp.where(t_s == i_s, 1.0, 0.0)
    pair_block = (t_s >> 1) == (i_s >> 1)
    levels = []
    for lb in range(1, int(math.log2(c_len))):
        levels.append(((t_s >> (lb + 1)) == (i_s >> (lb + 1))) & ((t_s >> lb) != (i_s >> lb)))
    rb = lax.broadcasted_iota(jnp.int32, (hpg * c_len, gw), 0) // c_len
    cb = lax.broadcasted_iota(jnp.int32, (hpg * c_len, gw), 1) // HEAD
    same_head = rb == cb

    def bd(x):
        return jnp.where(same_head, jnp.concatenate([x] * hpg, axis=0), 0.0).astype(bf16)

    ops = [_wkv_prep(*(ref[...] for ref in d_refs), kk_ref[...], ka_ref[...], forward=(dr == 0))
           for dr, d_refs in enumerate((fwd_refs, bwd_refs))]

    chains = [(dr, pr, slice(pr * gw, (pr + 1) * gw)) for dr in range(2) for pr in range(n_grp)]
    ids = range(len(chains))
    lhs = [jnp.concatenate([ops[dr]['a_t'][:, cs], ops[dr]['r_t'][:, cs]], axis=0).astype(bf16)
           for dr, pr, cs in chains]
    s0 = [s_scr[dr, pr] for dr, pr, cs in chains]
    sc = [_dot_nt(lhs[i], jnp.concatenate([bd(ops[dr]['b_inv'][:, cs]), bd(ops[dr]['k_inv'][:, cs]),
                                           s0[i].astype(bf16)], axis=0))
          for i, (dr, pr, cs) in enumerate(chains)]
    sc_b = [s[:, :gw] for s in sc]
    sc_k = [s[:, gw:2 * gw] for s in sc]
    x0 = [s[:, 2 * gw:] for s in sc]
    n_mat = [jnp.where(strict[dr], sc_b[i][:c_len], 0.0) for i, (dr, pr, cs) in enumerate(chains)]
    v_bd = [bd(ops[dr]['v'][:, cs]) for dr, pr, cs in chains]
    x = [x0[i][:c_len] + _dot(jnp.where(strict[dr], sc_k[i][:c_len], 0.0).astype(bf16), v_bd[i])
         for i, (dr, pr, cs) in enumerate(chains)]
    t_inv = [eye + jnp.where(pair_block, n_mat[i], 0.0) for i in ids]
    for off_diag in levels:
        half = [_dot(t_inv[i].astype(bf16), bd(jnp.where(off_diag, n_mat[i], 0.0))) for i in ids]
        t_inv = [t_inv[i] + _dot(half[i].astype(bf16), bd(t_inv[i])) for i in ids]
    uv_bd = [jnp.concatenate([bd(_dot(t_inv[i].astype(bf16), bd(x[i]))), v_bd[i]], axis=0) for i in ids]
    for i, (dr, pr, cs) in enumerate(chains):
        rbk = jnp.concatenate([jnp.where(incl[dr], sc_b[i][c_len:], 0.0),
                               jnp.where(incl[dr], sc_k[i][c_len:], 0.0)], axis=1).astype(bf16)
        y_ref = yf_ref if dr == 0 else yb_ref
        y_ref[:, cs] = (x0[i][c_len:] + _dot(rbk, uv_bd[i])).astype(y_ref.dtype)
    for i, (dr, pr, cs) in enumerate(chains):
        bk_rem = jnp.concatenate([bd(ops[dr]['b_rem'][:, cs]), bd(ops[dr]['k_rem'][:, cs])], axis=0)
        s_scr[dr, pr] = s0[i] * ops[dr]['p_total'][:, cs] + _dot_tn(uv_bd[i], bk_rem)

    @pl.when(c == pl.num_programs(1) - 1)
    def _():
        for dr in range(2):
            for h in range(n_grp * hpg):
                o = (h % hpg) * HEAD
                sout_ref[dr, h] = s_scr[dr, h // hpg, o:o + HEAD, o:o + HEAD]


def _wkv_scan(r, k, v, a2, lw2, p, s0, row0, n_seq, t_len):
    d = D_MODEL
    c_len = WKV_CHUNK
    nc = t_len // c_len
    b0 = row0 // c_len
    n_grp = d // WKV_LANES
    has_init = s0 is not None

    fwd_blk = lambda s, c: s * nc + c
    bwd_blk = lambda s, c: s * nc + (nc - 1 - c)
    tok_f = pl.BlockSpec((c_len, d), lambda s, c: (b0 + fwd_blk(s, c), 0))
    tok_b = pl.BlockSpec((c_len, d), lambda s, c: (b0 + bwd_blk(s, c), 0))
    z_f = pl.BlockSpec((None, c_len, d), lambda s, c: (0, b0 + fwd_blk(s, c), 0))
    z_b = pl.BlockSpec((None, c_len, d), lambda s, c: (1, b0 + bwd_blk(s, c), 0))
    vec = pl.BlockSpec((1, d), lambda s, c: (0, 0))
    st = pl.BlockSpec((None, 2, RWKV_HEADS, HEAD, HEAD), lambda s, c: (s, 0, 0, 0, 0))
    in_specs = [tok_f, tok_f, tok_f, z_f, z_f, tok_b, tok_b, tok_b, z_b, z_b, vec, vec]
    args = [r, k, v, a2, lw2, r, k, v, a2, lw2, p['k_k'].reshape(1, d), p['k_a'].reshape(1, d)]
    if has_init:
        in_specs.append(st)
        args.append(s0)
    return pl.pallas_call(
        functools.partial(_wkv_kernel, has_init=has_init),
        grid=(n_seq, nc),
        in_specs=in_specs,
        out_specs=[
            pl.BlockSpec((c_len, d), lambda s, c: (fwd_blk(s, c), 0)),
            pl.BlockSpec((c_len, d), lambda s, c: (bwd_blk(s, c), 0)),
            st,
        ],
        out_shape=[jax.ShapeDtypeStruct((n_seq * t_len, d), bf16)] * 2
        + [jax.ShapeDtypeStruct((n_seq, 2, RWKV_HEADS, HEAD, HEAD), f32)],
        scratch_shapes=[pltpu.VMEM((2, n_grp, WKV_LANES, WKV_LANES), f32)],
        compiler_params=_params(("parallel", "arbitrary")),
        name="wkv_scan",
    )(*args)


def _rwkv_post_kernel(x_ref, yfc_ref, ybc_ref, yfl_ref, ybl_ref, r_ref, k_ref, v_ref, g_ref, a_ref,
                      ka_ref, rk_ref, lnw_ref, lnb_ref, gate_ref, wo_ref, o_ref, *, nb_ctx):
    i = pl.program_id(0)
    y = (_pick_rows(i, nb_ctx, yfc_ref, yfl_ref).astype(f32)
         + _pick_rows(i, nb_ctx, ybc_ref, ybl_ref).astype(f32))
    mu = _head_sum(y) * (1.0 / HEAD)
    yc = y - mu
    var = _head_sum(yc * yc) * (1.0 / HEAD)
    yn = yc * lax.rsqrt(var + RWKV_GN_EPS) * lnw_ref[...] + lnb_ref[...]
    k = k_ref[...].astype(f32)
    ka = ka_ref[...]
    a0 = a_ref[0].astype(f32)
    a1 = a_ref[1].astype(f32)
    kd_sum = k * (1.0 + (a0 - 1.0) * ka) + k * (1.0 + (a1 - 1.0) * ka)
    bonus = _head_sum(r_ref[...].astype(f32) * kd_sum * rk_ref[...]) * v_ref[...].astype(f32)
    o = ((yn + bonus) * g_ref[...].astype(f32)).astype(bf16)
    o_ref[...] = x_ref[...] + gate_ref[...] * _dot(o, wo_ref[...])


def _rwkv_post_residual(x, mod, y_ctx, y_lat, r, k, v, g, a2, p, wo_bf16, lay, tm=256):
    n, d = r.shape
    tok = pl.BlockSpec((tm, d), lambda i: (i, 0))
    tok_z = pl.BlockSpec((2, tm, d), lambda i: (0, i, 0))
    vec = pl.BlockSpec((1, d), lambda i: (0, 0))
    yc_spec, yl_spec = _split_row_specs((tm, d), tm, lay)
    row = lambda a: a.reshape(1, d)
    return pl.pallas_call(
        functools.partial(_rwkv_post_kernel, nb_ctx=lay.n_ctx // tm),
        grid=(n // tm,),
        in_specs=[tok, yc_spec, yc_spec, yl_spec, yl_spec, tok, tok, tok, tok, tok_z, vec, vec, vec, vec,
                  _mod_spec(2, tm, lay), _resident((d, d))],
        out_specs=tok,
        out_shape=jax.ShapeDtypeStruct((n, d), f32),
        compiler_params=_params(("parallel",)),
        name="rwkv_post",
    )(x, y_ctx[0], y_ctx[1], y_lat[0], y_lat[1], r, k, v, g, a2,
      row(p['k_a']), row(p['r_k']), row(p['ln_w']), row(p['ln_b']), mod, wo_bf16)


def _rwkv_layer(x, mod, state0, p, wo_bf16, lay):
    r, k, v, g, a2, lw2 = _rwkv_proj(x, mod, p, lay)
    yf_c, yb_c, s_ctx = _wkv_scan(r, k, v, a2, lw2, p, None, 0, lay.b_ctx, lay.t_ctx)
    yf_l, yb_l, _ = _wkv_scan(r, k, v, a2, lw2, p, state0, lay.n_ctx, lay.b_lat, lay.t_lat)
    x = _rwkv_post_residual(x, mod, (yf_c, yb_c), (yf_l, yb_l), r, k, v, g, a2, p, wo_bf16, lay)
    return x, s_ctx


def _rope_table(n_tok, width):
    n_rows = n_tok // GRID_W
    rows = jnp.broadcast_to(jnp.arange(n_rows, dtype=f32)[:, None], (n_rows, GRID_W)).reshape(-1)
    cols = jnp.broadcast_to(jnp.arange(GRID_W, dtype=f32)[None, :], (n_rows, GRID_W)).reshape(-1)
    n_freq = HEAD // 4
    inv = ROPE_BASE ** (-jnp.arange(n_freq, dtype=f32) / n_freq)
    ang = jnp.concatenate([rows[:, None] * inv, cols[:, None] * inv], axis=-1)
    reps = width // (HEAD // 2)
    return jnp.tile(jnp.cos(ang), (1, reps)), jnp.tile(jnp.sin(ang), (1, reps))


def kernel(x_prompt, x_sample, cache_da_k, cache_da_v, state_rwkv, cache_swa_k, cache_swa_v, c, c_ctx, ada_w, ada_b, mlp_w1, mlp_w2, da_wqkv, da_q_norm, da_k_norm, da_lambda, da_subln, da_wo, rwkv_mu, rwkv_wrkv, rwkv_w0, rwkv_w1, rwkv_w2, rwkv_a0, rwkv_a1, rwkv_a2, rwkv_g1, rwkv_g2, rwkv_k_k, rwkv_k_a, rwkv_r_k, rwkv_ln_w, rwkv_ln_b, rwkv_wo, swa_wqkv, swa_q_norm, swa_k_norm, swa_sink, swa_wo):
    b_ctx, t_ctx, d = x_prompt.shape
    b_lat, t_lat, _ = x_sample.shape
    lay = Layout(b_ctx, t_ctx, b_lat, t_lat)
    depth = ada_w.shape[0]
    tm = 512
    assert d == D_MODEL and lay.n_ctx % t_lat == 0

    xs = (x_prompt.reshape(lay.n_ctx, d), x_sample.reshape(lay.n_lat, d))
    cond = jnp.concatenate([c_ctx[None, :], c, jnp.zeros((8 - 1 - b_lat, d), f32)], axis=0)
    mods = _modulation(cond, ada_w, ada_b)
    mods = mods[:, :1 + b_lat].reshape(depth, 1 + b_lat, 6, 1, d)

    rope_da = _rope_table(t_lat, LANES)
    rope_swa = _rope_table(t_lat, d)

    mlp_w1_b, mlp_w2_b = mlp_w1.astype(bf16), mlp_w2.astype(bf16)
    da_wqkv_b, da_wo_b = da_wqkv.astype(bf16), da_wo.astype(bf16)
    n_da = da_wqkv.shape[0]
    da_kv = None
    rwkv_s, swa_k, swa_v = [], [], []
    for i in range(depth):
        kind = i % N_MIXERS
        j = i // N_MIXERS
        mod = mods[i]
        w1 = LayerWeight(mlp_w1_b, i)
        w2 = LayerWeight(mlp_w2_b, i)
        if kind == 1:
            p = {'mu': rwkv_mu[j], 'wrkv': rwkv_wrkv[j], 'w0': rwkv_w0[j], 'w1': rwkv_w1[j],
                 'w2': rwkv_w2[j], 'a0': rwkv_a0[j], 'a1': rwkv_a1[j], 'a2': rwkv_a2[j],
                 'g1': rwkv_g1[j], 'g2': rwkv_g2[j], 'k_k': rwkv_k_k[j], 'k_a': rwkv_k_a[j],
                 'r_k': rwkv_r_k[j], 'ln_w': rwkv_ln_w[j], 'ln_b': rwkv_ln_b[j]}
            x = xs[0] if len(xs) == 1 else jnp.concatenate(xs, axis=0)
            x, s_new = _rwkv_layer(x, mod, state_rwkv[:, j], p, rwkv_wo[j].astype(bf16), lay)
            rwkv_s.append(s_new)
            xs = (_mlp(x, mod, w1, w2, lay),)
            continue
        if kind == 0:
            p = {'q_norm': da_q_norm[j], 'k_norm': da_k_norm[j], 'lam': da_lambda[j], 'subln': da_subln[j]}
            lam_init = 0.8 - 0.6 * math.exp(-0.3 * i)
            q, kv = _ln_matmul(xs, mod, LayerWeight(da_wqkv_b, j), lay)
            o_ctx, o_lat, *da_kv = _da_mixer(q, kv, cache_da_k, cache_da_v, j, n_da, da_kv, p, lam_init,
                                             rope_da, lay)
            wo = LayerWeight(da_wo_b, j)
        else:
            p = {'q_norm': swa_q_norm[j], 'k_norm': swa_k_norm[j], 'sink': swa_sink[j]}
            wqkv, wo = _swa_group_major_weights(swa_wqkv[j], swa_wo[j])
            q, kv = _ln_matmul(xs, mod, wqkv, lay)
            o_ctx, o_lat, k_new, v_new = _swa_mixer(q, kv, cache_swa_k, cache_swa_v, j, p, rope_swa, lay)
            swa_k.append(k_new.reshape(b_ctx, t_ctx, SWA_KV_HEADS, HEAD))
            swa_v.append(v_new.reshape(b_ctx, t_ctx, SWA_KV_HEADS, HEAD))
        if i == depth - 1 and len(xs) == 1:
            nb_ctx = lay.n_ctx // tm
            xs = (_out_proj_mlp(xs, (o_ctx,), mod, wo, w1, w2, lay, rows=(0, nb_ctx), tm=tm),
                  _out_proj_mlp(xs, (o_lat,), mod, wo, w1, w2, lay, rows=(nb_ctx, lay.n_lat // tm), tm=tm))
        else:
            xs = (_out_proj_mlp(xs, (o_ctx, o_lat), mod, wo, w1, w2, lay, tm=tm),)

    if len(xs) == 1:
        xs = (xs[0][:lay.n_ctx], xs[0][lay.n_ctx:])
    y_prompt = xs[0].reshape(b_ctx, t_ctx, d)
    y_sample = xs[1].reshape(b_lat, t_lat, d)
    new_da_k, new_da_v = (a.reshape(b_ctx, n_da, t_ctx, DA_HEADS, LANES) for a in da_kv)
    return (y_prompt, y_sample, new_da_k, new_da_v,
            jnp.stack(rwkv_s, axis=1), jnp.stack(swa_k, axis=1), jnp.stack(swa_v, axis=1))
```

```python
import functools
import math
from typing import NamedTuple

import jax
import jax.numpy as jnp
from jax import lax
from jax.experimental import pallas as pl
from jax.experimental.pallas import tpu as pltpu

f32 = jnp.float32
bf16 = jnp.bfloat16

D_MODEL = 1024
N_MIXERS = 3
GRID_W = 64
ROPE_BASE = 10000.0
NORM_EPS = 1e-6
NEG_INF = -1e30
LOG2E = math.log2(math.e)
HEAD = 64
LANES = 128
DA_HEADS = D_MODEL // LANES
RWKV_HEADS = D_MODEL // HEAD
RWKV_GN_EPS = 64e-5
SWA_HEADS = D_MODEL // HEAD
SWA_KV_HEADS = SWA_HEADS // 4
SWA_GROUP = SWA_HEADS // SWA_KV_HEADS
SWA_KV = SWA_KV_HEADS * HEAD
WINDOW = 128
WKV_CHUNK = 64
WKV_LANES = 128
VMEM_LIMIT = 56 * 1024 * 1024
QK_SCALE = HEAD ** -0.5 * LOG2E


class Layout(NamedTuple):
    b_ctx: int
    t_ctx: int
    b_lat: int
    t_lat: int

    @property
    def n_ctx(self):
        return self.b_ctx * self.t_ctx

    @property
    def n_lat(self):
        return self.b_lat * self.t_lat

    @property
    def n(self):
        return self.n_ctx + self.n_lat


def _params(sem):
    return pltpu.CompilerParams(dimension_semantics=sem, vmem_limit_bytes=VMEM_LIMIT)


def _group_of_tile(i, tm, lay):
    r = i * tm
    return jnp.where(r < lay.n_ctx, 0, 1 + (r - lay.n_ctx) // lay.t_lat)


def _mod_spec(chunk, tm, lay, tile0=0):
    return pl.BlockSpec((None, None, 1, D_MODEL),
                        lambda i, *_: (_group_of_tile(i + tile0, tm, lay), chunk, 0, 0))


def _resident(shape):
    return pl.BlockSpec(shape, lambda *_: (0,) * len(shape), pipeline_mode=pl.Buffered(1))


class LayerWeight(NamedTuple):
    stack: jax.Array
    layer: int

    @property
    def shape(self):
        return self.stack.shape[1:]

    def spec(self):
        idx = (self.layer,) + (0,) * (self.stack.ndim - 1)
        return pl.BlockSpec((None,) + self.shape, lambda *_: idx, pipeline_mode=pl.Buffered(1))


def _weight_spec(w):
    return w.spec() if isinstance(w, LayerWeight) else _resident(w.shape)


def _weight_arg(w):
    return w.stack if isinstance(w, LayerWeight) else w


def _split_row_specs(block, tm, lay):
    nb_ctx = lay.n_ctx // tm
    lead = (0,) * (len(block) - 2)
    ctx = pl.BlockSpec(block, lambda i, *_: lead + (jnp.minimum(i, nb_ctx - 1), 0))
    lat = pl.BlockSpec(block, lambda i, *_: lead + (jnp.maximum(i - nb_ctx, 0), 0))
    return ctx, lat


def _dot(a, b):
    return jnp.dot(a, b, preferred_element_type=f32)


def _dot_nt(a, b):
    return lax.dot_general(a, b, (((1,), (1,)), ((), ())), preferred_element_type=f32)


def _dot_tn(a, b):
    return lax.dot_general(a, b, (((0,), (0,)), ((), ())), preferred_element_type=f32)


def _split2(x):
    hi = x.astype(bf16)
    return hi, (x - hi.astype(f32)).astype(bf16)


def _head_ones():
    r = lax.broadcasted_iota(jnp.int32, (LANES, LANES), 0)
    c = lax.broadcasted_iota(jnp.int32, (LANES, LANES), 1)
    return ((r // HEAD) == (c // HEAD)).astype(bf16)


def _head_sum(x):
    g = _head_ones()
    hi, lo = _split2(x)
    cols = []
    for p in range(x.shape[1] // LANES):
        cs = slice(p * LANES, (p + 1) * LANES)
        cols.append(_dot(hi[:, cs], g) + _dot(lo[:, cs], g))
    return cols[0] if len(cols) == 1 else jnp.concatenate(cols, axis=1)


def _head_rms(x, gain):
    ms = _head_sum(x * x) * (1.0 / HEAD)
    return x * lax.rsqrt(ms + NORM_EPS) * gain


def _adaln(x, shift, scale):
    ms = jnp.mean(x * x, axis=-1, keepdims=True)
    return x * lax.rsqrt(ms + NORM_EPS) * (1.0 + scale) + shift


def _rope(x, cos, sin):
    w = x.shape[1]
    lane = lax.broadcasted_iota(jnp.int32, x.shape, 1)
    low = (lane & (HEAD - 1)) < HEAD // 2
    up = pltpu.roll(x, w - HEAD // 2, 1)
    dn = pltpu.roll(x, HEAD // 2, 1)
    return x * cos + jnp.where(low, -up, dn) * sin


def _pick_rows(i, nb_ctx, ctx_ref, lat_ref):
    return jnp.where(i < nb_ctx, ctx_ref[...], lat_ref[...])


def _mod_kernel(c_ref, w_ref, b_ref, o_ref):
    c = c_ref[...]
    s = c * jax.nn.sigmoid(c)
    sh, sl = _split2(s)
    wh, wl = _split2(w_ref[...])
    o_ref[...] = _dot(sh, wh) + (_dot(sh, wl) + _dot(sl, wh)) + b_ref[...]


def _modulation(cond8, ada_w, ada_b):
    depth, d, d6 = ada_w.shape
    tn = 1536
    return pl.pallas_call(
        _mod_kernel,
        grid=(depth, d6 // tn),
        in_specs=[
            pl.BlockSpec((8, d), lambda l, j: (0, 0)),
            pl.BlockSpec((None, d, tn), lambda l, j: (l, 0, j)),
            pl.BlockSpec((None, 1, tn), lambda l, j: (l, 0, j)),
        ],
        out_specs=pl.BlockSpec((None, 8, tn), lambda l, j: (l, 0, j)),
        out_shape=jax.ShapeDtypeStruct((depth, 8, d6), f32),
        compiler_params=_params(("parallel", "parallel")),
        name="modulation",
    )(cond8, ada_w, ada_b.reshape(depth, 1, d6))


def _ln_mm_kernel(*refs, tn, nb_ctx):
    if nb_ctx is None:
        x_ref, sh_ref, sc_ref, w_ref, q_ref, kv_ref = refs
        x = x_ref[...]
    else:
        xc_ref, xl_ref, sh_ref, sc_ref, w_ref, q_ref, kv_ref = refs
        x = _pick_rows(pl.program_id(0), nb_ctx, xc_ref, xl_ref)
    h = _adaln(x, sh_ref[...], sc_ref[...]).astype(bf16)
    n_q = q_ref.shape[1]
    for j in range(w_ref.shape[1] // tn):
        cols = slice(j * tn, (j + 1) * tn)
        part = _dot(h, w_ref[:, cols])
        if j * tn < n_q:
            q_ref[:, cols] = part.astype(q_ref.dtype)
        else:
            kv_ref[:, j * tn - n_q:(j + 1) * tn - n_q] = part


def _ln_matmul(xs, mod, w, lay, tm=512, tn=512):
    d = xs[0].shape[1]
    n_out = w.shape[1]
    assert d % tn == 0
    if len(xs) == 1:
        x_specs = [pl.BlockSpec((tm, d), lambda i: (i, 0))]
        nb_ctx = None
    else:
        x_specs = list(_split_row_specs((tm, d), tm, lay))
        nb_ctx = lay.n_ctx // tm
    return pl.pallas_call(
        functools.partial(_ln_mm_kernel, tn=tn, nb_ctx=nb_ctx),
        grid=(lay.n // tm,),
        in_specs=x_specs + [_mod_spec(0, tm, lay), _mod_spec(1, tm, lay), _weight_spec(w)],
        out_specs=[pl.BlockSpec((tm, d), lambda i: (i, 0)), pl.BlockSpec((tm, n_out - d), lambda i: (i, 0))],
        out_shape=[jax.ShapeDtypeStruct((lay.n, d), bf16), jax.ShapeDtypeStruct((lay.n, n_out - d), f32)],
        compiler_params=_params(("parallel",)),
        name="adaln_proj",
    )(*xs, mod, mod, _weight_arg(w))


def _mlp_body(x, sh_ref, sc_ref, g_ref, w1_ref, w2_ref, acc_ref, o_ref, fc):
    h = _adaln(x, sh_ref[...], sc_ref[...]).astype(bf16)
    d_ff = w1_ref.shape[1]
    for c in range(d_ff // fc):
        a = _dot(h, w1_ref[:, c * fc:(c + 1) * fc])
        a = jnp.square(jnp.maximum(a, 0.0)).astype(bf16)
        part = _dot(a, w2_ref[c * fc:(c + 1) * fc, :])
        if c == 0:
            acc_ref[...] = part
        else:
            acc_ref[...] += part
    o_ref[...] = x + g_ref[...] * acc_ref[...]


def _mlp_kernel(x_ref, sh_ref, sc_ref, g_ref, w1_ref, w2_ref, o_ref, acc_ref, *, fc):
    _mlp_body(x_ref[...], sh_ref, sc_ref, g_ref, w1_ref, w2_ref, acc_ref, o_ref, fc)


def _mlp(x, mod, w1, w2, lay, tm=512, fc=1024):
    n, d = x.shape
    return pl.pallas_call(
        functools.partial(_mlp_kernel, fc=fc),
        grid=(n // tm,),
        in_specs=[
            pl.BlockSpec((tm, d), lambda i: (i, 0)),
            _mod_spec(3, tm, lay), _mod_spec(4, tm, lay), _mod_spec(5, tm, lay),
            _weight_spec(w1), _weight_spec(w2),
        ],
        out_specs=pl.BlockSpec((tm, d), lambda i: (i, 0)),
        out_shape=jax.ShapeDtypeStruct((n, d), f32),
        scratch_shapes=[pltpu.VMEM((tm, d), f32)],
        compiler_params=_params(("parallel",)),
        name="mlp",
    )(x, mod, mod, mod, _weight_arg(w1), _weight_arg(w2))


def _out_mlp_kernel(*refs, fc, nb_ctx, split_x, split_a):
    refs = list(refs)
    i = pl.program_id(0)
    x = _pick_rows(i, nb_ctx, refs.pop(0), refs.pop(0)) if split_x else refs.pop(0)[...]
    a = _pick_rows(i, nb_ctx, refs.pop(0), refs.pop(0)) if split_a else refs.pop(0)[...]
    gm_ref, wo_ref, sh_ref, sc_ref, gf_ref, w1_ref, w2_ref, o_ref, acc_ref = refs
    x1 = x + gm_ref[...] * _dot(a.astype(bf16), wo_ref[...])
    _mlp_body(x1, sh_ref, sc_ref, gf_ref, w1_ref, w2_ref, acc_ref, o_ref, fc)


def _out_proj_mlp(xs, a_parts, mod, wo, w1, w2, lay, rows=None, tm=512, fc=1024):
    d = D_MODEL
    k_in = a_parts[0].shape[1]
    tile0, n_tiles = rows if rows is not None else (0, lay.n // tm)
    split_x, split_a = len(xs) == 2, len(a_parts) == 2
    assert rows is None or not (split_x or split_a)
    x_specs = (list(_split_row_specs((tm, d), tm, lay)) if split_x
               else [pl.BlockSpec((tm, d), lambda i: (i + tile0, 0))])
    a_specs = (list(_split_row_specs((tm, k_in), tm, lay)) if split_a
               else [pl.BlockSpec((tm, k_in), lambda i: (i, 0))])
    return pl.pallas_call(
        functools.partial(_out_mlp_kernel, fc=fc, nb_ctx=lay.n_ctx // tm, split_x=split_x, split_a=split_a),
        grid=(n_tiles,),
        in_specs=x_specs + a_specs + [
            _mod_spec(2, tm, lay, tile0), _weight_spec(wo),
            _mod_spec(3, tm, lay, tile0), _mod_spec(4, tm, lay, tile0), _mod_spec(5, tm, lay, tile0),
            _weight_spec(w1), _weight_spec(w2),
        ],
        out_specs=pl.BlockSpec((tm, d), lambda i: (i, 0)),
        out_shape=jax.ShapeDtypeStruct((n_tiles * tm, d), f32),
        scratch_shapes=[pltpu.VMEM((tm, d), f32)],
        compiler_params=_params(("parallel",)),
        name="out_proj_mlp",
    )(*xs, *a_parts, mod, _weight_arg(wo), mod, mod, mod, _weight_arg(w1), _weight_arg(w2))


def _da_lambda(lam_ref, lam_init):
    lp = lam_ref[...]
    e1 = jnp.exp(jnp.sum(lp[0:1] * lp[1:2], axis=-1, keepdims=True))
    e2 = jnp.exp(jnp.sum(lp[2:3] * lp[3:4], axis=-1, keepdims=True))
    return e1 - e2 + lam_init


def _da_qk(q, k_bf16):
    lane = lax.broadcasted_iota(jnp.int32, q.shape, 1)
    maps = [jnp.where((lane < HEAD) == (m == 0), q, 0.0).astype(bf16) for m in range(2)]
    return _dot_nt(jnp.concatenate(maps, axis=0), k_bf16)


def _with_ones(v_bf16):
    return jnp.concatenate([v_bf16, jnp.ones_like(v_bf16)], axis=1)


def _da_pv(scores, v_ones, lam):
    t = scores.shape[0] // 2
    e = jnp.exp2(scores - jnp.max(scores, axis=-1, keepdims=True))
    pv = _dot(e.astype(bf16), v_ones)
    o = pv[:, :LANES] * (1.0 / pv[:, LANES:])
    return o[:t] - lam * o[t:]


def _da_finish(o, sub_ref, lam_init):
    ms = jnp.mean(o * o, axis=-1, keepdims=True)
    return o * lax.rsqrt(ms + NORM_EPS) * sub_ref[...] * (1.0 - lam_init)


def _da_ctx_kernel(*refs, lam_init, layer):
    lam_ref, q_ref, k_ref, v_ref, qn_ref, kn_ref, sub_ref = refs[:7]
    o_ref, knew_ref, vnew_ref = refs[-3:]
    lam = _da_lambda(lam_ref, lam_init)
    q = _head_rms(q_ref[...].astype(f32), qn_ref[...]) * QK_SCALE
    k = _head_rms(k_ref[...], kn_ref[...])
    v = v_ref[...]
    if layer == 0:
        for l in range(knew_ref.shape[0]):
            knew_ref[l] = k if l == 0 else jnp.zeros_like(k)
            vnew_ref[l] = v if l == 0 else jnp.zeros_like(v)
    else:
        knew_ref[...] = k
        vnew_ref[...] = v
    kb = k.astype(bf16)
    vb = v.astype(bf16)
    cols = [slice(h * LANES, (h + 1) * LANES) for h in range(q.shape[1] // LANES)]
    s_next = _da_qk(q[:, cols[0]], kb[:, cols[0]])
    for h, cs in enumerate(cols):
        s_cur = s_next
        if h + 1 < len(cols):
            s_next = _da_qk(q[:, cols[h + 1]], kb[:, cols[h + 1]])
        o = _da_finish(_da_pv(s_cur, _with_ones(vb[:, cs]), lam), sub_ref, lam_init)
        o_ref[:, cs] = o.astype(o_ref.dtype)


def _da_lat_kernel(lam_ref, q_ref, k_ref, v_ref, ck_ref, cv_ref, qn_ref, kn_ref, sub_ref,
                   cq_ref, sq_ref, ckk_ref, skk_ref, o_ref, kf_scr, vf_scr, *, lam_init, past):
    cols = [slice(h * LANES, (h + 1) * LANES) for h in range(q_ref.shape[1] // LANES)]

    @pl.when(pl.program_id(2) == 0)
    def _():
        k = _rope(_head_rms(k_ref[...], kn_ref[...]), ckk_ref[...], skk_ref[...])
        kf_scr[0:past, :] = ck_ref[...].astype(bf16)
        kf_scr[past:, :] = k.astype(bf16)
        for h, cs in enumerate(cols):
            vf_scr[h, 0:past, :] = _with_ones(cv_ref[:, cs].astype(bf16))
            vf_scr[h, past:, :] = _with_ones(v_ref[:, cs].astype(bf16))

    lam = _da_lambda(lam_ref, lam_init)
    q = _rope(_head_rms(q_ref[...].astype(f32), qn_ref[...]), cq_ref[...], sq_ref[...]) * QK_SCALE
    s_next = _da_qk(q[:, cols[0]], kf_scr[:, cols[0]])
    for h, cs in enumerate(cols):
        s_cur = s_next
        if h + 1 < len(cols):
            s_next = _da_qk(q[:, cols[h + 1]], kf_scr[:, cols[h + 1]])
        o_ref[:, cs] = _da_finish(_da_pv(s_cur, vf_scr[h], lam), sub_ref, lam_init).astype(o_ref.dtype)


def _da_mixer(q, kv, cache_k, cache_v, j, n_layers, kv_prev, p, lam_init, rope_tab, lay, tq=256):
    d = D_MODEL
    nh = DA_HEADS
    assert (kv_prev is None) == (j == 0)
    qn = jnp.tile(p['q_norm'], 2).reshape(1, LANES)
    kn = jnp.tile(p['k_norm'], 2).reshape(1, LANES)
    qn_all = jnp.tile(qn, (1, nh))
    kn_all = jnp.tile(kn, (1, nh))
    sub = p['subln'].reshape(1, LANES)
    small = lambda shape: pl.BlockSpec(shape, lambda *_: (0,) * len(shape))
    t = lay.t_ctx
    ctx_rows = pl.BlockSpec((t, d), lambda b: (b, 0))
    kv_shape = jax.ShapeDtypeStruct((lay.b_ctx, n_layers, t, d), f32)
    if kv_prev is None:
        kv_spec = pl.BlockSpec((None, n_layers, t, d), lambda b: (b, 0, 0, 0))
        kv_in_specs, kv_args, aliases = [], (), {}
    else:
        kv_spec = pl.BlockSpec((None, None, t, d), lambda b: (b, j, 0, 0))
        kv_in_specs = [pl.BlockSpec(memory_space=pl.ANY)] * 2
        kv_args, aliases = tuple(kv_prev), {7: 1, 8: 2}
    o_ctx, k_new, v_new = pl.pallas_call(
        functools.partial(_da_ctx_kernel, lam_init=lam_init, layer=j),
        grid=(lay.b_ctx,),
        in_specs=[
            small((4, HEAD)),
            ctx_rows,
            ctx_rows,
            pl.BlockSpec((t, d), lambda b: (b, 1)),
            small((1, d)), small((1, d)), small((1, LANES)),
        ] + kv_in_specs,
        out_specs=[ctx_rows, kv_spec, kv_spec],
        out_shape=[jax.ShapeDtypeStruct((lay.n_ctx, d), bf16), kv_shape, kv_shape],
        input_output_aliases=aliases,
        compiler_params=_params(("parallel",)),
        name="da_ctx",
    )(p['lam'], q, kv, kv, qn_all, kn_all, sub, *kv_args)

    tl = lay.t_lat
    past = cache_k.shape[2]
    row0 = lay.n_ctx // tl
    q0 = lay.n_ctx // tq
    nq = tl // tq
    cos, sin = rope_tab
    ck = cache_k.reshape(lay.b_lat, cache_k.shape[1], past, d)
    cv = cache_v.reshape(lay.b_lat, cache_v.shape[1], past, d)
    hg = 2
    gw = hg * LANES
    ng = nh // hg
    cos_g, sin_g = jnp.tile(cos, (1, hg)), jnp.tile(sin, (1, hg))
    o_lat = pl.pallas_call(
        functools.partial(_da_lat_kernel, lam_init=lam_init, past=past),
        grid=(lay.b_lat, ng, nq),
        in_specs=[
            small((4, HEAD)),
            pl.BlockSpec((tq, gw), lambda b, h, i: (q0 + b * nq + i, h)),
            pl.BlockSpec((tl, gw), lambda b, h, i: (row0 + b, h)),
            pl.BlockSpec((tl, gw), lambda b, h, i: (row0 + b, ng + h)),
            pl.BlockSpec((None, None, past, gw), lambda b, h, i: (b, j, 0, h)),
            pl.BlockSpec((None, None, past, gw), lambda b, h, i: (b, j, 0, h)),
            small((1, gw)), small((1, gw)), small((1, LANES)),
            pl.BlockSpec((tq, gw), lambda b, h, i: (i, 0)),
            pl.BlockSpec((tq, gw), lambda b, h, i: (i, 0)),
            _resident((tl, gw)), _resident((tl, gw)),
        ],
        out_specs=pl.BlockSpec((tq, gw), lambda b, h, i: (b * nq + i, h)),
        out_shape=jax.ShapeDtypeStruct((lay.n_lat, d), bf16),
        scratch_shapes=[pltpu.VMEM((past + tl, gw), bf16), pltpu.VMEM((hg, past + tl, 2 * LANES), bf16)],
        compiler_params=_params(("parallel", "parallel", "arbitrary")),
        name="da_lat",
    )(p['lam'], q, kv, kv, ck, cv, qn_all[:, :gw], kn_all[:, :gw], sub, cos_g, sin_g, cos_g, sin_g)
    return o_ctx, o_lat, k_new, v_new


def _swa_head_mask(shape, j):
    lane = lax.broadcasted_iota(jnp.int32, shape, 1)
    return (lane >= j * HEAD) & (lane < (j + 1) * HEAD)


def _swa_heads(q, sink_ref, score_fn, mask_fn, pv_fn, o_ref):
    t = q.shape[0]

    def scores(g):
        qg = q[:, g * SWA_KV:(g + 1) * SWA_KV]
        rows = [jnp.where(_swa_head_mask(qg.shape, j), qg, 0.0).astype(bf16) for j in range(SWA_KV_HEADS)]
        return score_fn(jnp.concatenate(rows, axis=0))

    s_next = scores(0)
    for g in range(SWA_GROUP):
        s_grp = s_next
        if g + 1 < SWA_GROUP:
            s_next = scores(g + 1)
        o = None
        for j in range(SWA_KV_HEADS):
            s_cur = mask_fn([s[j * t:(j + 1) * t] for s in s_grp])
            sink = sink_ref[j * SWA_GROUP + g] * LOG2E
            m = sink
            for s in s_cur:
                m = jnp.maximum(jnp.max(s, axis=-1, keepdims=True), m)
            exps = [jnp.exp2(s - m) for s in s_cur]
            den = jnp.exp2(sink - m)
            for e in exps:
                den = den + jnp.sum(e, axis=-1, keepdims=True)
            oj = pv_fn(j, [e.astype(bf16) for e in exps]) * (1.0 / den)
            o = oj if j == 0 else o + oj
        o_ref[:, g * SWA_KV:(g + 1) * SWA_KV] = o.astype(o_ref.dtype)


def _swa_ctx_kernel(sink_ref, q_ref, k_ref, v_ref, qn_ref, kn_ref, o_ref, knew_ref, vnew_ref):
    q = _head_rms(q_ref[...].astype(f32), qn_ref[...]) * QK_SCALE
    k = _head_rms(k_ref[...], kn_ref[...])
    v = v_ref[...]
    knew_ref[...] = k
    vnew_ref[...] = v
    kb = k.astype(bf16)
    v_heads = [jnp.where(_swa_head_mask(v.shape, j), v, 0.0).astype(bf16) for j in range(SWA_KV_HEADS)]
    _swa_heads(q, sink_ref, lambda qm: [_dot_nt(qm, kb)], lambda s: s,
               lambda j, e: _dot(e[0], v_heads[j]), o_ref)


def _swa_lat_kernel(sink_ref, q_ref, k_ref, v_ref, ck_ref, cv_ref, qn_ref, kn_ref,
                    cq_ref, sq_ref, ckk_ref, skk_ref, o_ref, kp_scr, vp_scr, *, t_lat):
    n = pl.program_id(1)
    w = WINDOW

    @pl.when(n == 0)
    def _():
        zeros = jnp.zeros((w, SWA_KV), bf16)
        k = _rope(_head_rms(k_ref[...], kn_ref[...]), ckk_ref[...], skk_ref[...])
        kp_scr[0:w, :] = zeros
        kp_scr[w:w + t_lat, :] = k.astype(bf16)
        kp_scr[w + t_lat:, :] = zeros
        vp_scr[0:w, :] = zeros
        vp_scr[w:w + t_lat, :] = v_ref[...].astype(bf16)
        vp_scr[w + t_lat:, :] = zeros

    q = _rope(_head_rms(q_ref[...].astype(f32), qn_ref[...]), cq_ref[...], sq_ref[...]) * QK_SCALE
    start = pl.multiple_of(n * w, w)
    kband = kp_scr[pl.ds(start, 3 * w), :]
    vband = vp_scr[pl.ds(start, 3 * w), :]
    ckb = ck_ref[...].astype(bf16)
    cv = cv_ref[...]
    qi = lax.broadcasted_iota(jnp.int32, (w, 3 * w), 0)
    si = lax.broadcasted_iota(jnp.int32, (w, 3 * w), 1)
    rel = si - qi
    key_pos = (n - 1) * w + si
    valid = (rel >= 0) & (rel <= 2 * w) & (key_pos >= 0) & (key_pos < t_lat)
    cv_heads = [jnp.where(_swa_head_mask(cv.shape, j), cv, 0.0).astype(bf16) for j in range(SWA_KV_HEADS)]
    vb_heads = [jnp.where(_swa_head_mask(vband.shape, j), vband, jnp.zeros_like(vband))
                for j in range(SWA_KV_HEADS)]
    _swa_heads(q, sink_ref,
               lambda qm: [_dot_nt(qm, ckb), _dot_nt(qm, kband)],
               lambda s: [s[0], jnp.where(valid, s[1], NEG_INF)],
               lambda j, e: _dot(e[0], cv_heads[j]) + _dot(e[1], vb_heads[j]),
               o_ref)


def _swa_group_major_weights(wqkv, wo):
    d = D_MODEL
    wq = wqkv[:, :d].reshape(d, SWA_KV_HEADS, SWA_GROUP, HEAD).transpose(0, 2, 1, 3).reshape(d, d)
    wo = wo.reshape(SWA_KV_HEADS, SWA_GROUP, HEAD, d).transpose(1, 0, 2, 3).reshape(d, d)
    return jnp.concatenate([wq, wqkv[:, d:]], axis=1).astype(bf16), wo.astype(bf16)


def _swa_mixer(q, kv, cache_k, cache_v, j, p, rope_tab, lay):
    d = D_MODEL
    qn = jnp.tile(p['q_norm'], d // HEAD).reshape(1, d)
    kn = jnp.tile(p['k_norm'], SWA_KV_HEADS).reshape(1, SWA_KV)
    small = lambda shape: pl.BlockSpec(shape, lambda *_: (0,) * len(shape))
    smem = pl.BlockSpec(memory_space=pltpu.SMEM)
    t = lay.t_ctx
    kv_rows = pl.BlockSpec((t, SWA_KV), lambda b: (b, 0))
    o_ctx, k_new, v_new = pl.pallas_call(
        _swa_ctx_kernel,
        grid=(lay.b_ctx,),
        in_specs=[
            smem,
            pl.BlockSpec((t, d), lambda b: (b, 0)),
            kv_rows,
            pl.BlockSpec((t, SWA_KV), lambda b: (b, 1)),
            small((1, d)), small((1, SWA_KV)),
        ],
        out_specs=[pl.BlockSpec((t, d), lambda b: (b, 0)), kv_rows, kv_rows],
        out_shape=[jax.ShapeDtypeStruct((lay.n_ctx, d), bf16),
                   jax.ShapeDtypeStruct((lay.n_ctx, SWA_KV), f32),
                   jax.ShapeDtypeStruct((lay.n_ctx, SWA_KV), f32)],
        compiler_params=_params(("parallel",)),
        name="swa_ctx",
    )(p['sink'], q, kv, kv, qn, kn)

    tl = lay.t_lat
    w = WINDOW
    past = cache_k.shape[2]
    row0 = lay.n_ctx // tl
    q0 = lay.n_ctx // w
    nq = tl // w
    cos, sin = rope_tab
    ck = cache_k.reshape(lay.b_lat, cache_k.shape[1], past, SWA_KV)
    cv = cache_v.reshape(lay.b_lat, cache_v.shape[1], past, SWA_KV)
    o_lat = pl.pallas_call(
        functools.partial(_swa_lat_kernel, t_lat=tl),
        grid=(lay.b_lat, nq),
        in_specs=[
            smem,
            pl.BlockSpec((w, d), lambda b, i: (q0 + b * nq + i, 0)),
            pl.BlockSpec((tl, SWA_KV), lambda b, i: (row0 + b, 0)),
            pl.BlockSpec((tl, SWA_KV), lambda b, i: (row0 + b, 1)),
            pl.BlockSpec((None, None, past, SWA_KV), lambda b, i: (b, j, 0, 0)),
            pl.BlockSpec((None, None, past, SWA_KV), lambda b, i: (b, j, 0, 0)),
            small((1, d)), small((1, SWA_KV)),
            pl.BlockSpec((w, d), lambda b, i: (i, 0)),
            pl.BlockSpec((w, d), lambda b, i: (i, 0)),
            pl.BlockSpec((tl, SWA_KV), lambda b, i: (0, 0)),
            pl.BlockSpec((tl, SWA_KV), lambda b, i: (0, 0)),
        ],
        out_specs=pl.BlockSpec((w, d), lambda b, i: (b * nq + i, 0)),
        out_shape=jax.ShapeDtypeStruct((lay.n_lat, d), bf16),
        scratch_shapes=[pltpu.VMEM((tl + 2 * w, SWA_KV), bf16), pltpu.VMEM((tl + 2 * w, SWA_KV), bf16)],
        compiler_params=_params(("parallel", "arbitrary")),
        name="swa_lat",
    )(p['sink'], q, kv, kv, ck, cv, qn, kn, cos, sin, cos, sin)
    return o_ctx, o_lat, k_new, v_new


def _rwkv_proj_kernel(x_ref, xp_ref, xn_ref, sh_ref, sc_ref, mu_ref, wrkv_ref, w1_ref, w2_ref, w0_ref,
                      a1_ref, a2_ref, a0_ref, g1_ref, g2_ref,
                      r_ref, k_ref, v_ref, g_ref, a_ref, lw_ref, *, lay, tm):
    i = pl.program_id(0)
    r0 = i * tm
    in_ctx = r0 < lay.n_ctx
    pos = jnp.where(in_ctx, r0 % lay.t_ctx, (r0 - lay.n_ctx) % lay.t_lat)
    seq_len = jnp.where(in_ctx, lay.t_ctx, lay.t_lat)
    keep_prev = jnp.where(pos == 0, 0.0, 1.0)
    keep_next = jnp.where(pos + tm == seq_len, 0.0, 1.0)

    sh = sh_ref[...]
    sc = sc_ref[...]
    h = _adaln(x_ref[...], sh, sc)
    h_prev_row = _adaln(xp_ref[7:8, :], sh, sc) * keep_prev
    h_next_row = _adaln(xn_ref[0:1, :], sh, sc) * keep_next
    row = lax.broadcasted_iota(jnp.int32, h.shape, 0)
    prev = jnp.where(row == 0, h_prev_row, pltpu.roll(h, 1, 0))
    nxt = jnp.where(row == tm - 1, h_next_row, pltpu.roll(h, tm - 1, 0))
    xx = 0.5 * (prev + nxt) - h

    def mix(m):
        return (h + xx * mu_ref[m:m + 1, :]).astype(bf16)

    r_ref[...] = _dot(mix(0), wrkv_ref[0]).astype(r_ref.dtype)
    k_ref[...] = _dot(mix(1), wrkv_ref[1]).astype(k_ref.dtype)
    v_ref[...] = _dot(mix(2), wrkv_ref[2]).astype(v_ref.dtype)

    lane = lax.broadcasted_iota(jnp.int32, (tm, LANES), 1)
    first = lane < HEAD

    tw = jnp.tanh(_dot(mix(3), w1_ref[...]))
    la = _dot(mix(4), a1_ref[...])
    for z in range(2):
        sel = first if z == 0 else jnp.logical_not(first)
        w_lora = _dot(jnp.where(sel, tw, 0.0).astype(bf16), w2_ref[...])
        lw_ref[z] = -math.exp(-0.5) * jax.nn.sigmoid(w0_ref[z:z + 1, :] + w_lora)
        a_lora = _dot(jnp.where(sel, la, 0.0).astype(bf16), a2_ref[...])
        a_ref[z] = jax.nn.sigmoid(a0_ref[z:z + 1, :] + a_lora).astype(a_ref.dtype)

    g = _dot(jax.nn.sigmoid(_dot(mix(5), g1_ref[...])).astype(bf16), g2_ref[...])
    g_ref[...] = g.astype(g_ref.dtype)


def _rwkv_proj(x, mod, p, lay, tm=256):
    n, d = x.shape
    assert lay.t_ctx % tm == 0 and lay.t_lat % tm == 0
    nb8 = n // 8
    wrkv = p['wrkv'].astype(bf16)
    w1 = jnp.concatenate([p['w1'][0], p['w1'][1]], axis=1).astype(bf16)
    w2 = p['w2'].reshape(2 * p['w2'].shape[1], d).astype(bf16)
    a1 = jnp.concatenate([p['a1'][0], p['a1'][1]], axis=1).astype(bf16)
    a2 = p['a2'].reshape(2 * p['a2'].shape[1], d).astype(bf16)
    g1 = p['g1'].astype(bf16)
    g2 = p['g2'].astype(bf16)
    row_spec = pl.BlockSpec((tm, d), lambda i: (i, 0))
    z_spec = pl.BlockSpec((2, tm, d), lambda i: (0, i, 0))
    args = (x, x, x, mod, mod, p['mu'], wrkv, w1, w2, p['w0'], a1, a2, p['a0'], g1, g2)
    return pl.pallas_call(
        functools.partial(_rwkv_proj_kernel, lay=lay, tm=tm),
        grid=(n // tm,),
        in_specs=[
            row_spec,
            pl.BlockSpec((8, d), lambda i: (jnp.maximum(i * (tm // 8) - 1, 0), 0)),
            pl.BlockSpec((8, d), lambda i: (jnp.minimum((i + 1) * (tm // 8), nb8 - 1), 0)),
            _mod_spec(0, tm, lay),
            _mod_spec(1, tm, lay),
        ] + [_resident(a.shape) for a in args[5:]],
        out_specs=[row_spec] * 4 + [z_spec] * 2,
        out_shape=[jax.ShapeDtypeStruct((n, d), bf16)] * 4
        + [jax.ShapeDtypeStruct((2, n, d), bf16), jax.ShapeDtypeStruct((2, n, d), f32)],
        compiler_params=_params(("parallel",)),
        name="rwkv_proj",
    )(*args)


def _wkv_prep(r, k, v, a, lw, kk_gain, ka, forward):
    c_len = r.shape[0]
    r, k, v, a = (z.astype(f32) for z in (r, k, v, a))
    kkr = k * kk_gain
    kk = kkr * lax.rsqrt(jnp.maximum(_head_sum(kkr * kkr), 1e-24))
    beta = kk * a
    kd = k * (1.0 + (a - 1.0) * ka)
    row = lax.broadcasted_iota(jnp.int32, lw.shape, 0)
    cum = lw
    step = 1
    while step < c_len:
        if forward:
            cum = cum + jnp.where(row >= step, pltpu.roll(cum, step, 0), 0.0)
        else:
            cum = cum + jnp.where(row < c_len - step, pltpu.roll(cum, c_len - step, 0), 0.0)
        step *= 2
    total = cum[c_len - 1:c_len, :] if forward else cum[0:1, :]
    inv = jnp.exp(-cum)
    rem = jnp.exp(total - cum)
    return dict(a_t=-kk * jnp.exp(cum - lw), r_t=r * jnp.exp(cum), b_inv=beta * inv, k_inv=kd * inv,
                b_rem=beta * rem, k_rem=kd * rem, p_total=jnp.exp(total), v=v)


def _wkv_kernel(*refs, has_init):
    dir_refs = (refs[0:5], refs[5:10])
    kk_ref, ka_ref = refs[10:12]
    n_in = 13 if has_init else 12
    yf_ref, yb_ref, sout_ref, s_scr = refs[n_in:]
    n_par = yf_ref.shape[0]
    c_len = WKV_CHUNK
    gw = WKV_LANES
    hpg = gw // HEAD
    n_grp = yf_ref.shape[2] // gw
    c = pl.program_id(1)

    @pl.when(c == 0)
    def _():
        s_scr[...] = jnp.zeros_like(s_scr)
        if has_init:
            s0_ref = refs[n_in - 1]
            for p in range(n_par):
                for dr in range(2):
                    for h in range(n_grp * hpg):
                        o = (h % hpg) * HEAD
                        s_scr[p, dr, h // hpg, o:o + HEAD, o:o + HEAD] = s0_ref[p, dr, h]

    t_s = lax.broadcasted_iota(jnp.int32, (c_len, gw), 0)
    i_s = lax.broadcasted_iota(jnp.int32, (c_len, gw), 1) & (c_len - 1)
    strict = (i_s < t_s, i_s > t_s)
    incl = (i_s <= t_s, i_s >= t_s)
    eye = jnp.where(t_s == i_s, 1.0, 0.0)
    pair_block = (t_s >> 1) == (i_s >> 1)
    levels = []
    for lb in range(1, int(math.log2(c_len))):
        levels.append(((t_s >> (lb + 1)) == (i_s >> (lb + 1))) & ((t_s >> lb) != (i_s >> lb)))
    rb = lax.broadcasted_iota(jnp.int32, (hpg * c_len, gw), 0) // c_len
    cb = lax.broadcasted_iota(jnp.int32, (hpg * c_len, gw), 1) // HEAD
    same_head = rb == cb

    def bd(x):
        return jnp.where(same_head, jnp.concatenate([x] * hpg, axis=0), 0.0).astype(bf16)

    chains = []
    for p in range(n_par):
        for dr in range(2):
            op = _wkv_prep(*(ref[p] for ref in dir_refs[dr]), kk_ref[...], ka_ref[...], forward=(dr == 0))
            for pr in range(n_grp):
                chains.append((op, dr, (p, dr, pr), slice(pr * gw, (pr + 1) * gw),
                               (yf_ref if dr == 0 else yb_ref).at[p]))
    ids = range(len(chains))
    lhs = [jnp.concatenate([op['a_t'][:, cs], op['r_t'][:, cs]], axis=0).astype(bf16)
           for op, dr, si, cs, y_ref in chains]
    s0 = [s_scr[si] for op, dr, si, cs, y_ref in chains]
    sc = [_dot_nt(lhs[i], jnp.concatenate([bd(op['b_inv'][:, cs]), bd(op['k_inv'][:, cs]),
                                           s0[i].astype(bf16)], axis=0))
          for i, (op, dr, si, cs, y_ref) in enumerate(chains)]
    sc_b = [s[:, :gw] for s in sc]
    sc_k = [s[:, gw:2 * gw] for s in sc]
    x0 = [s[:, 2 * gw:] for s in sc]
    n_mat = [jnp.where(strict[dr], sc_b[i][:c_len], 0.0) for i, (op, dr, si, cs, y_ref) in enumerate(chains)]
    v_bd = [bd(op['v'][:, cs]) for op, dr, si, cs, y_ref in chains]
    x = [x0[i][:c_len] + _dot(jnp.where(strict[dr], sc_k[i][:c_len], 0.0).astype(bf16), v_bd[i])
         for i, (op, dr, si, cs, y_ref) in enumerate(chains)]
    t_inv = [eye + jnp.where(pair_block, n_mat[i], 0.0) for i in ids]
    for off_diag in levels:
        half = [_dot(t_inv[i].astype(bf16), bd(jnp.where(off_diag, n_mat[i], 0.0))) for i in ids]
        t_inv = [t_inv[i] + _dot(half[i].astype(bf16), bd(t_inv[i])) for i in ids]
    uv_bd = [jnp.concatenate([bd(_dot(t_inv[i].astype(bf16), bd(x[i]))), v_bd[i]], axis=0) for i in ids]
    for i, (op, dr, si, cs, y_ref) in enumerate(chains):
        rbk = jnp.concatenate([jnp.where(incl[dr], sc_b[i][c_len:], 0.0),
                               jnp.where(incl[dr], sc_k[i][c_len:], 0.0)], axis=1).astype(bf16)
        y_ref[:, cs] = (x0[i][c_len:] + _dot(rbk, uv_bd[i])).astype(y_ref.dtype)
    for i, (op, dr, si, cs, y_ref) in enumerate(chains):
        bk_rem = jnp.concatenate([bd(op['b_rem'][:, cs]), bd(op['k_rem'][:, cs])], axis=0)
        s_scr[si] = s0[i] * op['p_total'][:, cs] + _dot_tn(uv_bd[i], bk_rem)

    @pl.when(c == pl.num_programs(1) - 1)
    def _():
        for p in range(n_par):
            for dr in range(2):
                for h in range(n_grp * hpg):
                    o = (h % hpg) * HEAD
                    sout_ref[p, dr, h] = s_scr[p, dr, h // hpg, o:o + HEAD, o:o + HEAD]


def _wkv_scan(r, k, v, a2, lw2, p, s0, row0, n_seq, t_len, n_par=1):
    n, d = r.shape
    c_len = WKV_CHUNK
    nc = t_len // c_len
    n_grp = d // WKV_LANES
    has_init = s0 is not None
    if n_seq % n_par or (row0 // t_len) % n_par:
        n_par = 1
    assert n % t_len == 0 and row0 % t_len == 0
    u0 = row0 // t_len // n_par

    tok_f = pl.BlockSpec((n_par, c_len, d), lambda s, c: (u0 + s, c, 0))
    tok_b = pl.BlockSpec((n_par, c_len, d), lambda s, c: (u0 + s, nc - 1 - c, 0))
    z_f = pl.BlockSpec((None, n_par, c_len, d), lambda s, c: (0, u0 + s, c, 0))
    z_b = pl.BlockSpec((None, n_par, c_len, d), lambda s, c: (1, u0 + s, nc - 1 - c, 0))
    vec = pl.BlockSpec((1, d), lambda s, c: (0, 0))
    st = pl.BlockSpec((n_par, 2, RWKV_HEADS, HEAD, HEAD), lambda s, c: (s, 0, 0, 0, 0))
    r3, k3, v3 = (z.reshape(n // t_len, t_len, d) for z in (r, k, v))
    a4, lw4 = (z.reshape(2, n // t_len, t_len, d) for z in (a2, lw2))
    in_specs = [tok_f, tok_f, tok_f, z_f, z_f, tok_b, tok_b, tok_b, z_b, z_b, vec, vec]
    args = [r3, k3, v3, a4, lw4, r3, k3, v3, a4, lw4, p['k_k'].reshape(1, d), p['k_a'].reshape(1, d)]
    if has_init:
        in_specs.append(st)
        args.append(s0)
    y_shape = jax.ShapeDtypeStruct((n_seq, t_len, d), bf16)
    y_f, y_b, s_out = pl.pallas_call(
        functools.partial(_wkv_kernel, has_init=has_init),
        grid=(n_seq // n_par, nc),
        in_specs=in_specs,
        out_specs=[
            pl.BlockSpec((n_par, c_len, d), lambda s, c: (s, c, 0)),
            pl.BlockSpec((n_par, c_len, d), lambda s, c: (s, nc - 1 - c, 0)),
            st,
        ],
        out_shape=[y_shape, y_shape, jax.ShapeDtypeStruct((n_seq, 2, RWKV_HEADS, HEAD, HEAD), f32)],
        scratch_shapes=[pltpu.VMEM((n_par, 2, n_grp, WKV_LANES, WKV_LANES), f32)],
        compiler_params=_params(("parallel", "arbitrary")),
        name="wkv_scan",
    )(*args)
    return y_f.reshape(n_seq * t_len, d), y_b.reshape(n_seq * t_len, d), s_out


def _rwkv_post_kernel(x_ref, yfc_ref, ybc_ref, yfl_ref, ybl_ref, r_ref, k_ref, v_ref, g_ref, a_ref,
                      ka_ref, rk_ref, lnw_ref, lnb_ref, gate_ref, wo_ref, o_ref, *, nb_ctx):
    i = pl.program_id(0)
    y = (_pick_rows(i, nb_ctx, yfc_ref, yfl_ref).astype(f32)
         + _pick_rows(i, nb_ctx, ybc_ref, ybl_ref).astype(f32))
    mu = _head_sum(y) * (1.0 / HEAD)
    yc = y - mu
    var = _head_sum(yc * yc) * (1.0 / HEAD)
    yn = yc * lax.rsqrt(var + RWKV_GN_EPS) * lnw_ref[...] + lnb_ref[...]
    k = k_ref[...].astype(f32)
    ka = ka_ref[...]
    a0 = a_ref[0].astype(f32)
    a1 = a_ref[1].astype(f32)
    kd_sum = k * (1.0 + (a0 - 1.0) * ka) + k * (1.0 + (a1 - 1.0) * ka)
    bonus = _head_sum(r_ref[...].astype(f32) * kd_sum * rk_ref[...]) * v_ref[...].astype(f32)
    o = ((yn + bonus) * g_ref[...].astype(f32)).astype(bf16)
    o_ref[...] = x_ref[...] + gate_ref[...] * _dot(o, wo_ref[...])


def _rwkv_post_residual(x, mod, y_ctx, y_lat, r, k, v, g, a2, p, wo_bf16, lay, tm=256):
    n, d = r.shape
    tok = pl.BlockSpec((tm, d), lambda i: (i, 0))
    tok_z = pl.BlockSpec((2, tm, d), lambda i: (0, i, 0))
    vec = pl.BlockSpec((1, d), lambda i: (0, 0))
    yc_spec, yl_spec = _split_row_specs((tm, d), tm, lay)
    row = lambda a: a.reshape(1, d)
    return pl.pallas_call(
        functools.partial(_rwkv_post_kernel, nb_ctx=lay.n_ctx // tm),
        grid=(n // tm,),
        in_specs=[tok, yc_spec, yc_spec, yl_spec, yl_spec, tok, tok, tok, tok, tok_z, vec, vec, vec, vec,
                  _mod_spec(2, tm, lay), _resident((d, d))],
        out_specs=tok,
        out_shape=jax.ShapeDtypeStruct((n, d), f32),
        compiler_params=_params(("parallel",)),
        name="rwkv_post",
    )(x, y_ctx[0], y_ctx[1], y_lat[0], y_lat[1], r, k, v, g, a2,
      row(p['k_a']), row(p['r_k']), row(p['ln_w']), row(p['ln_b']), mod, wo_bf16)


def _rwkv_layer(x, mod, state0, p, wo_bf16, lay):
    r, k, v, g, a2, lw2 = _rwkv_proj(x, mod, p, lay)
    yf_c, yb_c, s_ctx = _wkv_scan(r, k, v, a2, lw2, p, None, 0, lay.b_ctx, lay.t_ctx)
    yf_l, yb_l, _ = _wkv_scan(r, k, v, a2, lw2, p, state0, lay.n_ctx, lay.b_lat, lay.t_lat)
    x = _rwkv_post_residual(x, mod, (yf_c, yb_c), (yf_l, yb_l), r, k, v, g, a2, p, wo_bf16, lay)
    return x, s_ctx


def _rope_table(n_tok, width):
    n_rows = n_tok // GRID_W
    rows = jnp.broadcast_to(jnp.arange(n_rows, dtype=f32)[:, None], (n_rows, GRID_W)).reshape(-1)
    cols = jnp.broadcast_to(jnp.arange(GRID_W, dtype=f32)[None, :], (n_rows, GRID_W)).reshape(-1)
    n_freq = HEAD // 4
    inv = ROPE_BASE ** (-jnp.arange(n_freq, dtype=f32) / n_freq)
    ang = jnp.concatenate([rows[:, None] * inv, cols[:, None] * inv], axis=-1)
    reps = width // (HEAD // 2)
    return jnp.tile(jnp.cos(ang), (1, reps)), jnp.tile(jnp.sin(ang), (1, reps))


def kernel(x_prompt, x_sample, cache_da_k, cache_da_v, state_rwkv, cache_swa_k, cache_swa_v, c, c_ctx, ada_w, ada_b, mlp_w1, mlp_w2, da_wqkv, da_q_norm, da_k_norm, da_lambda, da_subln, da_wo, rwkv_mu, rwkv_wrkv, rwkv_w0, rwkv_w1, rwkv_w2, rwkv_a0, rwkv_a1, rwkv_a2, rwkv_g1, rwkv_g2, rwkv_k_k, rwkv_k_a, rwkv_r_k, rwkv_ln_w, rwkv_ln_b, rwkv_wo, swa_wqkv, swa_q_norm, swa_k_norm, swa_sink, swa_wo):
    b_ctx, t_ctx, d = x_prompt.shape
    b_lat, t_lat, _ = x_sample.shape
    lay = Layout(b_ctx, t_ctx, b_lat, t_lat)
    depth = ada_w.shape[0]
    tm = 512
    assert d == D_MODEL and lay.n_ctx % t_lat == 0

    xs = (x_prompt.reshape(lay.n_ctx, d), x_sample.reshape(lay.n_lat, d))
    cond = jnp.concatenate([c_ctx[None, :], c, jnp.zeros((8 - 1 - b_lat, d), f32)], axis=0)
    mods = _modulation(cond, ada_w, ada_b)
    mods = mods[:, :1 + b_lat].reshape(depth, 1 + b_lat, 6, 1, d)

    rope_da = _rope_table(t_lat, LANES)
    rope_swa = _rope_table(t_lat, d)

    mlp_w1_b, mlp_w2_b = mlp_w1.astype(bf16), mlp_w2.astype(bf16)
    da_wqkv_b, da_wo_b = da_wqkv.astype(bf16), da_wo.astype(bf16)
    n_da = da_wqkv.shape[0]
    da_kv = None
    rwkv_s, swa_k, swa_v = [], [], []
    for i in range(depth):
        kind = i % N_MIXERS
        j = i // N_MIXERS
        mod = mods[i]
        w1 = LayerWeight(mlp_w1_b, i)
        w2 = LayerWeight(mlp_w2_b, i)
        if kind == 1:
            p = {'mu': rwkv_mu[j], 'wrkv': rwkv_wrkv[j], 'w0': rwkv_w0[j], 'w1': rwkv_w1[j],
                 'w2': rwkv_w2[j], 'a0': rwkv_a0[j], 'a1': rwkv_a1[j], 'a2': rwkv_a2[j],
                 'g1': rwkv_g1[j], 'g2': rwkv_g2[j], 'k_k': rwkv_k_k[j], 'k_a': rwkv_k_a[j],
                 'r_k': rwkv_r_k[j], 'ln_w': rwkv_ln_w[j], 'ln_b': rwkv_ln_b[j]}
            x = xs[0] if len(xs) == 1 else jnp.concatenate(xs, axis=0)
            x, s_new = _rwkv_layer(x, mod, state_rwkv[:, j], p, rwkv_wo[j].astype(bf16), lay)
            rwkv_s.append(s_new)
            xs = (_mlp(x, mod, w1, w2, lay),)
            continue
        if kind == 0:
            p = {'q_norm': da_q_norm[j], 'k_norm': da_k_norm[j], 'lam': da_lambda[j], 'subln': da_subln[j]}
            lam_init = 0.8 - 0.6 * math.exp(-0.3 * i)
            q, kv = _ln_matmul(xs, mod, LayerWeight(da_wqkv_b, j), lay)
            o_ctx, o_lat, *da_kv = _da_mixer(q, kv, cache_da_k, cache_da_v, j, n_da, da_kv, p, lam_init,
                                             rope_da, lay)
            wo = LayerWeight(da_wo_b, j)
        else:
            p = {'q_norm': swa_q_norm[j], 'k_norm': swa_k_norm[j], 'sink': swa_sink[j]}
            wqkv, wo = _swa_group_major_weights(swa_wqkv[j], swa_wo[j])
            q, kv = _ln_matmul(xs, mod, wqkv, lay)
            o_ctx, o_lat, k_new, v_new = _swa_mixer(q, kv, cache_swa_k, cache_swa_v, j, p, rope_swa, lay)
            swa_k.append(k_new.reshape(b_ctx, t_ctx, SWA_KV_HEADS, HEAD))
            swa_v.append(v_new.reshape(b_ctx, t_ctx, SWA_KV_HEADS, HEAD))
        if i == depth - 1 and len(xs) == 1:
            nb_ctx = lay.n_ctx // tm
            xs = (_out_proj_mlp(xs, (o_ctx,), mod, wo, w1, w2, lay, rows=(0, nb_ctx), tm=tm),
                  _out_proj_mlp(xs, (o_lat,), mod, wo, w1, w2, lay, rows=(nb_ctx, lay.n_lat // tm), tm=tm))
        else:
            xs = (_out_proj_mlp(xs, (o_ctx, o_lat), mod, wo, w1, w2, lay, tm=tm),)

    if len(xs) == 1:
        xs = (xs[0][:lay.n_ctx], xs[0][lay.n_ctx:])
    y_prompt = xs[0].reshape(b_ctx, t_ctx, d)
    y_sample = xs[1].reshape(b_lat, t_lat, d)
    new_da_k, new_da_v = (a.reshape(b_ctx, n_da, t_ctx, DA_HEADS, LANES) for a in da_kv)
    return (y_prompt, y_sample, new_da_k, new_da_v,
            jnp.stack(rwkv_s, axis=1), jnp.stack(swa_k, axis=1), jnp.stack(swa_v, axis=1))
```

```python
import functools
import math
from typing import NamedTuple

import jax
import jax.numpy as jnp
from jax import lax
from jax.experimental import pallas as pl
from jax.experimental.pallas import tpu as pltpu

f32 = jnp.float32
bf16 = jnp.bfloat16

D_MODEL = 1024
N_MIXERS = 3
GRID_W = 64
ROPE_BASE = 10000.0
NORM_EPS = 1e-6
NEG_INF = -1e30
LOG2E = math.log2(math.e)
HEAD = 64
LANES = 128
DA_HEADS = D_MODEL // LANES
RWKV_HEADS = D_MODEL // HEAD
RWKV_GN_EPS = 64e-5
SWA_HEADS = D_MODEL // HEAD
SWA_KV_HEADS = SWA_HEADS // 4
SWA_GROUP = SWA_HEADS // SWA_KV_HEADS
SWA_KV = SWA_KV_HEADS * HEAD
WINDOW = 128
WKV_CHUNK = 64
WKV_LANES = 128
VMEM_LIMIT = 56 * 1024 * 1024
QK_SCALE = HEAD ** -0.5 * LOG2E


class Layout(NamedTuple):
    b_ctx: int
    t_ctx: int
    b_lat: int
    t_lat: int

    @property
    def n_ctx(self):
        return self.b_ctx * self.t_ctx

    @property
    def n_lat(self):
        return self.b_lat * self.t_lat

    @property
    def n(self):
        return self.n_ctx + self.n_lat


def _params(sem):
    return pltpu.CompilerParams(dimension_semantics=sem, vmem_limit_bytes=VMEM_LIMIT)


def _group_of_tile(i, tm, lay):
    r = i * tm
    return jnp.where(r < lay.n_ctx, 0, 1 + (r - lay.n_ctx) // lay.t_lat)


def _mod_spec(chunk, tm, lay, tile0=0):
    return pl.BlockSpec((None, None, 1, D_MODEL),
                        lambda i, *_: (_group_of_tile(i + tile0, tm, lay), chunk, 0, 0))


def _resident(shape):
    return pl.BlockSpec(shape, lambda *_: (0,) * len(shape), pipeline_mode=pl.Buffered(1))


class LayerWeight(NamedTuple):
    stack: jax.Array
    layer: int

    @property
    def shape(self):
        return self.stack.shape[1:]

    def spec(self):
        idx = (self.layer,) + (0,) * (self.stack.ndim - 1)
        return pl.BlockSpec((None,) + self.shape, lambda *_: idx, pipeline_mode=pl.Buffered(1))


def _weight_spec(w):
    return w.spec() if isinstance(w, LayerWeight) else _resident(w.shape)


def _weight_arg(w):
    return w.stack if isinstance(w, LayerWeight) else w


def _split_row_specs(block, tm, lay):
    nb_ctx = lay.n_ctx // tm
    lead = (0,) * (len(block) - 2)
    ctx = pl.BlockSpec(block, lambda i, *_: lead + (jnp.minimum(i, nb_ctx - 1), 0))
    lat = pl.BlockSpec(block, lambda i, *_: lead + (jnp.maximum(i - nb_ctx, 0), 0))
    return ctx, lat


def _dot(a, b):
    return jnp.dot(a, b, preferred_element_type=f32)


def _dot_nt(a, b):
    return lax.dot_general(a, b, (((1,), (1,)), ((), ())), preferred_element_type=f32)


def _dot_tn(a, b):
    return lax.dot_general(a, b, (((0,), (0,)), ((), ())), preferred_element_type=f32)


def _split2(x):
    hi = x.astype(bf16)
    return hi, (x - hi.astype(f32)).astype(bf16)


def _head_ones():
    r = lax.broadcasted_iota(jnp.int32, (LANES, LANES), 0)
    c = lax.broadcasted_iota(jnp.int32, (LANES, LANES), 1)
    return ((r // HEAD) == (c // HEAD)).astype(bf16)


def _head_sum(x):
    g = _head_ones()
    xb = x.astype(bf16)
    cols = [_dot(xb[:, p * LANES:(p + 1) * LANES], g) for p in range(x.shape[1] // LANES)]
    return cols[0] if len(cols) == 1 else jnp.concatenate(cols, axis=1)


def _head_rms(x, gain):
    ms = _head_sum(x * x) * (1.0 / HEAD)
    return x * lax.rsqrt(ms + NORM_EPS) * gain


def _adaln(x, shift, scale):
    ms = jnp.mean(x * x, axis=-1, keepdims=True)
    return x * lax.rsqrt(ms + NORM_EPS) * (1.0 + scale) + shift


def _rope(x, cos, sin):
    w = x.shape[1]
    lane = lax.broadcasted_iota(jnp.int32, x.shape, 1)
    low = (lane & (HEAD - 1)) < HEAD // 2
    up = pltpu.roll(x, w - HEAD // 2, 1)
    dn = pltpu.roll(x, HEAD // 2, 1)
    return x * cos + jnp.where(low, -up, dn) * sin


def _pick_rows(i, nb_ctx, ctx_ref, lat_ref):
    return jnp.where(i < nb_ctx, ctx_ref[...], lat_ref[...])


def _mod_kernel(c_ref, w_ref, b_ref, o_ref):
    c = c_ref[...]
    s = c * jax.nn.sigmoid(c)
    sh, sl = _split2(s)
    wh, wl = _split2(w_ref[...])
    o_ref[...] = _dot(sh, wh) + (_dot(sh, wl) + _dot(sl, wh)) + b_ref[...]


def _modulation(cond8, ada_w, ada_b):
    depth, d, d6 = ada_w.shape
    tn = 1536
    return pl.pallas_call(
        _mod_kernel,
        grid=(depth, d6 // tn),
        in_specs=[
            pl.BlockSpec((8, d), lambda l, j: (0, 0)),
            pl.BlockSpec((None, d, tn), lambda l, j: (l, 0, j)),
            pl.BlockSpec((None, 1, tn), lambda l, j: (l, 0, j)),
        ],
        out_specs=pl.BlockSpec((None, 8, tn), lambda l, j: (l, 0, j)),
        out_shape=jax.ShapeDtypeStruct((depth, 8, d6), f32),
        compiler_params=_params(("parallel", "parallel")),
        name="modulation",
    )(cond8, ada_w, ada_b.reshape(depth, 1, d6))


def _ln_mm_kernel(*refs, tn, nb_ctx):
    if nb_ctx is None:
        x_ref, sh_ref, sc_ref, w_ref, q_ref, kv_ref = refs
        x = x_ref[...]
    else:
        xc_ref, xl_ref, sh_ref, sc_ref, w_ref, q_ref, kv_ref = refs
        x = _pick_rows(pl.program_id(0), nb_ctx, xc_ref, xl_ref)
    h = _adaln(x, sh_ref[...], sc_ref[...]).astype(bf16)
    n_q = q_ref.shape[1]
    for j in range(w_ref.shape[1] // tn):
        cols = slice(j * tn, (j + 1) * tn)
        part = _dot(h, w_ref[:, cols])
        if j * tn < n_q:
            q_ref[:, cols] = part.astype(q_ref.dtype)
        else:
            kv_ref[:, j * tn - n_q:(j + 1) * tn - n_q] = part


def _ln_matmul(xs, mod, w, lay, tm=512, tn=512):
    d = xs[0].shape[1]
    n_out = w.shape[1]
    assert d % tn == 0
    if len(xs) == 1:
        x_specs = [pl.BlockSpec((tm, d), lambda i: (i, 0))]
        nb_ctx = None
    else:
        x_specs = list(_split_row_specs((tm, d), tm, lay))
        nb_ctx = lay.n_ctx // tm
    return pl.pallas_call(
        functools.partial(_ln_mm_kernel, tn=tn, nb_ctx=nb_ctx),
        grid=(lay.n // tm,),
        in_specs=x_specs + [_mod_spec(0, tm, lay), _mod_spec(1, tm, lay), _weight_spec(w)],
        out_specs=[pl.BlockSpec((tm, d), lambda i: (i, 0)), pl.BlockSpec((tm, n_out - d), lambda i: (i, 0))],
        out_shape=[jax.ShapeDtypeStruct((lay.n, d), bf16), jax.ShapeDtypeStruct((lay.n, n_out - d), f32)],
        compiler_params=_params(("parallel",)),
        name="adaln_proj",
    )(*xs, mod, mod, _weight_arg(w))


def _mlp_body(x, sh_ref, sc_ref, g_ref, w1_ref, w2_ref, acc_ref, o_ref, fc):
    h = _adaln(x, sh_ref[...], sc_ref[...]).astype(bf16)
    d_ff = w1_ref.shape[1]
    for c in range(d_ff // fc):
        a = _dot(h, w1_ref[:, c * fc:(c + 1) * fc])
        a = jnp.square(jnp.maximum(a, 0.0)).astype(bf16)
        part = _dot(a, w2_ref[c * fc:(c + 1) * fc, :])
        if c == 0:
            acc_ref[...] = part
        else:
            acc_ref[...] += part
    o_ref[...] = x + g_ref[...] * acc_ref[...]


def _mlp_kernel(x_ref, sh_ref, sc_ref, g_ref, w1_ref, w2_ref, o_ref, acc_ref, *, fc):
    _mlp_body(x_ref[...], sh_ref, sc_ref, g_ref, w1_ref, w2_ref, acc_ref, o_ref, fc)


def _mlp(x, mod, w1, w2, lay, tm=512, fc=1024):
    n, d = x.shape
    return pl.pallas_call(
        functools.partial(_mlp_kernel, fc=fc),
        grid=(n // tm,),
        in_specs=[
            pl.BlockSpec((tm, d), lambda i: (i, 0)),
            _mod_spec(3, tm, lay), _mod_spec(4, tm, lay), _mod_spec(5, tm, lay),
            _weight_spec(w1), _weight_spec(w2),
        ],
        out_specs=pl.BlockSpec((tm, d), lambda i: (i, 0)),
        out_shape=jax.ShapeDtypeStruct((n, d), f32),
        scratch_shapes=[pltpu.VMEM((tm, d), f32)],
        compiler_params=_params(("parallel",)),
        name="mlp",
    )(x, mod, mod, mod, _weight_arg(w1), _weight_arg(w2))


def _out_mlp_kernel(*refs, fc, nb_ctx, split_x, split_a):
    refs = list(refs)
    i = pl.program_id(0)
    x = _pick_rows(i, nb_ctx, refs.pop(0), refs.pop(0)) if split_x else refs.pop(0)[...]
    a = _pick_rows(i, nb_ctx, refs.pop(0), refs.pop(0)) if split_a else refs.pop(0)[...]
    gm_ref, wo_ref, sh_ref, sc_ref, gf_ref, w1_ref, w2_ref, o_ref, acc_ref = refs
    x1 = x + gm_ref[...] * _dot(a.astype(bf16), wo_ref[...])
    _mlp_body(x1, sh_ref, sc_ref, gf_ref, w1_ref, w2_ref, acc_ref, o_ref, fc)


def _out_proj_mlp(xs, a_parts, mod, wo, w1, w2, lay, rows=None, tm=512, fc=1024):
    d = D_MODEL
    k_in = a_parts[0].shape[1]
    tile0, n_tiles = rows if rows is not None else (0, lay.n // tm)
    split_x, split_a = len(xs) == 2, len(a_parts) == 2
    assert rows is None or not (split_x or split_a)
    x_specs = (list(_split_row_specs((tm, d), tm, lay)) if split_x
               else [pl.BlockSpec((tm, d), lambda i: (i + tile0, 0))])
    a_specs = (list(_split_row_specs((tm, k_in), tm, lay)) if split_a
               else [pl.BlockSpec((tm, k_in), lambda i: (i, 0))])
    return pl.pallas_call(
        functools.partial(_out_mlp_kernel, fc=fc, nb_ctx=lay.n_ctx // tm, split_x=split_x, split_a=split_a),
        grid=(n_tiles,),
        in_specs=x_specs + a_specs + [
            _mod_spec(2, tm, lay, tile0), _weight_spec(wo),
            _mod_spec(3, tm, lay, tile0), _mod_spec(4, tm, lay, tile0), _mod_spec(5, tm, lay, tile0),
            _weight_spec(w1), _weight_spec(w2),
        ],
        out_specs=pl.BlockSpec((tm, d), lambda i: (i, 0)),
        out_shape=jax.ShapeDtypeStruct((n_tiles * tm, d), f32),
        scratch_shapes=[pltpu.VMEM((tm, d), f32)],
        compiler_params=_params(("parallel",)),
        name="out_proj_mlp",
    )(*xs, *a_parts, mod, _weight_arg(wo), mod, mod, mod, _weight_arg(w1), _weight_arg(w2))


def _da_lambda(lam_ref, lam_init):
    lp = lam_ref[...]
    e1 = jnp.exp(jnp.sum(lp[0:1] * lp[1:2], axis=-1, keepdims=True))
    e2 = jnp.exp(jnp.sum(lp[2:3] * lp[3:4], axis=-1, keepdims=True))
    return e1 - e2 + lam_init


def _da_qk(q, k_bf16):
    lane = lax.broadcasted_iota(jnp.int32, q.shape, 1)
    maps = [jnp.where((lane < HEAD) == (m == 0), q, 0.0).astype(bf16) for m in range(2)]
    return _dot_nt(jnp.concatenate(maps, axis=0), k_bf16)


def _with_ones(v_bf16):
    return jnp.concatenate([v_bf16, jnp.ones_like(v_bf16)], axis=1)


def _da_pv(scores, v_ones, lam):
    t = scores.shape[0] // 2
    e = jnp.exp2(scores - jnp.max(scores, axis=-1, keepdims=True))
    pv = _dot(e.astype(bf16), v_ones)
    o = pv[:, :LANES] * (1.0 / pv[:, LANES:])
    return o[:t] - lam * o[t:]


def _da_finish(o, sub_ref, lam_init):
    ms = jnp.mean(o * o, axis=-1, keepdims=True)
    return o * lax.rsqrt(ms + NORM_EPS) * sub_ref[...] * (1.0 - lam_init)


def _da_ctx_kernel(*refs, lam_init, layer):
    lam_ref, q_ref, k_ref, v_ref, qn_ref, kn_ref, sub_ref = refs[:7]
    o_ref, knew_ref, vnew_ref = refs[-3:]
    lam = _da_lambda(lam_ref, lam_init)
    q = _head_rms(q_ref[...].astype(f32), qn_ref[...]) * QK_SCALE
    k = _head_rms(k_ref[...], kn_ref[...])
    v = v_ref[...]
    if layer == 0:
        for l in range(knew_ref.shape[0]):
            knew_ref[l] = k if l == 0 else jnp.zeros_like(k)
            vnew_ref[l] = v if l == 0 else jnp.zeros_like(v)
    else:
        knew_ref[...] = k
        vnew_ref[...] = v
    kb = k.astype(bf16)
    vb = v.astype(bf16)
    cols = [slice(h * LANES, (h + 1) * LANES) for h in range(q.shape[1] // LANES)]
    s_next = _da_qk(q[:, cols[0]], kb[:, cols[0]])
    for h, cs in enumerate(cols):
        s_cur = s_next
        if h + 1 < len(cols):
            s_next = _da_qk(q[:, cols[h + 1]], kb[:, cols[h + 1]])
        o = _da_finish(_da_pv(s_cur, _with_ones(vb[:, cs]), lam), sub_ref, lam_init)
        o_ref[:, cs] = o.astype(o_ref.dtype)


def _da_lat_kernel(lam_ref, q_ref, k_ref, v_ref, ck_ref, cv_ref, qn_ref, kn_ref, sub_ref,
                   cq_ref, sq_ref, ckk_ref, skk_ref, o_ref, kf_scr, vf_scr, *, lam_init, past):
    cols = [slice(h * LANES, (h + 1) * LANES) for h in range(q_ref.shape[1] // LANES)]

    @pl.when(pl.program_id(2) == 0)
    def _():
        k = _rope(_head_rms(k_ref[...], kn_ref[...]), ckk_ref[...], skk_ref[...])
        kf_scr[0:past, :] = ck_ref[...].astype(bf16)
        kf_scr[past:, :] = k.astype(bf16)
        for h, cs in enumerate(cols):
            vf_scr[h, 0:past, :] = _with_ones(cv_ref[:, cs].astype(bf16))
            vf_scr[h, past:, :] = _with_ones(v_ref[:, cs].astype(bf16))

    lam = _da_lambda(lam_ref, lam_init)
    q = _rope(_head_rms(q_ref[...].astype(f32), qn_ref[...]), cq_ref[...], sq_ref[...]) * QK_SCALE
    s_next = _da_qk(q[:, cols[0]], kf_scr[:, cols[0]])
    for h, cs in enumerate(cols):
        s_cur = s_next
        if h + 1 < len(cols):
            s_next = _da_qk(q[:, cols[h + 1]], kf_scr[:, cols[h + 1]])
        o_ref[:, cs] = _da_finish(_da_pv(s_cur, vf_scr[h], lam), sub_ref, lam_init).astype(o_ref.dtype)


def _da_mixer(q, kv, cache_k, cache_v, j, n_layers, kv_prev, p, lam_init, rope_tab, lay, tq=256):
    d = D_MODEL
    nh = DA_HEADS
    assert (kv_prev is None) == (j == 0)
    qn = jnp.tile(p['q_norm'], 2).reshape(1, LANES)
    kn = jnp.tile(p['k_norm'], 2).reshape(1, LANES)
    qn_all = jnp.tile(qn, (1, nh))
    kn_all = jnp.tile(kn, (1, nh))
    sub = p['subln'].reshape(1, LANES)
    small = lambda shape: pl.BlockSpec(shape, lambda *_: (0,) * len(shape))
    t = lay.t_ctx
    ctx_rows = pl.BlockSpec((t, d), lambda b: (b, 0))
    kv_shape = jax.ShapeDtypeStruct((lay.b_ctx, n_layers, t, d), f32)
    if kv_prev is None:
        kv_spec = pl.BlockSpec((None, n_layers, t, d), lambda b: (b, 0, 0, 0))
        kv_in_specs, kv_args, aliases = [], (), {}
    else:
        kv_spec = pl.BlockSpec((None, None, t, d), lambda b: (b, j, 0, 0))
        kv_in_specs = [pl.BlockSpec(memory_space=pl.ANY)] * 2
        kv_args, aliases = tuple(kv_prev), {7: 1, 8: 2}
    o_ctx, k_new, v_new = pl.pallas_call(
        functools.partial(_da_ctx_kernel, lam_init=lam_init, layer=j),
        grid=(lay.b_ctx,),
        in_specs=[
            small((4, HEAD)),
            ctx_rows,
            ctx_rows,
            pl.BlockSpec((t, d), lambda b: (b, 1)),
            small((1, d)), small((1, d)), small((1, LANES)),
        ] + kv_in_specs,
        out_specs=[ctx_rows, kv_spec, kv_spec],
        out_shape=[jax.ShapeDtypeStruct((lay.n_ctx, d), bf16), kv_shape, kv_shape],
        input_output_aliases=aliases,
        compiler_params=_params(("parallel",)),
        name="da_ctx",
    )(p['lam'], q, kv, kv, qn_all, kn_all, sub, *kv_args)

    tl = lay.t_lat
    past = cache_k.shape[2]
    row0 = lay.n_ctx // tl
    q0 = lay.n_ctx // tq
    nq = tl // tq
    cos, sin = rope_tab
    ck = cache_k.reshape(lay.b_lat, cache_k.shape[1], past, d)
    cv = cache_v.reshape(lay.b_lat, cache_v.shape[1], past, d)
    hg = 2
    gw = hg * LANES
    ng = nh // hg
    cos_g, sin_g = jnp.tile(cos, (1, hg)), jnp.tile(sin, (1, hg))
    o_lat = pl.pallas_call(
        functools.partial(_da_lat_kernel, lam_init=lam_init, past=past),
        grid=(lay.b_lat, ng, nq),
        in_specs=[
            small((4, HEAD)),
            pl.BlockSpec((tq, gw), lambda b, h, i: (q0 + b * nq + i, h)),
            pl.BlockSpec((tl, gw), lambda b, h, i: (row0 + b, h)),
            pl.BlockSpec((tl, gw), lambda b, h, i: (row0 + b, ng + h)),
            pl.BlockSpec((None, None, past, gw), lambda b, h, i: (b, j, 0, h)),
            pl.BlockSpec((None, None, past, gw), lambda b, h, i: (b, j, 0, h)),
            small((1, gw)), small((1, gw)), small((1, LANES)),
            pl.BlockSpec((tq, gw), lambda b, h, i: (i, 0)),
            pl.BlockSpec((tq, gw), lambda b, h, i: (i, 0)),
            _resident((tl, gw)), _resident((tl, gw)),
        ],
        out_specs=pl.BlockSpec((tq, gw), lambda b, h, i: (b * nq + i, h)),
        out_shape=jax.ShapeDtypeStruct((lay.n_lat, d), bf16),
        scratch_shapes=[pltpu.VMEM((past + tl, gw), bf16), pltpu.VMEM((hg, past + tl, 2 * LANES), bf16)],
        compiler_params=_params(("parallel", "parallel", "arbitrary")),
        name="da_lat",
    )(p['lam'], q, kv, kv, ck, cv, qn_all[:, :gw], kn_all[:, :gw], sub, cos_g, sin_g, cos_g, sin_g)
    return o_ctx, o_lat, k_new, v_new


def _swa_head_mask(shape, j):
    lane = lax.broadcasted_iota(jnp.int32, shape, 1)
    return (lane >= j * HEAD) & (lane < (j + 1) * HEAD)


def _swa_heads(q, sink_ref, score_fn, mask_fn, pv_fn, o_ref):
    t = q.shape[0]

    def scores(g):
        qg = q[:, g * SWA_KV:(g + 1) * SWA_KV]
        rows = [jnp.where(_swa_head_mask(qg.shape, j), qg, 0.0).astype(bf16) for j in range(SWA_KV_HEADS)]
        return score_fn(jnp.concatenate(rows, axis=0))

    s_next = scores(0)
    for g in range(SWA_GROUP):
        s_grp = s_next
        if g + 1 < SWA_GROUP:
            s_next = scores(g + 1)
        o = None
        for j in range(SWA_KV_HEADS):
            s_cur = mask_fn([s[j * t:(j + 1) * t] for s in s_grp])
            sink = sink_ref[j * SWA_GROUP + g] * LOG2E
            m = sink
            for s in s_cur:
                m = jnp.maximum(jnp.max(s, axis=-1, keepdims=True), m)
            exps = [jnp.exp2(s - m) for s in s_cur]
            den = jnp.exp2(sink - m)
            for e in exps:
                den = den + jnp.sum(e, axis=-1, keepdims=True)
            oj = pv_fn(j, [e.astype(bf16) for e in exps]) * (1.0 / den)
            o = oj if j == 0 else o + oj
        o_ref[:, g * SWA_KV:(g + 1) * SWA_KV] = o.astype(o_ref.dtype)


def _swa_ctx_kernel(sink_ref, q_ref, k_ref, v_ref, qn_ref, kn_ref, o_ref, knew_ref, vnew_ref):
    q = _head_rms(q_ref[...].astype(f32), qn_ref[...]) * QK_SCALE
    k = _head_rms(k_ref[...], kn_ref[...])
    v = v_ref[...]
    knew_ref[...] = k
    vnew_ref[...] = v
    kb = k.astype(bf16)
    v_heads = [jnp.where(_swa_head_mask(v.shape, j), v, 0.0).astype(bf16) for j in range(SWA_KV_HEADS)]
    _swa_heads(q, sink_ref, lambda qm: [_dot_nt(qm, kb)], lambda s: s,
               lambda j, e: _dot(e[0], v_heads[j]), o_ref)


def _swa_lat_kernel(sink_ref, q_ref, k_ref, v_ref, ck_ref, cv_ref, qn_ref, kn_ref,
                    cq_ref, sq_ref, ckk_ref, skk_ref, o_ref, kp_scr, vp_scr, *, t_lat):
    n = pl.program_id(1)
    w = WINDOW

    @pl.when(n == 0)
    def _():
        zeros = jnp.zeros((w, SWA_KV), bf16)
        k = _rope(_head_rms(k_ref[...], kn_ref[...]), ckk_ref[...], skk_ref[...])
        kp_scr[0:w, :] = zeros
        kp_scr[w:w + t_lat, :] = k.astype(bf16)
        kp_scr[w + t_lat:, :] = zeros
        vp_scr[0:w, :] = zeros
        vp_scr[w:w + t_lat, :] = v_ref[...].astype(bf16)
        vp_scr[w + t_lat:, :] = zeros

    q = _rope(_head_rms(q_ref[...].astype(f32), qn_ref[...]), cq_ref[...], sq_ref[...]) * QK_SCALE
    start = pl.multiple_of(n * w, w)
    kband = kp_scr[pl.ds(start, 3 * w), :]
    vband = vp_scr[pl.ds(start, 3 * w), :]
    ckb = ck_ref[...].astype(bf16)
    cv = cv_ref[...]
    qi = lax.broadcasted_iota(jnp.int32, (w, 3 * w), 0)
    si = lax.broadcasted_iota(jnp.int32, (w, 3 * w), 1)
    rel = si - qi
    key_pos = (n - 1) * w + si
    valid = (rel >= 0) & (rel <= 2 * w) & (key_pos >= 0) & (key_pos < t_lat)
    cv_heads = [jnp.where(_swa_head_mask(cv.shape, j), cv, 0.0).astype(bf16) for j in range(SWA_KV_HEADS)]
    vb_heads = [jnp.where(_swa_head_mask(vband.shape, j), vband, jnp.zeros_like(vband))
                for j in range(SWA_KV_HEADS)]
    _swa_heads(q, sink_ref,
               lambda qm: [_dot_nt(qm, ckb), _dot_nt(qm, kband)],
               lambda s: [s[0], jnp.where(valid, s[1], NEG_INF)],
               lambda j, e: _dot(e[0], cv_heads[j]) + _dot(e[1], vb_heads[j]),
               o_ref)


def _swa_group_major_weights(wqkv, wo):
    d = D_MODEL
    wq = wqkv[:, :d].reshape(d, SWA_KV_HEADS, SWA_GROUP, HEAD).transpose(0, 2, 1, 3).reshape(d, d)
    wo = wo.reshape(SWA_KV_HEADS, SWA_GROUP, HEAD, d).transpose(1, 0, 2, 3).reshape(d, d)
    return jnp.concatenate([wq, wqkv[:, d:]], axis=1).astype(bf16), wo.astype(bf16)


def _swa_mixer(q, kv, cache_k, cache_v, j, p, rope_tab, lay):
    d = D_MODEL
    qn = jnp.tile(p['q_norm'], d // HEAD).reshape(1, d)
    kn = jnp.tile(p['k_norm'], SWA_KV_HEADS).reshape(1, SWA_KV)
    small = lambda shape: pl.BlockSpec(shape, lambda *_: (0,) * len(shape))
    smem = pl.BlockSpec(memory_space=pltpu.SMEM)
    t = lay.t_ctx
    kv_rows = pl.BlockSpec((t, SWA_KV), lambda b: (b, 0))
    o_ctx, k_new, v_new = pl.pallas_call(
        _swa_ctx_kernel,
        grid=(lay.b_ctx,),
        in_specs=[
            smem,
            pl.BlockSpec((t, d), lambda b: (b, 0)),
            kv_rows,
            pl.BlockSpec((t, SWA_KV), lambda b: (b, 1)),
            small((1, d)), small((1, SWA_KV)),
        ],
        out_specs=[pl.BlockSpec((t, d), lambda b: (b, 0)), kv_rows, kv_rows],
        out_shape=[jax.ShapeDtypeStruct((lay.n_ctx, d), bf16),
                   jax.ShapeDtypeStruct((lay.n_ctx, SWA_KV), f32),
                   jax.ShapeDtypeStruct((lay.n_ctx, SWA_KV), f32)],
        compiler_params=_params(("parallel",)),
        name="swa_ctx",
    )(p['sink'], q, kv, kv, qn, kn)

    tl = lay.t_lat
    w = WINDOW
    past = cache_k.shape[2]
    row0 = lay.n_ctx // tl
    q0 = lay.n_ctx // w
    nq = tl // w
    cos, sin = rope_tab
    ck = cache_k.reshape(lay.b_lat, cache_k.shape[1], past, SWA_KV)
    cv = cache_v.reshape(lay.b_lat, cache_v.shape[1], past, SWA_KV)
    o_lat = pl.pallas_call(
        functools.partial(_swa_lat_kernel, t_lat=tl),
        grid=(lay.b_lat, nq),
        in_specs=[
            smem,
            pl.BlockSpec((w, d), lambda b, i: (q0 + b * nq + i, 0)),
            pl.BlockSpec((tl, SWA_KV), lambda b, i: (row0 + b, 0)),
            pl.BlockSpec((tl, SWA_KV), lambda b, i: (row0 + b, 1)),
            pl.BlockSpec((None, None, past, SWA_KV), lambda b, i: (b, j, 0, 0)),
            pl.BlockSpec((None, None, past, SWA_KV), lambda b, i: (b, j, 0, 0)),
            small((1, d)), small((1, SWA_KV)),
            pl.BlockSpec((w, d), lambda b, i: (i, 0)),
            pl.BlockSpec((w, d), lambda b, i: (i, 0)),
            pl.BlockSpec((tl, SWA_KV), lambda b, i: (0, 0)),
            pl.BlockSpec((tl, SWA_KV), lambda b, i: (0, 0)),
        ],
        out_specs=pl.BlockSpec((w, d), lambda b, i: (b * nq + i, 0)),
        out_shape=jax.ShapeDtypeStruct((lay.n_lat, d), bf16),
        scratch_shapes=[pltpu.VMEM((tl + 2 * w, SWA_KV), bf16), pltpu.VMEM((tl + 2 * w, SWA_KV), bf16)],
        compiler_params=_params(("parallel", "arbitrary")),
        name="swa_lat",
    )(p['sink'], q, kv, kv, ck, cv, qn, kn, cos, sin, cos, sin)
    return o_ctx, o_lat, k_new, v_new


def _rwkv_proj_kernel(x_ref, xp_ref, xn_ref, sh_ref, sc_ref, mu_ref, wrkv_ref, w1_ref, w2_ref, w0_ref,
                      a1_ref, a2_ref, a0_ref, g1_ref, g2_ref,
                      r_ref, k_ref, v_ref, g_ref, a_ref, lw_ref, *, lay, tm):
    i = pl.program_id(0)
    r0 = i * tm
    in_ctx = r0 < lay.n_ctx
    pos = jnp.where(in_ctx, r0 % lay.t_ctx, (r0 - lay.n_ctx) % lay.t_lat)
    seq_len = jnp.where(in_ctx, lay.t_ctx, lay.t_lat)
    keep_prev = jnp.where(pos == 0, 0.0, 1.0)
    keep_next = jnp.where(pos + tm == seq_len, 0.0, 1.0)

    sh = sh_ref[...]
    sc = sc_ref[...]
    h = _adaln(x_ref[...], sh, sc)
    h_prev_row = _adaln(xp_ref[7:8, :], sh, sc) * keep_prev
    h_next_row = _adaln(xn_ref[0:1, :], sh, sc) * keep_next
    row = lax.broadcasted_iota(jnp.int32, h.shape, 0)
    prev = jnp.where(row == 0, h_prev_row, pltpu.roll(h, 1, 0))
    nxt = jnp.where(row == tm - 1, h_next_row, pltpu.roll(h, tm - 1, 0))
    xx = 0.5 * (prev + nxt) - h

    def mix(m):
        return (h + xx * mu_ref[m:m + 1, :]).astype(bf16)

    r_ref[...] = _dot(mix(0), wrkv_ref[0]).astype(r_ref.dtype)
    k_ref[...] = _dot(mix(1), wrkv_ref[1]).astype(k_ref.dtype)
    v_ref[...] = _dot(mix(2), wrkv_ref[2]).astype(v_ref.dtype)

    lane = lax.broadcasted_iota(jnp.int32, (tm, LANES), 1)
    first = lane < HEAD

    tw = jnp.tanh(_dot(mix(3), w1_ref[...]))
    la = _dot(mix(4), a1_ref[...])
    for z in range(2):
        sel = first if z == 0 else jnp.logical_not(first)
        w_lora = _dot(jnp.where(sel, tw, 0.0).astype(bf16), w2_ref[...])
        lw_ref[z] = -math.exp(-0.5) * jax.nn.sigmoid(w0_ref[z:z + 1, :] + w_lora)
        a_lora = _dot(jnp.where(sel, la, 0.0).astype(bf16), a2_ref[...])
        a_ref[z] = jax.nn.sigmoid(a0_ref[z:z + 1, :] + a_lora).astype(a_ref.dtype)

    g = _dot(jax.nn.sigmoid(_dot(mix(5), g1_ref[...])).astype(bf16), g2_ref[...])
    g_ref[...] = g.astype(g_ref.dtype)


def _rwkv_proj(x, mod, p, lay, tm=256):
    n, d = x.shape
    assert lay.t_ctx % tm == 0 and lay.t_lat % tm == 0
    nb8 = n // 8
    wrkv = p['wrkv'].astype(bf16)
    w1 = jnp.concatenate([p['w1'][0], p['w1'][1]], axis=1).astype(bf16)
    w2 = p['w2'].reshape(2 * p['w2'].shape[1], d).astype(bf16)
    a1 = jnp.concatenate([p['a1'][0], p['a1'][1]], axis=1).astype(bf16)
    a2 = p['a2'].reshape(2 * p['a2'].shape[1], d).astype(bf16)
    g1 = p['g1'].astype(bf16)
    g2 = p['g2'].astype(bf16)
    row_spec = pl.BlockSpec((tm, d), lambda i: (i, 0))
    z_spec = pl.BlockSpec((2, tm, d), lambda i: (0, i, 0))
    args = (x, x, x, mod, mod, p['mu'], wrkv, w1, w2, p['w0'], a1, a2, p['a0'], g1, g2)
    return pl.pallas_call(
        functools.partial(_rwkv_proj_kernel, lay=lay, tm=tm),
        grid=(n // tm,),
        in_specs=[
            row_spec,
            pl.BlockSpec((8, d), lambda i: (jnp.maximum(i * (tm // 8) - 1, 0), 0)),
            pl.BlockSpec((8, d), lambda i: (jnp.minimum((i + 1) * (tm // 8), nb8 - 1), 0)),
            _mod_spec(0, tm, lay),
            _mod_spec(1, tm, lay),
        ] + [_resident(a.shape) for a in args[5:]],
        out_specs=[row_spec] * 4 + [z_spec] * 2,
        out_shape=[jax.ShapeDtypeStruct((n, d), bf16)] * 4
        + [jax.ShapeDtypeStruct((2, n, d), bf16), jax.ShapeDtypeStruct((2, n, d), f32)],
        compiler_params=_params(("parallel",)),
        name="rwkv_proj",
    )(*args)


def _wkv_prep(r, k, v, a, lw, kk_gain, ka, forward):
    c_len = r.shape[0]
    r, k, v, a = (z.astype(f32) for z in (r, k, v, a))
    kkr = k * kk_gain
    kk = kkr * lax.rsqrt(jnp.maximum(_head_sum(kkr * kkr), 1e-24))
    beta = kk * a
    kd = k * (1.0 + (a - 1.0) * ka)
    row = lax.broadcasted_iota(jnp.int32, lw.shape, 0)
    cum = lw
    step = 1
    while step < c_len:
        if forward:
            cum = cum + jnp.where(row >= step, pltpu.roll(cum, step, 0), 0.0)
        else:
            cum = cum + jnp.where(row < c_len - step, pltpu.roll(cum, c_len - step, 0), 0.0)
        step *= 2
    total = cum[c_len - 1:c_len, :] if forward else cum[0:1, :]
    inv = jnp.exp(-cum)
    rem = jnp.exp(total - cum)
    return dict(a_t=-kk * jnp.exp(cum - lw), r_t=r * jnp.exp(cum), b_inv=beta * inv, k_inv=kd * inv,
                b_rem=beta * rem, k_rem=kd * rem, p_total=jnp.exp(total), v=v)


def _wkv_kernel(*refs, has_init):
    dir_refs = (refs[0:5], refs[5:10])
    kk_ref, ka_ref = refs[10:12]
    n_in = 13 if has_init else 12
    yf_ref, yb_ref, sout_ref, s_scr = refs[n_in:]
    n_par = yf_ref.shape[0]
    c_len = WKV_CHUNK
    gw = WKV_LANES
    hpg = gw // HEAD
    n_grp = yf_ref.shape[2] // gw
    c = pl.program_id(1)

    @pl.when(c == 0)
    def _():
        s_scr[...] = jnp.zeros_like(s_scr)
        if has_init:
            s0_ref = refs[n_in - 1]
            for p in range(n_par):
                for dr in range(2):
                    for h in range(n_grp * hpg):
                        o = (h % hpg) * HEAD
                        s_scr[p, dr, h // hpg, o:o + HEAD, o:o + HEAD] = s0_ref[p, dr, h]

    t_s = lax.broadcasted_iota(jnp.int32, (c_len, gw), 0)
    i_s = lax.broadcasted_iota(jnp.int32, (c_len, gw), 1) & (c_len - 1)
    strict = (i_s < t_s, i_s > t_s)
    incl = (i_s <= t_s, i_s >= t_s)
    eye = jnp.where(t_s == i_s, 1.0, 0.0)
    pair_block = (t_s >> 1) == (i_s >> 1)
    levels = []
    for lb in range(1, int(math.log2(c_len))):
        levels.append(((t_s >> (lb + 1)) == (i_s >> (lb + 1))) & ((t_s >> lb) != (i_s >> lb)))
    rb = lax.broadcasted_iota(jnp.int32, (hpg * c_len, gw), 0) // c_len
    cb = lax.broadcasted_iota(jnp.int32, (hpg * c_len, gw), 1) // HEAD
    same_head = rb == cb

    def bd(x):
        return jnp.where(same_head, jnp.concatenate([x] * hpg, axis=0), 0.0).astype(bf16)

    chains = []
    for p in range(n_par):
        for dr in range(2):
            op = _wkv_prep(*(ref[p] for ref in dir_refs[dr]), kk_ref[...], ka_ref[...], forward=(dr == 0))
            for pr in range(n_grp):
                chains.append((op, dr, (p, dr, pr), slice(pr * gw, (pr + 1) * gw),
                               (yf_ref if dr == 0 else yb_ref).at[p]))
    ids = range(len(chains))
    lhs = [jnp.concatenate([op['a_t'][:, cs], op['r_t'][:, cs]], axis=0).astype(bf16)
           for op, dr, si, cs, y_ref in chains]
    s0 = [s_scr[si] for op, dr, si, cs, y_ref in chains]
    sc = [_dot_nt(lhs[i], jnp.concatenate([bd(op['b_inv'][:, cs]), bd(op['k_inv'][:, cs]),
                                           s0[i].astype(bf16)], axis=0))
          for i, (op, dr, si, cs, y_ref) in enumerate(chains)]
    sc_b = [s[:, :gw] for s in sc]
    sc_k = [s[:, gw:2 * gw] for s in sc]
    x0 = [s[:, 2 * gw:] for s in sc]
    n_mat = [jnp.where(strict[dr], sc_b[i][:c_len], 0.0) for i, (op, dr, si, cs, y_ref) in enumerate(chains)]
    v_bd = [bd(op['v'][:, cs]) for op, dr, si, cs, y_ref in chains]
    x = [x0[i][:c_len] + _dot(jnp.where(strict[dr], sc_k[i][:c_len], 0.0).astype(bf16), v_bd[i])
         for i, (op, dr, si, cs, y_ref) in enumerate(chains)]
    t_inv = [eye + jnp.where(pair_block, n_mat[i], 0.0) for i in ids]
    for off_diag in levels:
        half = [_dot(t_inv[i].astype(bf16), bd(jnp.where(off_diag, n_mat[i], 0.0))) for i in ids]
        t_inv = [t_inv[i] + _dot(half[i].astype(bf16), bd(t_inv[i])) for i in ids]
    uv_bd = [jnp.concatenate([bd(_dot(t_inv[i].astype(bf16), bd(x[i]))), v_bd[i]], axis=0) for i in ids]
    for i, (op, dr, si, cs, y_ref) in enumerate(chains):
        rbk = jnp.concatenate([jnp.where(incl[dr], sc_b[i][c_len:], 0.0),
                               jnp.where(incl[dr], sc_k[i][c_len:], 0.0)], axis=1).astype(bf16)
        y_ref[:, cs] = (x0[i][c_len:] + _dot(rbk, uv_bd[i])).astype(y_ref.dtype)
    for i, (op, dr, si, cs, y_ref) in enumerate(chains):
        bk_rem = jnp.concatenate([bd(op['b_rem'][:, cs]), bd(op['k_rem'][:, cs])], axis=0)
        s_scr[si] = s0[i] * op['p_total'][:, cs] + _dot_tn(uv_bd[i], bk_rem)

    @pl.when(c == pl.num_programs(1) - 1)
    def _():
        for p in range(n_par):
            for dr in range(2):
                for h in range(n_grp * hpg):
                    o = (h % hpg) * HEAD
                    sout_ref[p, dr, h] = s_scr[p, dr, h // hpg, o:o + HEAD, o:o + HEAD]


def _wkv_scan(r, k, v, a2, lw2, p, s0, row0, n_seq, t_len, n_par=1):
    n, d = r.shape
    c_len = WKV_CHUNK
    nc = t_len // c_len
    n_grp = d // WKV_LANES
    has_init = s0 is not None
    if n_seq % n_par or (row0 // t_len) % n_par:
        n_par = 1
    assert n % t_len == 0 and row0 % t_len == 0
    u0 = row0 // t_len // n_par

    tok_f = pl.BlockSpec((n_par, c_len, d), lambda s, c: (u0 + s, c, 0))
    tok_b = pl.BlockSpec((n_par, c_len, d), lambda s, c: (u0 + s, nc - 1 - c, 0))
    z_f = pl.BlockSpec((None, n_par, c_len, d), lambda s, c: (0, u0 + s, c, 0))
    z_b = pl.BlockSpec((None, n_par, c_len, d), lambda s, c: (1, u0 + s, nc - 1 - c, 0))
    vec = pl.BlockSpec((1, d), lambda s, c: (0, 0))
    st = pl.BlockSpec((n_par, 2, RWKV_HEADS, HEAD, HEAD), lambda s, c: (s, 0, 0, 0, 0))
    r3, k3, v3 = (z.reshape(n // t_len, t_len, d) for z in (r, k, v))
    a4, lw4 = (z.reshape(2, n // t_len, t_len, d) for z in (a2, lw2))
    in_specs = [tok_f, tok_f, tok_f, z_f, z_f, tok_b, tok_b, tok_b, z_b, z_b, vec, vec]
    args = [r3, k3, v3, a4, lw4, r3, k3, v3, a4, lw4, p['k_k'].reshape(1, d), p['k_a'].reshape(1, d)]
    if has_init:
        in_specs.append(st)
        args.append(s0)
    y_shape = jax.ShapeDtypeStruct((n_seq, t_len, d), bf16)
    y_f, y_b, s_out = pl.pallas_call(
        functools.partial(_wkv_kernel, has_init=has_init),
        grid=(n_seq // n_par, nc),
        in_specs=in_specs,
        out_specs=[
            pl.BlockSpec((n_par, c_len, d), lambda s, c: (s, c, 0)),
            pl.BlockSpec((n_par, c_len, d), lambda s, c: (s, nc - 1 - c, 0)),
            st,
        ],
        out_shape=[y_shape, y_shape, jax.ShapeDtypeStruct((n_seq, 2, RWKV_HEADS, HEAD, HEAD), f32)],
        scratch_shapes=[pltpu.VMEM((n_par, 2, n_grp, WKV_LANES, WKV_LANES), f32)],
        compiler_params=_params(("parallel", "arbitrary")),
        name="wkv_scan",
    )(*args)
    return y_f.reshape(n_seq * t_len, d), y_b.reshape(n_seq * t_len, d), s_out


def _rwkv_post_kernel(x_ref, yfc_ref, ybc_ref, yfl_ref, ybl_ref, r_ref, k_ref, v_ref, g_ref, a_ref,
                      ka_ref, rk_ref, lnw_ref, lnb_ref, gate_ref, wo_ref, o_ref, *, nb_ctx):
    i = pl.program_id(0)
    y = (_pick_rows(i, nb_ctx, yfc_ref, yfl_ref).astype(f32)
         + _pick_rows(i, nb_ctx, ybc_ref, ybl_ref).astype(f32))
    mu = _head_sum(y) * (1.0 / HEAD)
    yc = y - mu
    var = _head_sum(yc * yc) * (1.0 / HEAD)
    yn = yc * lax.rsqrt(var + RWKV_GN_EPS) * lnw_ref[...] + lnb_ref[...]
    k = k_ref[...].astype(f32)
    ka = ka_ref[...]
    a_sum = a_ref[0].astype(f32) + a_ref[1].astype(f32)
    kd_sum = k * (2.0 + (a_sum - 2.0) * ka)
    bonus = _head_sum(r_ref[...].astype(f32) * kd_sum * rk_ref[...]) * v_ref[...].astype(f32)
    o = ((yn + bonus) * g_ref[...].astype(f32)).astype(bf16)
    o_ref[...] = x_ref[...] + gate_ref[...] * _dot(o, wo_ref[...])


def _rwkv_post_residual(x, mod, y_ctx, y_lat, r, k, v, g, a2, p, wo_bf16, lay, tm=256):
    n, d = r.shape
    tok = pl.BlockSpec((tm, d), lambda i: (i, 0))
    tok_z = pl.BlockSpec((2, tm, d), lambda i: (0, i, 0))
    vec = pl.BlockSpec((1, d), lambda i: (0, 0))
    yc_spec, yl_spec = _split_row_specs((tm, d), tm, lay)
    row = lambda a: a.reshape(1, d)
    return pl.pallas_call(
        functools.partial(_rwkv_post_kernel, nb_ctx=lay.n_ctx // tm),
        grid=(n // tm,),
        in_specs=[tok, yc_spec, yc_spec, yl_spec, yl_spec, tok, tok, tok, tok, tok_z, vec, vec, vec, vec,
                  _mod_spec(2, tm, lay), _resident((d, d))],
        out_specs=tok,
        out_shape=jax.ShapeDtypeStruct((n, d), f32),
        compiler_params=_params(("parallel",)),
        name="rwkv_post",
    )(x, y_ctx[0], y_ctx[1], y_lat[0], y_lat[1], r, k, v, g, a2,
      row(p['k_a']), row(p['r_k']), row(p['ln_w']), row(p['ln_b']), mod, wo_bf16)


def _rwkv_layer(x, mod, state0, p, wo_bf16, lay):
    r, k, v, g, a2, lw2 = _rwkv_proj(x, mod, p, lay)
    yf_c, yb_c, s_ctx = _wkv_scan(r, k, v, a2, lw2, p, None, 0, lay.b_ctx, lay.t_ctx)
    yf_l, yb_l, _ = _wkv_scan(r, k, v, a2, lw2, p, state0, lay.n_ctx, lay.b_lat, lay.t_lat)
    x = _rwkv_post_residual(x, mod, (yf_c, yb_c), (yf_l, yb_l), r, k, v, g, a2, p, wo_bf16, lay)
    return x, s_ctx


def _rope_table(n_tok, width):
    n_rows = n_tok // GRID_W
    rows = jnp.broadcast_to(jnp.arange(n_rows, dtype=f32)[:, None], (n_rows, GRID_W)).reshape(-1)
    cols = jnp.broadcast_to(jnp.arange(GRID_W, dtype=f32)[None, :], (n_rows, GRID_W)).reshape(-1)
    n_freq = HEAD // 4
    inv = ROPE_BASE ** (-jnp.arange(n_freq, dtype=f32) / n_freq)
    ang = jnp.concatenate([rows[:, None] * inv, cols[:, None] * inv], axis=-1)
    reps = width // (HEAD // 2)
    return jnp.tile(jnp.cos(ang), (1, reps)), jnp.tile(jnp.sin(ang), (1, reps))


def kernel(x_prompt, x_sample, cache_da_k, cache_da_v, state_rwkv, cache_swa_k, cache_swa_v, c, c_ctx, ada_w, ada_b, mlp_w1, mlp_w2, da_wqkv, da_q_norm, da_k_norm, da_lambda, da_subln, da_wo, rwkv_mu, rwkv_wrkv, rwkv_w0, rwkv_w1, rwkv_w2, rwkv_a0, rwkv_a1, rwkv_a2, rwkv_g1, rwkv_g2, rwkv_k_k, rwkv_k_a, rwkv_r_k, rwkv_ln_w, rwkv_ln_b, rwkv_wo, swa_wqkv, swa_q_norm, swa_k_norm, swa_sink, swa_wo):
    b_ctx, t_ctx, d = x_prompt.shape
    b_lat, t_lat, _ = x_sample.shape
    lay = Layout(b_ctx, t_ctx, b_lat, t_lat)
    depth = ada_w.shape[0]
    tm = 512
    assert d == D_MODEL and lay.n_ctx % t_lat == 0

    xs = (x_prompt.reshape(lay.n_ctx, d), x_sample.reshape(lay.n_lat, d))
    cond = jnp.concatenate([c_ctx[None, :], c, jnp.zeros((8 - 1 - b_lat, d), f32)], axis=0)
    mods = _modulation(cond, ada_w, ada_b)
    mods = mods[:, :1 + b_lat].reshape(depth, 1 + b_lat, 6, 1, d)

    rope_da = _rope_table(t_lat, LANES)
    rope_swa = _rope_table(t_lat, d)

    mlp_w1_b, mlp_w2_b = mlp_w1.astype(bf16), mlp_w2.astype(bf16)
    da_wqkv_b, da_wo_b = da_wqkv.astype(bf16), da_wo.astype(bf16)
    n_da = da_wqkv.shape[0]
    da_kv = None
    rwkv_s, swa_k, swa_v = [], [], []
    for i in range(depth):
        kind = i % N_MIXERS
        j = i // N_MIXERS
        mod = mods[i]
        w1 = LayerWeight(mlp_w1_b, i)
        w2 = LayerWeight(mlp_w2_b, i)
        if kind == 1:
            p = {'mu': rwkv_mu[j], 'wrkv': rwkv_wrkv[j], 'w0': rwkv_w0[j], 'w1': rwkv_w1[j],
                 'w2': rwkv_w2[j], 'a0': rwkv_a0[j], 'a1': rwkv_a1[j], 'a2': rwkv_a2[j],
                 'g1': rwkv_g1[j], 'g2': rwkv_g2[j], 'k_k': rwkv_k_k[j], 'k_a': rwkv_k_a[j],
                 'r_k': rwkv_r_k[j], 'ln_w': rwkv_ln_w[j], 'ln_b': rwkv_ln_b[j]}
            x = xs[0] if len(xs) == 1 else jnp.concatenate(xs, axis=0)
            x, s_new = _rwkv_layer(x, mod, state_rwkv[:, j], p, rwkv_wo[j].astype(bf16), lay)
            rwkv_s.append(s_new)
            xs = (_mlp(x, mod, w1, w2, lay),)
            continue
        if kind == 0:
            p = {'q_norm': da_q_norm[j], 'k_norm': da_k_norm[j], 'lam': da_lambda[j], 'subln': da_subln[j]}
            lam_init = 0.8 - 0.6 * math.exp(-0.3 * i)
            q, kv = _ln_matmul(xs, mod, LayerWeight(da_wqkv_b, j), lay)
            o_ctx, o_lat, *da_kv = _da_mixer(q, kv, cache_da_k, cache_da_v, j, n_da, da_kv, p, lam_init,
                                             rope_da, lay)
            wo = LayerWeight(da_wo_b, j)
        else:
            p = {'q_norm': swa_q_norm[j], 'k_norm': swa_k_norm[j], 'sink': swa_sink[j]}
            wqkv, wo = _swa_group_major_weights(swa_wqkv[j], swa_wo[j])
            q, kv = _ln_matmul(xs, mod, wqkv, lay)
            o_ctx, o_lat, k_new, v_new = _swa_mixer(q, kv, cache_swa_k, cache_swa_v, j, p, rope_swa, lay)
            swa_k.append(k_new.reshape(b_ctx, t_ctx, SWA_KV_HEADS, HEAD))
            swa_v.append(v_new.reshape(b_ctx, t_ctx, SWA_KV_HEADS, HEAD))
        if i == depth - 1 and len(xs) == 1:
            nb_ctx = lay.n_ctx // tm
            xs = (_out_proj_mlp(xs, (o_ctx,), mod, wo, w1, w2, lay, rows=(0, nb_ctx), tm=tm),
                  _out_proj_mlp(xs, (o_lat,), mod, wo, w1, w2, lay, rows=(nb_ctx, lay.n_lat // tm), tm=tm))
        else:
            xs = (_out_proj_mlp(xs, (o_ctx, o_lat), mod, wo, w1, w2, lay, tm=tm),)

    if len(xs) == 1:
        xs = (xs[0][:lay.n_ctx], xs[0][lay.n_ctx:])
    y_prompt = xs[0].reshape(b_ctx, t_ctx, d)
    y_sample = xs[1].reshape(b_lat, t_lat, d)
    new_da_k, new_da_v = (a.reshape(b_ctx, n_da, t_ctx, DA_HEADS, LANES) for a in da_kv)
    return (y_prompt, y_sample, new_da_k, new_da_v,
            jnp.stack(rwkv_s, axis=1), jnp.stack(swa_k, axis=1), jnp.stack(swa_v, axis=1))
```

```python
import functools
import math
from typing import NamedTuple

import jax
import jax.numpy as jnp
from jax import lax
from jax.experimental import pallas as pl
from jax.experimental.pallas import tpu as pltpu

f32 = jnp.float32
bf16 = jnp.bfloat16

D_MODEL = 1024
N_MIXERS = 3
GRID_W = 64
ROPE_BASE = 10000.0
NORM_EPS = 1e-6
NEG_INF = -1e30
LOG2E = math.log2(math.e)
HEAD = 64
LANES = 128
DA_HEADS = D_MODEL // LANES
RWKV_HEADS = D_MODEL // HEAD
RWKV_GN_EPS = 64e-5
SWA_HEADS = D_MODEL // HEAD
SWA_KV_HEADS = SWA_HEADS // 4
SWA_GROUP = SWA_HEADS // SWA_KV_HEADS
SWA_KV = SWA_KV_HEADS * HEAD
WINDOW = 128
WKV_CHUNK = 64
WKV_LANES = 128
VMEM_LIMIT = 56 * 1024 * 1024
QK_SCALE = HEAD ** -0.5 * LOG2E


class Layout(NamedTuple):
    b_ctx: int
    t_ctx: int
    b_lat: int
    t_lat: int

    @property
    def n_ctx(self):
        return self.b_ctx * self.t_ctx

    @property
    def n_lat(self):
        return self.b_lat * self.t_lat

    @property
    def n(self):
        return self.n_ctx + self.n_lat


def _params(sem):
    return pltpu.CompilerParams(dimension_semantics=sem, vmem_limit_bytes=VMEM_LIMIT)


def _group_of_tile(i, tm, lay):
    r = i * tm
    return jnp.where(r < lay.n_ctx, 0, 1 + (r - lay.n_ctx) // lay.t_lat)


def _mod_spec(chunk, tm, lay, tile0=0):
    return pl.BlockSpec((None, None, 1, D_MODEL),
                        lambda i, *_: (_group_of_tile(i + tile0, tm, lay), chunk, 0, 0))


def _resident(shape):
    return pl.BlockSpec(shape, lambda *_: (0,) * len(shape), pipeline_mode=pl.Buffered(1))


class LayerWeight(NamedTuple):
    stack: jax.Array
    layer: int

    @property
    def shape(self):
        return self.stack.shape[1:]

    def spec(self):
        idx = (self.layer,) + (0,) * (self.stack.ndim - 1)
        return pl.BlockSpec((None,) + self.shape, lambda *_: idx, pipeline_mode=pl.Buffered(1))


def _weight_spec(w):
    return w.spec() if isinstance(w, LayerWeight) else _resident(w.shape)


def _weight_arg(w):
    return w.stack if isinstance(w, LayerWeight) else w


def _split_row_specs(block, tm, lay):
    nb_ctx = lay.n_ctx // tm
    lead = (0,) * (len(block) - 2)
    ctx = pl.BlockSpec(block, lambda i, *_: lead + (jnp.minimum(i, nb_ctx - 1), 0))
    lat = pl.BlockSpec(block, lambda i, *_: lead + (jnp.maximum(i - nb_ctx, 0), 0))
    return ctx, lat


def _dot(a, b):
    return jnp.dot(a, b, preferred_element_type=f32)


def _dot_nt(a, b):
    return lax.dot_general(a, b, (((1,), (1,)), ((), ())), preferred_element_type=f32)


def _dot_tn(a, b):
    return lax.dot_general(a, b, (((0,), (0,)), ((), ())), preferred_element_type=f32)


def _split2(x):
    hi = x.astype(bf16)
    return hi, (x - hi.astype(f32)).astype(bf16)


def _head_ones():
    r = lax.broadcasted_iota(jnp.int32, (LANES, LANES), 0)
    c = lax.broadcasted_iota(jnp.int32, (LANES, LANES), 1)
    return ((r // HEAD) == (c // HEAD)).astype(bf16)


def _head_sum(x):
    g = _head_ones()
    xb = x.astype(bf16)
    cols = [_dot(xb[:, p * LANES:(p + 1) * LANES], g) for p in range(x.shape[1] // LANES)]
    return cols[0] if len(cols) == 1 else jnp.concatenate(cols, axis=1)


def _head_rms(x, gain):
    ms = _head_sum(x * x) * (1.0 / HEAD)
    return x * lax.rsqrt(ms + NORM_EPS) * gain


def _adaln(x, shift, scale):
    ms = jnp.mean(x * x, axis=-1, keepdims=True)
    return x * lax.rsqrt(ms + NORM_EPS) * (1.0 + scale) + shift


def _rope(x, cos, sin):
    w = x.shape[1]
    lane = lax.broadcasted_iota(jnp.int32, x.shape, 1)
    low = (lane & (HEAD - 1)) < HEAD // 2
    up = pltpu.roll(x, w - HEAD // 2, 1)
    dn = pltpu.roll(x, HEAD // 2, 1)
    return x * cos + jnp.where(low, -up, dn) * sin


def _pick_rows(i, nb_ctx, ctx_ref, lat_ref):
    return jnp.where(i < nb_ctx, ctx_ref[...], lat_ref[...])


def _mod_kernel(c_ref, w_ref, b_ref, o_ref):
    c = c_ref[...]
    s = c * jax.nn.sigmoid(c)
    sh, sl = _split2(s)
    wh, wl = _split2(w_ref[...])
    o_ref[...] = _dot(sh, wh) + (_dot(sh, wl) + _dot(sl, wh)) + b_ref[...]


def _modulation(cond8, ada_w, ada_b):
    depth, d, d6 = ada_w.shape
    tn = 1536
    return pl.pallas_call(
        _mod_kernel,
        grid=(depth, d6 // tn),
        in_specs=[
            pl.BlockSpec((8, d), lambda l, j: (0, 0)),
            pl.BlockSpec((None, d, tn), lambda l, j: (l, 0, j)),
            pl.BlockSpec((None, 1, tn), lambda l, j: (l, 0, j)),
        ],
        out_specs=pl.BlockSpec((None, 8, tn), lambda l, j: (l, 0, j)),
        out_shape=jax.ShapeDtypeStruct((depth, 8, d6), f32),
        compiler_params=_params(("parallel", "parallel")),
        name="modulation",
    )(cond8, ada_w, ada_b.reshape(depth, 1, d6))


def _ln_mm_kernel(*refs, tn, nb_ctx):
    if nb_ctx is None:
        x_ref, sh_ref, sc_ref, w_ref, q_ref, kv_ref = refs
        x = x_ref[...]
    else:
        xc_ref, xl_ref, sh_ref, sc_ref, w_ref, q_ref, kv_ref = refs
        x = _pick_rows(pl.program_id(0), nb_ctx, xc_ref, xl_ref)
    h = _adaln(x, sh_ref[...], sc_ref[...]).astype(bf16)
    n_q = q_ref.shape[1]
    for j in range(w_ref.shape[1] // tn):
        cols = slice(j * tn, (j + 1) * tn)
        part = _dot(h, w_ref[:, cols])
        if j * tn < n_q:
            q_ref[:, cols] = part.astype(q_ref.dtype)
        else:
            kv_ref[:, j * tn - n_q:(j + 1) * tn - n_q] = part


def _ln_matmul(xs, mod, w, lay, tm=512, tn=512):
    d = xs[0].shape[1]
    n_out = w.shape[1]
    assert d % tn == 0
    if len(xs) == 1:
        x_specs = [pl.BlockSpec((tm, d), lambda i: (i, 0))]
        nb_ctx = None
    else:
        x_specs = list(_split_row_specs((tm, d), tm, lay))
        nb_ctx = lay.n_ctx // tm
    return pl.pallas_call(
        functools.partial(_ln_mm_kernel, tn=tn, nb_ctx=nb_ctx),
        grid=(lay.n // tm,),
        in_specs=x_specs + [_mod_spec(0, tm, lay), _mod_spec(1, tm, lay), _weight_spec(w)],
        out_specs=[pl.BlockSpec((tm, d), lambda i: (i, 0)), pl.BlockSpec((tm, n_out - d), lambda i: (i, 0))],
        out_shape=[jax.ShapeDtypeStruct((lay.n, d), bf16), jax.ShapeDtypeStruct((lay.n, n_out - d), f32)],
        compiler_params=_params(("parallel",)),
        name="adaln_proj",
    )(*xs, mod, mod, _weight_arg(w))


def _mlp_body(x, sh_ref, sc_ref, g_ref, w1_ref, w2_ref, acc_ref, o_ref, fc):
    h = _adaln(x, sh_ref[...], sc_ref[...]).astype(bf16)
    d_ff = w1_ref.shape[1]
    for c in range(d_ff // fc):
        a = _dot(h, w1_ref[:, c * fc:(c + 1) * fc])
        a = jnp.square(jnp.maximum(a, 0.0)).astype(bf16)
        part = _dot(a, w2_ref[c * fc:(c + 1) * fc, :])
        if c == 0:
            acc_ref[...] = part
        else:
            acc_ref[...] += part
    o_ref[...] = x + g_ref[...] * acc_ref[...]


def _mlp_kernel(x_ref, sh_ref, sc_ref, g_ref, w1_ref, w2_ref, o_ref, acc_ref, *, fc):
    _mlp_body(x_ref[...], sh_ref, sc_ref, g_ref, w1_ref, w2_ref, acc_ref, o_ref, fc)


def _mlp(x, mod, w1, w2, lay, tm=512, fc=1024):
    n, d = x.shape
    return pl.pallas_call(
        functools.partial(_mlp_kernel, fc=fc),
        grid=(n // tm,),
        in_specs=[
            pl.BlockSpec((tm, d), lambda i: (i, 0)),
            _mod_spec(3, tm, lay), _mod_spec(4, tm, lay), _mod_spec(5, tm, lay),
            _weight_spec(w1), _weight_spec(w2),
        ],
        out_specs=pl.BlockSpec((tm, d), lambda i: (i, 0)),
        out_shape=jax.ShapeDtypeStruct((n, d), f32),
        scratch_shapes=[pltpu.VMEM((tm, d), f32)],
        compiler_params=_params(("parallel",)),
        name="mlp",
    )(x, mod, mod, mod, _weight_arg(w1), _weight_arg(w2))


def _out_mlp_kernel(*refs, fc, nb_ctx, split_x, split_a):
    refs = list(refs)
    i = pl.program_id(0)
    x = _pick_rows(i, nb_ctx, refs.pop(0), refs.pop(0)) if split_x else refs.pop(0)[...]
    a = _pick_rows(i, nb_ctx, refs.pop(0), refs.pop(0)) if split_a else refs.pop(0)[...]
    gm_ref, wo_ref, sh_ref, sc_ref, gf_ref, w1_ref, w2_ref, o_ref, acc_ref = refs
    x1 = x + gm_ref[...] * _dot(a.astype(bf16), wo_ref[...])
    _mlp_body(x1, sh_ref, sc_ref, gf_ref, w1_ref, w2_ref, acc_ref, o_ref, fc)


def _out_proj_mlp(xs, a_parts, mod, wo, w1, w2, lay, rows=None, tm=512, fc=1024):
    d = D_MODEL
    k_in = a_parts[0].shape[1]
    tile0, n_tiles = rows if rows is not None else (0, lay.n // tm)
    split_x, split_a = len(xs) == 2, len(a_parts) == 2
    assert rows is None or not (split_x or split_a)
    x_specs = (list(_split_row_specs((tm, d), tm, lay)) if split_x
               else [pl.BlockSpec((tm, d), lambda i: (i + tile0, 0))])
    a_specs = (list(_split_row_specs((tm, k_in), tm, lay)) if split_a
               else [pl.BlockSpec((tm, k_in), lambda i: (i, 0))])
    return pl.pallas_call(
        functools.partial(_out_mlp_kernel, fc=fc, nb_ctx=lay.n_ctx // tm, split_x=split_x, split_a=split_a),
        grid=(n_tiles,),
        in_specs=x_specs + a_specs + [
            _mod_spec(2, tm, lay, tile0), _weight_spec(wo),
            _mod_spec(3, tm, lay, tile0), _mod_spec(4, tm, lay, tile0), _mod_spec(5, tm, lay, tile0),
            _weight_spec(w1), _weight_spec(w2),
        ],
        out_specs=pl.BlockSpec((tm, d), lambda i: (i, 0)),
        out_shape=jax.ShapeDtypeStruct((n_tiles * tm, d), f32),
        scratch_shapes=[pltpu.VMEM((tm, d), f32)],
        compiler_params=_params(("parallel",)),
        name="out_proj_mlp",
    )(*xs, *a_parts, mod, _weight_arg(wo), mod, mod, mod, _weight_arg(w1), _weight_arg(w2))


def _da_lambda(lam_ref, lam_init):
    lp = lam_ref[...]
    e1 = jnp.exp(jnp.sum(lp[0:1] * lp[1:2], axis=-1, keepdims=True))
    e2 = jnp.exp(jnp.sum(lp[2:3] * lp[3:4], axis=-1, keepdims=True))
    return e1 - e2 + lam_init


def _da_qk(q, k_bf16):
    lane = lax.broadcasted_iota(jnp.int32, q.shape, 1)
    maps = [jnp.where((lane < HEAD) == (m == 0), q, 0.0).astype(bf16) for m in range(2)]
    return _dot_nt(jnp.concatenate(maps, axis=0), k_bf16)


def _with_ones(v_bf16):
    return jnp.concatenate([v_bf16, jnp.ones_like(v_bf16)], axis=1)


def _da_pv(scores, v_ones, lam):
    t = scores.shape[0] // 2
    e = jnp.exp2(scores - jnp.max(scores, axis=-1, keepdims=True))
    pv = _dot(e.astype(bf16), v_ones)
    o = pv[:, :LANES] * (1.0 / pv[:, LANES:])
    return o[:t] - lam * o[t:]


def _da_finish(o, sub_ref, lam_init):
    ms = jnp.mean(o * o, axis=-1, keepdims=True)
    return o * lax.rsqrt(ms + NORM_EPS) * sub_ref[...] * (1.0 - lam_init)


def _da_ctx_kernel(*refs, lam_init, layer):
    lam_ref, q_ref, k_ref, v_ref, qn_ref, kn_ref, sub_ref = refs[:7]
    o_ref, knew_ref, vnew_ref = refs[-3:]
    lam = _da_lambda(lam_ref, lam_init)
    q = _head_rms(q_ref[...].astype(f32), qn_ref[...]) * QK_SCALE
    k = _head_rms(k_ref[...], kn_ref[...])
    v = v_ref[...]
    if layer == 0:
        for l in range(knew_ref.shape[0]):
            knew_ref[l] = k if l == 0 else jnp.zeros_like(k)
            vnew_ref[l] = v if l == 0 else jnp.zeros_like(v)
    else:
        knew_ref[...] = k
        vnew_ref[...] = v
    kb = k.astype(bf16)
    vb = v.astype(bf16)
    cols = [slice(h * LANES, (h + 1) * LANES) for h in range(q.shape[1] // LANES)]
    s_next = _da_qk(q[:, cols[0]], kb[:, cols[0]])
    for h, cs in enumerate(cols):
        s_cur = s_next
        if h + 1 < len(cols):
            s_next = _da_qk(q[:, cols[h + 1]], kb[:, cols[h + 1]])
        o = _da_finish(_da_pv(s_cur, _with_ones(vb[:, cs]), lam), sub_ref, lam_init)
        o_ref[:, cs] = o.astype(o_ref.dtype)


def _da_lat_kernel(lam_ref, q_ref, k_ref, v_ref, ck_ref, cv_ref, qn_ref, kn_ref, sub_ref,
                   cq_ref, sq_ref, ckk_ref, skk_ref, o_ref, kf_scr, vf_scr, *, lam_init, past):
    cols = [slice(h * LANES, (h + 1) * LANES) for h in range(q_ref.shape[1] // LANES)]

    @pl.when(pl.program_id(2) == 0)
    def _():
        k = _rope(_head_rms(k_ref[...], kn_ref[...]), ckk_ref[...], skk_ref[...])
        kf_scr[0:past, :] = ck_ref[...].astype(bf16)
        kf_scr[past:, :] = k.astype(bf16)
        for h, cs in enumerate(cols):
            vf_scr[h, 0:past, :] = _with_ones(cv_ref[:, cs].astype(bf16))
            vf_scr[h, past:, :] = _with_ones(v_ref[:, cs].astype(bf16))

    lam = _da_lambda(lam_ref, lam_init)
    q = _rope(_head_rms(q_ref[...].astype(f32), qn_ref[...]), cq_ref[...], sq_ref[...]) * QK_SCALE
    s_next = _da_qk(q[:, cols[0]], kf_scr[:, cols[0]])
    for h, cs in enumerate(cols):
        s_cur = s_next
        if h + 1 < len(cols):
            s_next = _da_qk(q[:, cols[h + 1]], kf_scr[:, cols[h + 1]])
        o_ref[:, cs] = _da_finish(_da_pv(s_cur, vf_scr[h], lam), sub_ref, lam_init).astype(o_ref.dtype)


def _da_mixer(q, kv, cache_k, cache_v, j, n_layers, kv_prev, p, lam_init, rope_tab, lay, tq=512):
    d = D_MODEL
    nh = DA_HEADS
    assert (kv_prev is None) == (j == 0)
    qn = jnp.tile(p['q_norm'], 2).reshape(1, LANES)
    kn = jnp.tile(p['k_norm'], 2).reshape(1, LANES)
    qn_all = jnp.tile(qn, (1, nh))
    kn_all = jnp.tile(kn, (1, nh))
    sub = p['subln'].reshape(1, LANES)
    small = lambda shape: pl.BlockSpec(shape, lambda *_: (0,) * len(shape))
    t = lay.t_ctx
    ctx_rows = pl.BlockSpec((t, d), lambda b: (b, 0))
    kv_shape = jax.ShapeDtypeStruct((lay.b_ctx, n_layers, t, d), f32)
    if kv_prev is None:
        kv_spec = pl.BlockSpec((None, n_layers, t, d), lambda b: (b, 0, 0, 0))
        kv_in_specs, kv_args, aliases = [], (), {}
    else:
        kv_spec = pl.BlockSpec((None, None, t, d), lambda b: (b, j, 0, 0))
        kv_in_specs = [pl.BlockSpec(memory_space=pl.ANY)] * 2
        kv_args, aliases = tuple(kv_prev), {7: 1, 8: 2}
    o_ctx, k_new, v_new = pl.pallas_call(
        functools.partial(_da_ctx_kernel, lam_init=lam_init, layer=j),
        grid=(lay.b_ctx,),
        in_specs=[
            small((4, HEAD)),
            ctx_rows,
            ctx_rows,
            pl.BlockSpec((t, d), lambda b: (b, 1)),
            small((1, d)), small((1, d)), small((1, LANES)),
        ] + kv_in_specs,
        out_specs=[ctx_rows, kv_spec, kv_spec],
        out_shape=[jax.ShapeDtypeStruct((lay.n_ctx, d), bf16), kv_shape, kv_shape],
        input_output_aliases=aliases,
        compiler_params=_params(("parallel",)),
        name="da_ctx",
    )(p['lam'], q, kv, kv, qn_all, kn_all, sub, *kv_args)

    tl = lay.t_lat
    past = cache_k.shape[2]
    row0 = lay.n_ctx // tl
    q0 = lay.n_ctx // tq
    nq = tl // tq
    cos, sin = rope_tab
    ck = cache_k.reshape(lay.b_lat, cache_k.shape[1], past, d)
    cv = cache_v.reshape(lay.b_lat, cache_v.shape[1], past, d)
    hg = 2
    gw = hg * LANES
    ng = nh // hg
    cos_g, sin_g = jnp.tile(cos, (1, hg)), jnp.tile(sin, (1, hg))
    o_lat = pl.pallas_call(
        functools.partial(_da_lat_kernel, lam_init=lam_init, past=past),
        grid=(lay.b_lat, ng, nq),
        in_specs=[
            small((4, HEAD)),
            pl.BlockSpec((tq, gw), lambda b, h, i: (q0 + b * nq + i, h)),
            pl.BlockSpec((tl, gw), lambda b, h, i: (row0 + b, h)),
            pl.BlockSpec((tl, gw), lambda b, h, i: (row0 + b, ng + h)),
            pl.BlockSpec((None, None, past, gw), lambda b, h, i: (b, j, 0, h)),
            pl.BlockSpec((None, None, past, gw), lambda b, h, i: (b, j, 0, h)),
            small((1, gw)), small((1, gw)), small((1, LANES)),
            pl.BlockSpec((tq, gw), lambda b, h, i: (i, 0)),
            pl.BlockSpec((tq, gw), lambda b, h, i: (i, 0)),
            _resident((tl, gw)), _resident((tl, gw)),
        ],
        out_specs=pl.BlockSpec((tq, gw), lambda b, h, i: (b * nq + i, h)),
        out_shape=jax.ShapeDtypeStruct((lay.n_lat, d), bf16),
        scratch_shapes=[pltpu.VMEM((past + tl, gw), bf16), pltpu.VMEM((hg, past + tl, 2 * LANES), bf16)],
        compiler_params=_params(("parallel", "parallel", "arbitrary")),
        name="da_lat",
    )(p['lam'], q, kv, kv, ck, cv, qn_all[:, :gw], kn_all[:, :gw], sub, cos_g, sin_g, cos_g, sin_g)
    return o_ctx, o_lat, k_new, v_new


def _swa_head_mask(shape, j):
    lane = lax.broadcasted_iota(jnp.int32, shape, 1)
    return (lane >= j * HEAD) & (lane < (j + 1) * HEAD)


def _swa_heads(q, sink_ref, score_fn, mask_fn, pv_fn, o_ref):
    t = q.shape[0]

    def scores(g):
        qg = q[:, g * SWA_KV:(g + 1) * SWA_KV]
        rows = [jnp.where(_swa_head_mask(qg.shape, j), qg, 0.0).astype(bf16) for j in range(SWA_KV_HEADS)]
        return score_fn(jnp.concatenate(rows, axis=0))

    s_next = scores(0)
    for g in range(SWA_GROUP):
        s_grp = s_next
        if g + 1 < SWA_GROUP:
            s_next = scores(g + 1)
        o = None
        for j in range(SWA_KV_HEADS):
            s_cur = mask_fn([s[j * t:(j + 1) * t] for s in s_grp])
            sink = sink_ref[j * SWA_GROUP + g] * LOG2E
            m = sink
            for s in s_cur:
                m = jnp.maximum(jnp.max(s, axis=-1, keepdims=True), m)
            exps = [jnp.exp2(s - m) for s in s_cur]
            den = jnp.exp2(sink - m)
            for e in exps:
                den = den + jnp.sum(e, axis=-1, keepdims=True)
            oj = pv_fn(j, [e.astype(bf16) for e in exps]) * (1.0 / den)
            o = oj if j == 0 else o + oj
        o_ref[:, g * SWA_KV:(g + 1) * SWA_KV] = o.astype(o_ref.dtype)


def _swa_ctx_kernel(sink_ref, q_ref, k_ref, v_ref, qn_ref, kn_ref, o_ref, knew_ref, vnew_ref):
    q = _head_rms(q_ref[...].astype(f32), qn_ref[...]) * QK_SCALE
    k = _head_rms(k_ref[...], kn_ref[...])
    v = v_ref[...]
    knew_ref[...] = k
    vnew_ref[...] = v
    kb = k.astype(bf16)
    v_heads = [jnp.where(_swa_head_mask(v.shape, j), v, 0.0).astype(bf16) for j in range(SWA_KV_HEADS)]
    _swa_heads(q, sink_ref, lambda qm: [_dot_nt(qm, kb)], lambda s: s,
               lambda j, e: _dot(e[0], v_heads[j]), o_ref)


def _swa_lat_kernel(sink_ref, q_ref, k_ref, v_ref, ck_ref, cv_ref, qn_ref, kn_ref,
                    cq_ref, sq_ref, ckk_ref, skk_ref, o_ref, kp_scr, vp_scr, *, t_lat):
    n = pl.program_id(1)
    w = WINDOW

    @pl.when(n == 0)
    def _():
        zeros = jnp.zeros((w, SWA_KV), bf16)
        k = _rope(_head_rms(k_ref[...], kn_ref[...]), ckk_ref[...], skk_ref[...])
        kp_scr[0:w, :] = zeros
        kp_scr[w:w + t_lat, :] = k.astype(bf16)
        kp_scr[w + t_lat:, :] = zeros
        vp_scr[0:w, :] = zeros
        vp_scr[w:w + t_lat, :] = v_ref[...].astype(bf16)
        vp_scr[w + t_lat:, :] = zeros

    q = _rope(_head_rms(q_ref[...].astype(f32), qn_ref[...]), cq_ref[...], sq_ref[...]) * QK_SCALE
    start = pl.multiple_of(n * w, w)
    kband = kp_scr[pl.ds(start, 3 * w), :]
    vband = vp_scr[pl.ds(start, 3 * w), :]
    ckb = ck_ref[...].astype(bf16)
    cv = cv_ref[...]
    qi = lax.broadcasted_iota(jnp.int32, (w, 3 * w), 0)
    si = lax.broadcasted_iota(jnp.int32, (w, 3 * w), 1)
    rel = si - qi
    key_pos = (n - 1) * w + si
    valid = (rel >= 0) & (rel <= 2 * w) & (key_pos >= 0) & (key_pos < t_lat)
    cv_heads = [jnp.where(_swa_head_mask(cv.shape, j), cv, 0.0).astype(bf16) for j in range(SWA_KV_HEADS)]
    vb_heads = [jnp.where(_swa_head_mask(vband.shape, j), vband, jnp.zeros_like(vband))
                for j in range(SWA_KV_HEADS)]
    _swa_heads(q, sink_ref,
               lambda qm: [_dot_nt(qm, ckb), _dot_nt(qm, kband)],
               lambda s: [s[0], jnp.where(valid, s[1], NEG_INF)],
               lambda j, e: _dot(e[0], cv_heads[j]) + _dot(e[1], vb_heads[j]),
               o_ref)


def _swa_group_major_weights(wqkv, wo):
    d = D_MODEL
    wq = wqkv[:, :d].reshape(d, SWA_KV_HEADS, SWA_GROUP, HEAD).transpose(0, 2, 1, 3).reshape(d, d)
    wo = wo.reshape(SWA_KV_HEADS, SWA_GROUP, HEAD, d).transpose(1, 0, 2, 3).reshape(d, d)
    return jnp.concatenate([wq, wqkv[:, d:]], axis=1).astype(bf16), wo.astype(bf16)


def _swa_mixer(q, kv, cache_k, cache_v, j, p, rope_tab, lay):
    d = D_MODEL
    qn = jnp.tile(p['q_norm'], d // HEAD).reshape(1, d)
    kn = jnp.tile(p['k_norm'], SWA_KV_HEADS).reshape(1, SWA_KV)
    small = lambda shape: pl.BlockSpec(shape, lambda *_: (0,) * len(shape))
    smem = pl.BlockSpec(memory_space=pltpu.SMEM)
    t = lay.t_ctx
    kv_rows = pl.BlockSpec((t, SWA_KV), lambda b: (b, 0))
    o_ctx, k_new, v_new = pl.pallas_call(
        _swa_ctx_kernel,
        grid=(lay.b_ctx,),
        in_specs=[
            smem,
            pl.BlockSpec((t, d), lambda b: (b, 0)),
            kv_rows,
            pl.BlockSpec((t, SWA_KV), lambda b: (b, 1)),
            small((1, d)), small((1, SWA_KV)),
        ],
        out_specs=[pl.BlockSpec((t, d), lambda b: (b, 0)), kv_rows, kv_rows],
        out_shape=[jax.ShapeDtypeStruct((lay.n_ctx, d), bf16),
                   jax.ShapeDtypeStruct((lay.n_ctx, SWA_KV), f32),
                   jax.ShapeDtypeStruct((lay.n_ctx, SWA_KV), f32)],
        compiler_params=_params(("parallel",)),
        name="swa_ctx",
    )(p['sink'], q, kv, kv, qn, kn)

    tl = lay.t_lat
    w = WINDOW
    past = cache_k.shape[2]
    row0 = lay.n_ctx // tl
    q0 = lay.n_ctx // w
    nq = tl // w
    cos, sin = rope_tab
    ck = cache_k.reshape(lay.b_lat, cache_k.shape[1], past, SWA_KV)
    cv = cache_v.reshape(lay.b_lat, cache_v.shape[1], past, SWA_KV)
    o_lat = pl.pallas_call(
        functools.partial(_swa_lat_kernel, t_lat=tl),
        grid=(lay.b_lat, nq),
        in_specs=[
            smem,
            pl.BlockSpec((w, d), lambda b, i: (q0 + b * nq + i, 0)),
            pl.BlockSpec((tl, SWA_KV), lambda b, i: (row0 + b, 0)),
            pl.BlockSpec((tl, SWA_KV), lambda b, i: (row0 + b, 1)),
            pl.BlockSpec((None, None, past, SWA_KV), lambda b, i: (b, j, 0, 0)),
            pl.BlockSpec((None, None, past, SWA_KV), lambda b, i: (b, j, 0, 0)),
            small((1, d)), small((1, SWA_KV)),
            pl.BlockSpec((w, d), lambda b, i: (i, 0)),
            pl.BlockSpec((w, d), lambda b, i: (i, 0)),
            pl.BlockSpec((tl, SWA_KV), lambda b, i: (0, 0)),
            pl.BlockSpec((tl, SWA_KV), lambda b, i: (0, 0)),
        ],
        out_specs=pl.BlockSpec((w, d), lambda b, i: (b * nq + i, 0)),
        out_shape=jax.ShapeDtypeStruct((lay.n_lat, d), bf16),
        scratch_shapes=[pltpu.VMEM((tl + 2 * w, SWA_KV), bf16), pltpu.VMEM((tl + 2 * w, SWA_KV), bf16)],
        compiler_params=_params(("parallel", "arbitrary")),
        name="swa_lat",
    )(p['sink'], q, kv, kv, ck, cv, qn, kn, cos, sin, cos, sin)
    return o_ctx, o_lat, k_new, v_new


def _rwkv_proj_kernel(x_ref, xp_ref, xn_ref, sh_ref, sc_ref, mu_ref, wrkv_ref, w1_ref, w2_ref, w0_ref,
                      a1_ref, a2_ref, a0_ref, g1_ref, g2_ref,
                      r_ref, k_ref, v_ref, g_ref, a_ref, lw_ref, *, lay, tm):
    i = pl.program_id(0)
    r0 = i * tm
    in_ctx = r0 < lay.n_ctx
    pos = jnp.where(in_ctx, r0 % lay.t_ctx, (r0 - lay.n_ctx) % lay.t_lat)
    seq_len = jnp.where(in_ctx, lay.t_ctx, lay.t_lat)
    keep_prev = jnp.where(pos == 0, 0.0, 1.0)
    keep_next = jnp.where(pos + tm == seq_len, 0.0, 1.0)

    sh = sh_ref[...]
    sc = sc_ref[...]
    h = _adaln(x_ref[...], sh, sc)
    h_prev_row = _adaln(xp_ref[7:8, :], sh, sc) * keep_prev
    h_next_row = _adaln(xn_ref[0:1, :], sh, sc) * keep_next
    row = lax.broadcasted_iota(jnp.int32, h.shape, 0)
    prev = jnp.where(row == 0, h_prev_row, pltpu.roll(h, 1, 0))
    nxt = jnp.where(row == tm - 1, h_next_row, pltpu.roll(h, tm - 1, 0))
    xx = 0.5 * (prev + nxt) - h

    def mix(m):
        return (h + xx * mu_ref[m:m + 1, :]).astype(bf16)

    r_ref[...] = _dot(mix(0), wrkv_ref[0]).astype(r_ref.dtype)
    k_ref[...] = _dot(mix(1), wrkv_ref[1]).astype(k_ref.dtype)
    v_ref[...] = _dot(mix(2), wrkv_ref[2]).astype(v_ref.dtype)

    lane = lax.broadcasted_iota(jnp.int32, (tm, LANES), 1)
    first = lane < HEAD

    tw = jnp.tanh(_dot(mix(3), w1_ref[...]))
    la = _dot(mix(4), a1_ref[...])
    for z in range(2):
        sel = first if z == 0 else jnp.logical_not(first)
        w_lora = _dot(jnp.where(sel, tw, 0.0).astype(bf16), w2_ref[...])
        lw_ref[z] = -math.exp(-0.5) * jax.nn.sigmoid(w0_ref[z:z + 1, :] + w_lora)
        a_lora = _dot(jnp.where(sel, la, 0.0).astype(bf16), a2_ref[...])
        a_ref[z] = jax.nn.sigmoid(a0_ref[z:z + 1, :] + a_lora).astype(a_ref.dtype)

    g = _dot(jax.nn.sigmoid(_dot(mix(5), g1_ref[...])).astype(bf16), g2_ref[...])
    g_ref[...] = g.astype(g_ref.dtype)


def _rwkv_proj(x, mod, p, lay, tm=256):
    n, d = x.shape
    assert lay.t_ctx % tm == 0 and lay.t_lat % tm == 0
    nb8 = n // 8
    wrkv = p['wrkv'].astype(bf16)
    w1 = jnp.concatenate([p['w1'][0], p['w1'][1]], axis=1).astype(bf16)
    w2 = p['w2'].reshape(2 * p['w2'].shape[1], d).astype(bf16)
    a1 = jnp.concatenate([p['a1'][0], p['a1'][1]], axis=1).astype(bf16)
    a2 = p['a2'].reshape(2 * p['a2'].shape[1], d).astype(bf16)
    g1 = p['g1'].astype(bf16)
    g2 = p['g2'].astype(bf16)
    row_spec = pl.BlockSpec((tm, d), lambda i: (i, 0))
    z_spec = pl.BlockSpec((2, tm, d), lambda i: (0, i, 0))
    args = (x, x, x, mod, mod, p['mu'], wrkv, w1, w2, p['w0'], a1, a2, p['a0'], g1, g2)
    return pl.pallas_call(
        functools.partial(_rwkv_proj_kernel, lay=lay, tm=tm),
        grid=(n // tm,),
        in_specs=[
            row_spec,
            pl.BlockSpec((8, d), lambda i: (jnp.maximum(i * (tm // 8) - 1, 0), 0)),
            pl.BlockSpec((8, d), lambda i: (jnp.minimum((i + 1) * (tm // 8), nb8 - 1), 0)),
            _mod_spec(0, tm, lay),
            _mod_spec(1, tm, lay),
        ] + [_resident(a.shape) for a in args[5:]],
        out_specs=[row_spec] * 4 + [z_spec] * 2,
        out_shape=[jax.ShapeDtypeStruct((n, d), bf16)] * 4
        + [jax.ShapeDtypeStruct((2, n, d), bf16), jax.ShapeDtypeStruct((2, n, d), f32)],
        compiler_params=_params(("parallel",)),
        name="rwkv_proj",
    )(*args)


def _wkv_prep(r, k, v, a, lw, kk_gain, ka, forward):
    c_len = r.shape[0]
    r, k, v, a = (z.astype(f32) for z in (r, k, v, a))
    kkr = k * kk_gain
    kk = kkr * lax.rsqrt(jnp.maximum(_head_sum(kkr * kkr), 1e-24))
    beta = kk * a
    kd = k * (1.0 + (a - 1.0) * ka)
    row = lax.broadcasted_iota(jnp.int32, lw.shape, 0)
    cum = lw
    step = 1
    while step < c_len:
        if forward:
            cum = cum + jnp.where(row >= step, pltpu.roll(cum, step, 0), 0.0)
        else:
            cum = cum + jnp.where(row < c_len - step, pltpu.roll(cum, c_len - step, 0), 0.0)
        step *= 2
    total = cum[c_len - 1:c_len, :] if forward else cum[0:1, :]
    inv = jnp.exp(-cum)
    rem = jnp.exp(total - cum)
    return dict(a_t=-kk * jnp.exp(cum - lw), r_t=r * jnp.exp(cum), b_inv=beta * inv, k_inv=kd * inv,
                b_rem=beta * rem, k_rem=kd * rem, p_total=jnp.exp(total), v=v)


def _wkv_kernel(*refs, has_init):
    dir_refs = (refs[0:5], refs[5:10])
    kk_ref, ka_ref = refs[10:12]
    n_in = 13 if has_init else 12
    yf_ref, yb_ref, sout_ref, s_scr = refs[n_in:]
    n_par = yf_ref.shape[0]
    c_len = WKV_CHUNK
    gw = WKV_LANES
    hpg = gw // HEAD
    n_grp = yf_ref.shape[2] // gw
    c = pl.program_id(1)

    @pl.when(c == 0)
    def _():
        s_scr[...] = jnp.zeros_like(s_scr)
        if has_init:
            s0_ref = refs[n_in - 1]
            for p in range(n_par):
                for dr in range(2):
                    for h in range(n_grp * hpg):
                        o = (h % hpg) * HEAD
                        s_scr[p, dr, h // hpg, o:o + HEAD, o:o + HEAD] = s0_ref[p, dr, h]

    t_s = lax.broadcasted_iota(jnp.int32, (c_len, gw), 0)
    i_s = lax.broadcasted_iota(jnp.int32, (c_len, gw), 1) & (c_len - 1)
    strict = (i_s < t_s, i_s > t_s)
    incl = (i_s <= t_s, i_s >= t_s)
    eye = jnp.where(t_s == i_s, 1.0, 0.0)
    pair_block = (t_s >> 1) == (i_s >> 1)
    levels = []
    for lb in range(1, int(math.log2(c_len))):
        levels.append(((t_s >> (lb + 1)) == (i_s >> (lb + 1))) & ((t_s >> lb) != (i_s >> lb)))
    rb = lax.broadcasted_iota(jnp.int32, (hpg * c_len, gw), 0) // c_len
    cb = lax.broadcasted_iota(jnp.int32, (hpg * c_len, gw), 1) // HEAD
    same_head = rb == cb

    def bd(x):
        return jnp.where(same_head, jnp.concatenate([x] * hpg, axis=0), 0.0).astype(bf16)

    chains = []
    for p in range(n_par):
        for dr in range(2):
            op = _wkv_prep(*(ref[p] for ref in dir_refs[dr]), kk_ref[...], ka_ref[...], forward=(dr == 0))
            for pr in range(n_grp):
                chains.append((op, dr, (p, dr, pr), slice(pr * gw, (pr + 1) * gw),
                               (yf_ref if dr == 0 else yb_ref).at[p]))
    ids = range(len(chains))
    lhs = [jnp.concatenate([op['a_t'][:, cs], op['r_t'][:, cs]], axis=0).astype(bf16)
           for op, dr, si, cs, y_ref in chains]
    s0 = [s_scr[si] for op, dr, si, cs, y_ref in chains]
    sc = [_dot_nt(lhs[i], jnp.concatenate([bd(op['b_inv'][:, cs]), bd(op['k_inv'][:, cs]),
                                           s0[i].astype(bf16)], axis=0))
          for i, (op, dr, si, cs, y_ref) in enumerate(chains)]
    sc_b = [s[:, :gw] for s in sc]
    sc_k = [s[:, gw:2 * gw] for s in sc]
    x0 = [s[:, 2 * gw:] for s in sc]
    n_mat = [jnp.where(strict[dr], sc_b[i][:c_len], 0.0) for i, (op, dr, si, cs, y_ref) in enumerate(chains)]
    v_bd = [bd(op['v'][:, cs]) for op, dr, si, cs, y_ref in chains]
    x = [x0[i][:c_len] + _dot(jnp.where(strict[dr], sc_k[i][:c_len], 0.0).astype(bf16), v_bd[i])
         for i, (op, dr, si, cs, y_ref) in enumerate(chains)]
    t_inv = [eye + jnp.where(pair_block, n_mat[i], 0.0) for i in ids]
    for off_diag in levels:
        half = [_dot(t_inv[i].astype(bf16), bd(jnp.where(off_diag, n_mat[i], 0.0))) for i in ids]
        t_inv = [t_inv[i] + _dot(half[i].astype(bf16), bd(t_inv[i])) for i in ids]
    uv_bd = [jnp.concatenate([bd(_dot(t_inv[i].astype(bf16), bd(x[i]))), v_bd[i]], axis=0) for i in ids]
    for i, (op, dr, si, cs, y_ref) in enumerate(chains):
        rbk = jnp.concatenate([jnp.where(incl[dr], sc_b[i][c_len:], 0.0),
                               jnp.where(incl[dr], sc_k[i][c_len:], 0.0)], axis=1).astype(bf16)
        y_ref[:, cs] = (x0[i][c_len:] + _dot(rbk, uv_bd[i])).astype(y_ref.dtype)
    for i, (op, dr, si, cs, y_ref) in enumerate(chains):
        bk_rem = jnp.concatenate([bd(op['b_rem'][:, cs]), bd(op['k_rem'][:, cs])], axis=0)
        s_scr[si] = s0[i] * op['p_total'][:, cs] + _dot_tn(uv_bd[i], bk_rem)

    @pl.when(c == pl.num_programs(1) - 1)
    def _():
        for p in range(n_par):
            for dr in range(2):
                for h in range(n_grp * hpg):
                    o = (h % hpg) * HEAD
                    sout_ref[p, dr, h] = s_scr[p, dr, h // hpg, o:o + HEAD, o:o + HEAD]


def _wkv_scan(r, k, v, a2, lw2, p, s0, row0, n_seq, t_len, n_par=1):
    n, d = r.shape
    c_len = WKV_CHUNK
    nc = t_len // c_len
    n_grp = d // WKV_LANES
    has_init = s0 is not None
    if n_seq % n_par or (row0 // t_len) % n_par:
        n_par = 1
    assert n % t_len == 0 and row0 % t_len == 0
    u0 = row0 // t_len // n_par

    tok_f = pl.BlockSpec((n_par, c_len, d), lambda s, c: (u0 + s, c, 0))
    tok_b = pl.BlockSpec((n_par, c_len, d), lambda s, c: (u0 + s, nc - 1 - c, 0))
    z_f = pl.BlockSpec((None, n_par, c_len, d), lambda s, c: (0, u0 + s, c, 0))
    z_b = pl.BlockSpec((None, n_par, c_len, d), lambda s, c: (1, u0 + s, nc - 1 - c, 0))
    vec = pl.BlockSpec((1, d), lambda s, c: (0, 0))
    st = pl.BlockSpec((n_par, 2, RWKV_HEADS, HEAD, HEAD), lambda s, c: (s, 0, 0, 0, 0))
    r3, k3, v3 = (z.reshape(n // t_len, t_len, d) for z in (r, k, v))
    a4, lw4 = (z.reshape(2, n // t_len, t_len, d) for z in (a2, lw2))
    in_specs = [tok_f, tok_f, tok_f, z_f, z_f, tok_b, tok_b, tok_b, z_b, z_b, vec, vec]
    args = [r3, k3, v3, a4, lw4, r3, k3, v3, a4, lw4, p['k_k'].reshape(1, d), p['k_a'].reshape(1, d)]
    if has_init:
        in_specs.append(st)
        args.append(s0)
    y_shape = jax.ShapeDtypeStruct((n_seq, t_len, d), bf16)
    y_f, y_b, s_out = pl.pallas_call(
        functools.partial(_wkv_kernel, has_init=has_init),
        grid=(n_seq // n_par, nc),
        in_specs=in_specs,
        out_specs=[
            pl.BlockSpec((n_par, c_len, d), lambda s, c: (s, c, 0)),
            pl.BlockSpec((n_par, c_len, d), lambda s, c: (s, nc - 1 - c, 0)),
            st,
        ],
        out_shape=[y_shape, y_shape, jax.ShapeDtypeStruct((n_seq, 2, RWKV_HEADS, HEAD, HEAD), f32)],
        scratch_shapes=[pltpu.VMEM((n_par, 2, n_grp, WKV_LANES, WKV_LANES), f32)],
        compiler_params=_params(("parallel", "arbitrary")),
        name="wkv_scan",
    )(*args)
    return y_f.reshape(n_seq * t_len, d), y_b.reshape(n_seq * t_len, d), s_out


def _rwkv_post_kernel(x_ref, yfc_ref, ybc_ref, yfl_ref, ybl_ref, r_ref, k_ref, v_ref, g_ref, a_ref,
                      ka_ref, rk_ref, lnw_ref, lnb_ref, gate_ref, wo_ref, o_ref, *, nb_ctx):
    i = pl.program_id(0)
    y = (_pick_rows(i, nb_ctx, yfc_ref, yfl_ref).astype(f32)
         + _pick_rows(i, nb_ctx, ybc_ref, ybl_ref).astype(f32))
    mu = _head_sum(y) * (1.0 / HEAD)
    yc = y - mu
    var = _head_sum(yc * yc) * (1.0 / HEAD)
    yn = yc * lax.rsqrt(var + RWKV_GN_EPS) * lnw_ref[...] + lnb_ref[...]
    k = k_ref[...].astype(f32)
    ka = ka_ref[...]
    a_sum = a_ref[0].astype(f32) + a_ref[1].astype(f32)
    kd_sum = k * (2.0 + (a_sum - 2.0) * ka)
    bonus = _head_sum(r_ref[...].astype(f32) * kd_sum * rk_ref[...]) * v_ref[...].astype(f32)
    o = ((yn + bonus) * g_ref[...].astype(f32)).astype(bf16)
    o_ref[...] = x_ref[...] + gate_ref[...] * _dot(o, wo_ref[...])


def _rwkv_post_residual(x, mod, y_ctx, y_lat, r, k, v, g, a2, p, wo_bf16, lay, tm=256):
    n, d = r.shape
    tok = pl.BlockSpec((tm, d), lambda i: (i, 0))
    tok_z = pl.BlockSpec((2, tm, d), lambda i: (0, i, 0))
    vec = pl.BlockSpec((1, d), lambda i: (0, 0))
    yc_spec, yl_spec = _split_row_specs((tm, d), tm, lay)
    row = lambda a: a.reshape(1, d)
    return pl.pallas_call(
        functools.partial(_rwkv_post_kernel, nb_ctx=lay.n_ctx // tm),
        grid=(n // tm,),
        in_specs=[tok, yc_spec, yc_spec, yl_spec, yl_spec, tok, tok, tok, tok, tok_z, vec, vec, vec, vec,
                  _mod_spec(2, tm, lay), _resident((d, d))],
        out_specs=tok,
        out_shape=jax.ShapeDtypeStruct((n, d), f32),
        compiler_params=_params(("parallel",)),
        name="rwkv_post",
    )(x, y_ctx[0], y_ctx[1], y_lat[0], y_lat[1], r, k, v, g, a2,
      row(p['k_a']), row(p['r_k']), row(p['ln_w']), row(p['ln_b']), mod, wo_bf16)


def _rwkv_layer(x, mod, state0, p, wo_bf16, lay):
    r, k, v, g, a2, lw2 = _rwkv_proj(x, mod, p, lay)
    yf_c, yb_c, s_ctx = _wkv_scan(r, k, v, a2, lw2, p, None, 0, lay.b_ctx, lay.t_ctx)
    yf_l, yb_l, _ = _wkv_scan(r, k, v, a2, lw2, p, state0, lay.n_ctx, lay.b_lat, lay.t_lat)
    x = _rwkv_post_residual(x, mod, (yf_c, yb_c), (yf_l, yb_l), r, k, v, g, a2, p, wo_bf16, lay)
    return x, s_ctx


def _rope_table(n_tok, width):
    n_rows = n_tok // GRID_W
    rows = jnp.broadcast_to(jnp.arange(n_rows, dtype=f32)[:, None], (n_rows, GRID_W)).reshape(-1)
    cols = jnp.broadcast_to(jnp.arange(GRID_W, dtype=f32)[None, :], (n_rows, GRID_W)).reshape(-1)
    n_freq = HEAD // 4
    inv = ROPE_BASE ** (-jnp.arange(n_freq, dtype=f32) / n_freq)
    ang = jnp.concatenate([rows[:, None] * inv, cols[:, None] * inv], axis=-1)
    reps = width // (HEAD // 2)
    return jnp.tile(jnp.cos(ang), (1, reps)), jnp.tile(jnp.sin(ang), (1, reps))


def kernel(x_prompt, x_sample, cache_da_k, cache_da_v, state_rwkv, cache_swa_k, cache_swa_v, c, c_ctx, ada_w, ada_b, mlp_w1, mlp_w2, da_wqkv, da_q_norm, da_k_norm, da_lambda, da_subln, da_wo, rwkv_mu, rwkv_wrkv, rwkv_w0, rwkv_w1, rwkv_w2, rwkv_a0, rwkv_a1, rwkv_a2, rwkv_g1, rwkv_g2, rwkv_k_k, rwkv_k_a, rwkv_r_k, rwkv_ln_w, rwkv_ln_b, rwkv_wo, swa_wqkv, swa_q_norm, swa_k_norm, swa_sink, swa_wo):
    b_ctx, t_ctx, d = x_prompt.shape
    b_lat, t_lat, _ = x_sample.shape
    lay = Layout(b_ctx, t_ctx, b_lat, t_lat)
    depth = ada_w.shape[0]
    tm = 512
    assert d == D_MODEL and lay.n_ctx % t_lat == 0

    xs = (x_prompt.reshape(lay.n_ctx, d), x_sample.reshape(lay.n_lat, d))
    cond = jnp.concatenate([c_ctx[None, :], c, jnp.zeros((8 - 1 - b_lat, d), f32)], axis=0)
    mods = _modulation(cond, ada_w, ada_b)
    mods = mods[:, :1 + b_lat].reshape(depth, 1 + b_lat, 6, 1, d)

    rope_da = _rope_table(t_lat, LANES)
    rope_swa = _rope_table(t_lat, d)

    mlp_w1_b, mlp_w2_b = mlp_w1.astype(bf16), mlp_w2.astype(bf16)
    da_wqkv_b, da_wo_b = da_wqkv.astype(bf16), da_wo.astype(bf16)
    n_da = da_wqkv.shape[0]
    da_kv = None
    rwkv_s, swa_k, swa_v = [], [], []
    for i in range(depth):
        kind = i % N_MIXERS
        j = i // N_MIXERS
        mod = mods[i]
        w1 = LayerWeight(mlp_w1_b, i)
        w2 = LayerWeight(mlp_w2_b, i)
        if kind == 1:
            p = {'mu': rwkv_mu[j], 'wrkv': rwkv_wrkv[j], 'w0': rwkv_w0[j], 'w1': rwkv_w1[j],
                 'w2': rwkv_w2[j], 'a0': rwkv_a0[j], 'a1': rwkv_a1[j], 'a2': rwkv_a2[j],
                 'g1': rwkv_g1[j], 'g2': rwkv_g2[j], 'k_k': rwkv_k_k[j], 'k_a': rwkv_k_a[j],
                 'r_k': rwkv_r_k[j], 'ln_w': rwkv_ln_w[j], 'ln_b': rwkv_ln_b[j]}
            x = xs[0] if len(xs) == 1 else jnp.concatenate(xs, axis=0)
            x, s_new = _rwkv_layer(x, mod, state_rwkv[:, j], p, rwkv_wo[j].astype(bf16), lay)
            rwkv_s.append(s_new)
            xs = (_mlp(x, mod, w1, w2, lay),)
            continue
        if kind == 0:
            p = {'q_norm': da_q_norm[j], 'k_norm': da_k_norm[j], 'lam': da_lambda[j], 'subln': da_subln[j]}
            lam_init = 0.8 - 0.6 * math.exp(-0.3 * i)
            q, kv = _ln_matmul(xs, mod, LayerWeight(da_wqkv_b, j), lay)
            o_ctx, o_lat, *da_kv = _da_mixer(q, kv, cache_da_k, cache_da_v, j, n_da, da_kv, p, lam_init,
                                             rope_da, lay)
            wo = LayerWeight(da_wo_b, j)
        else:
            p = {'q_norm': swa_q_norm[j], 'k_norm': swa_k_norm[j], 'sink': swa_sink[j]}
            wqkv, wo = _swa_group_major_weights(swa_wqkv[j], swa_wo[j])
            q, kv = _ln_matmul(xs, mod, wqkv, lay)
            o_ctx, o_lat, k_new, v_new = _swa_mixer(q, kv, cache_swa_k, cache_swa_v, j, p, rope_swa, lay)
            swa_k.append(k_new.reshape(b_ctx, t_ctx, SWA_KV_HEADS, HEAD))
            swa_v.append(v_new.reshape(b_ctx, t_ctx, SWA_KV_HEADS, HEAD))
        if i == depth - 1 and len(xs) == 1:
            nb_ctx = lay.n_ctx // tm
            xs = (_out_proj_mlp(xs, (o_ctx,), mod, wo, w1, w2, lay, rows=(0, nb_ctx), tm=tm),
                  _out_proj_mlp(xs, (o_lat,), mod, wo, w1, w2, lay, rows=(nb_ctx, lay.n_lat // tm), tm=tm))
        else:
            xs = (_out_proj_mlp(xs, (o_ctx, o_lat), mod, wo, w1, w2, lay, tm=tm),)

    if len(xs) == 1:
        xs = (xs[0][:lay.n_ctx], xs[0][lay.n_ctx:])
    y_prompt = xs[0].reshape(b_ctx, t_ctx, d)
    y_sample = xs[1].reshape(b_lat, t_lat, d)
    new_da_k, new_da_v = (a.reshape(b_ctx, n_da, t_ctx, DA_HEADS, LANES) for a in da_kv)
    return (y_prompt, y_sample, new_da_k, new_da_v,
            jnp.stack(rwkv_s, axis=1), jnp.stack(swa_k, axis=1), jnp.stack(swa_v, axis=1))
```

```python
import functools
import math
from typing import NamedTuple

import jax
import jax.numpy as jnp
from jax import lax
from jax.experimental import pallas as pl
from jax.experimental.pallas import tpu as pltpu

f32 = jnp.float32
bf16 = jnp.bfloat16

D_MODEL = 1024
N_MIXERS = 3
GRID_W = 64
ROPE_BASE = 10000.0
NORM_EPS = 1e-6
NEG_INF = -1e30
LOG2E = math.log2(math.e)
HEAD = 64
LANES = 128
DA_HEADS = D_MODEL // LANES
RWKV_HEADS = D_MODEL // HEAD
RWKV_GN_EPS = 64e-5
SWA_HEADS = D_MODEL // HEAD
SWA_KV_HEADS = SWA_HEADS // 4
SWA_GROUP = SWA_HEADS // SWA_KV_HEADS
SWA_KV = SWA_KV_HEADS * HEAD
WINDOW = 128
WKV_CHUNK = 64
WKV_LANES = 128
VMEM_LIMIT = 56 * 1024 * 1024
QK_SCALE = HEAD ** -0.5 * LOG2E


class Layout(NamedTuple):
    b_ctx: int
    t_ctx: int
    b_lat: int
    t_lat: int

    @property
    def n_ctx(self):
        return self.b_ctx * self.t_ctx

    @property
    def n_lat(self):
        return self.b_lat * self.t_lat

    @property
    def n(self):
        return self.n_ctx + self.n_lat


def _params(sem):
    return pltpu.CompilerParams(dimension_semantics=sem, vmem_limit_bytes=VMEM_LIMIT)


def _group_of_tile(i, tm, lay):
    r = i * tm
    return jnp.where(r < lay.n_ctx, 0, 1 + (r - lay.n_ctx) // lay.t_lat)


def _mod_spec(chunk, tm, lay, tile0=0):
    return pl.BlockSpec((None, None, 1, D_MODEL),
                        lambda i, *_: (_group_of_tile(i + tile0, tm, lay), chunk, 0, 0))


def _resident(shape):
    return pl.BlockSpec(shape, lambda *_: (0,) * len(shape), pipeline_mode=pl.Buffered(1))


class LayerWeight(NamedTuple):
    stack: jax.Array
    layer: int

    @property
    def shape(self):
        return self.stack.shape[1:]

    def spec(self):
        idx = (self.layer,) + (0,) * (self.stack.ndim - 1)
        return pl.BlockSpec((None,) + self.shape, lambda *_: idx, pipeline_mode=pl.Buffered(1))


def _weight_spec(w):
    return w.spec() if isinstance(w, LayerWeight) else _resident(w.shape)


def _weight_arg(w):
    return w.stack if isinstance(w, LayerWeight) else w


def _split_row_specs(block, tm, lay):
    nb_ctx = lay.n_ctx // tm
    lead = (0,) * (len(block) - 2)
    ctx = pl.BlockSpec(block, lambda i, *_: lead + (jnp.minimum(i, nb_ctx - 1), 0))
    lat = pl.BlockSpec(block, lambda i, *_: lead + (jnp.maximum(i - nb_ctx, 0), 0))
    return ctx, lat


def _dot(a, b):
    return jnp.dot(a, b, preferred_element_type=f32)


def _dot_nt(a, b):
    return lax.dot_general(a, b, (((1,), (1,)), ((), ())), preferred_element_type=f32)


def _dot_tn(a, b):
    return lax.dot_general(a, b, (((0,), (0,)), ((), ())), preferred_element_type=f32)


def _split2(x):
    hi = x.astype(bf16)
    return hi, (x - hi.astype(f32)).astype(bf16)


def _head_ones():
    r = lax.broadcasted_iota(jnp.int32, (LANES, LANES), 0)
    c = lax.broadcasted_iota(jnp.int32, (LANES, LANES), 1)
    return ((r // HEAD) == (c // HEAD)).astype(bf16)


def _head_sum(x):
    g = _head_ones()
    xb = x.astype(bf16)
    cols = [_dot(xb[:, p * LANES:(p + 1) * LANES], g) for p in range(x.shape[1] // LANES)]
    return cols[0] if len(cols) == 1 else jnp.concatenate(cols, axis=1)


def _head_rms(x, gain):
    ms = _head_sum(x * x) * (1.0 / HEAD)
    return x * lax.rsqrt(ms + NORM_EPS) * gain


def _adaln(x, shift, scale):
    ms = jnp.mean(x * x, axis=-1, keepdims=True)
    return x * lax.rsqrt(ms + NORM_EPS) * (1.0 + scale) + shift


def _rope(x, cos, sin):
    w = x.shape[1]
    lane = lax.broadcasted_iota(jnp.int32, x.shape, 1)
    low = (lane & (HEAD - 1)) < HEAD // 2
    up = pltpu.roll(x, w - HEAD // 2, 1)
    dn = pltpu.roll(x, HEAD // 2, 1)
    return x * cos + jnp.where(low, -up, dn) * sin


def _pick_rows(i, nb_ctx, ctx_ref, lat_ref):
    return jnp.where(i < nb_ctx, ctx_ref[...], lat_ref[...])


def _mod_kernel(c_ref, w_ref, b_ref, o_ref):
    c = c_ref[...]
    s = c * jax.nn.sigmoid(c)
    sh, sl = _split2(s)
    wh, wl = _split2(w_ref[...])
    o_ref[...] = _dot(sh, wh) + (_dot(sh, wl) + _dot(sl, wh)) + b_ref[...]


def _modulation(cond8, ada_w, ada_b):
    depth, d, d6 = ada_w.shape
    tn = 1536
    return pl.pallas_call(
        _mod_kernel,
        grid=(depth, d6 // tn),
        in_specs=[
            pl.BlockSpec((8, d), lambda l, j: (0, 0)),
            pl.BlockSpec((None, d, tn), lambda l, j: (l, 0, j)),
            pl.BlockSpec((None, 1, tn), lambda l, j: (l, 0, j)),
        ],
        out_specs=pl.BlockSpec((None, 8, tn), lambda l, j: (l, 0, j)),
        out_shape=jax.ShapeDtypeStruct((depth, 8, d6), f32),
        compiler_params=_params(("parallel", "parallel")),
        name="modulation",
    )(cond8, ada_w, ada_b.reshape(depth, 1, d6))


def _ln_mm_kernel(*refs, tn, nb_ctx):
    if nb_ctx is None:
        x_ref, sh_ref, sc_ref, w_ref, q_ref, kv_ref = refs
        x = x_ref[...]
    else:
        xc_ref, xl_ref, sh_ref, sc_ref, w_ref, q_ref, kv_ref = refs
        x = _pick_rows(pl.program_id(0), nb_ctx, xc_ref, xl_ref)
    h = _adaln(x, sh_ref[...], sc_ref[...]).astype(bf16)
    n_q = q_ref.shape[1]
    for j in range(w_ref.shape[1] // tn):
        cols = slice(j * tn, (j + 1) * tn)
        part = _dot(h, w_ref[:, cols])
        if j * tn < n_q:
            q_ref[:, cols] = part.astype(q_ref.dtype)
        else:
            kv_ref[:, j * tn - n_q:(j + 1) * tn - n_q] = part


def _ln_matmul(xs, mod, w, lay, tm=512, tn=512):
    d = xs[0].shape[1]
    n_out = w.shape[1]
    assert d % tn == 0
    if len(xs) == 1:
        x_specs = [pl.BlockSpec((tm, d), lambda i: (i, 0))]
        nb_ctx = None
    else:
        x_specs = list(_split_row_specs((tm, d), tm, lay))
        nb_ctx = lay.n_ctx // tm
    return pl.pallas_call(
        functools.partial(_ln_mm_kernel, tn=tn, nb_ctx=nb_ctx),
        grid=(lay.n // tm,),
        in_specs=x_specs + [_mod_spec(0, tm, lay), _mod_spec(1, tm, lay), _weight_spec(w)],
        out_specs=[pl.BlockSpec((tm, d), lambda i: (i, 0)), pl.BlockSpec((tm, n_out - d), lambda i: (i, 0))],
        out_shape=[jax.ShapeDtypeStruct((lay.n, d), bf16), jax.ShapeDtypeStruct((lay.n, n_out - d), f32)],
        compiler_params=_params(("parallel",)),
        name="adaln_proj",
    )(*xs, mod, mod, _weight_arg(w))


def _mlp_body(x, sh_ref, sc_ref, g_ref, w1_ref, w2_ref, acc_ref, o_ref, fc):
    h = _adaln(x, sh_ref[...], sc_ref[...]).astype(bf16)
    d_ff = w1_ref.shape[1]
    for c in range(d_ff // fc):
        a = _dot(h, w1_ref[:, c * fc:(c + 1) * fc])
        a = jnp.square(jnp.maximum(a, 0.0)).astype(bf16)
        part = _dot(a, w2_ref[c * fc:(c + 1) * fc, :])
        if c == 0:
            acc_ref[...] = part
        else:
            acc_ref[...] += part
    o_ref[...] = x + g_ref[...] * acc_ref[...]


def _mlp_kernel(x_ref, sh_ref, sc_ref, g_ref, w1_ref, w2_ref, o_ref, acc_ref, *, fc):
    _mlp_body(x_ref[...], sh_ref, sc_ref, g_ref, w1_ref, w2_ref, acc_ref, o_ref, fc)


def _mlp(x, mod, w1, w2, lay, tm=512, fc=1024):
    n, d = x.shape
    return pl.pallas_call(
        functools.partial(_mlp_kernel, fc=fc),
        grid=(n // tm,),
        in_specs=[
            pl.BlockSpec((tm, d), lambda i: (i, 0)),
            _mod_spec(3, tm, lay), _mod_spec(4, tm, lay), _mod_spec(5, tm, lay),
            _weight_spec(w1), _weight_spec(w2),
        ],
        out_specs=pl.BlockSpec((tm, d), lambda i: (i, 0)),
        out_shape=jax.ShapeDtypeStruct((n, d), f32),
        scratch_shapes=[pltpu.VMEM((tm, d), f32)],
        compiler_params=_params(("parallel",)),
        name="mlp",
    )(x, mod, mod, mod, _weight_arg(w1), _weight_arg(w2))


def _out_mlp_kernel(*refs, fc, nb_ctx, split_x, split_a):
    refs = list(refs)
    i = pl.program_id(0)
    x = _pick_rows(i, nb_ctx, refs.pop(0), refs.pop(0)) if split_x else refs.pop(0)[...]
    a = _pick_rows(i, nb_ctx, refs.pop(0), refs.pop(0)) if split_a else refs.pop(0)[...]
    gm_ref, wo_ref, sh_ref, sc_ref, gf_ref, w1_ref, w2_ref, o_ref, acc_ref = refs
    x1 = x + gm_ref[...] * _dot(a.astype(bf16), wo_ref[...])
    _mlp_body(x1, sh_ref, sc_ref, gf_ref, w1_ref, w2_ref, acc_ref, o_ref, fc)


def _out_proj_mlp(xs, a_parts, mod, wo, w1, w2, lay, rows=None, tm=512, fc=1024):
    d = D_MODEL
    k_in = a_parts[0].shape[1]
    tile0, n_tiles = rows if rows is not None else (0, lay.n // tm)
    split_x, split_a = len(xs) == 2, len(a_parts) == 2
    assert rows is None or not (split_x or split_a)
    x_specs = (list(_split_row_specs((tm, d), tm, lay)) if split_x
               else [pl.BlockSpec((tm, d), lambda i: (i + tile0, 0))])
    a_specs = (list(_split_row_specs((tm, k_in), tm, lay)) if split_a
               else [pl.BlockSpec((tm, k_in), lambda i: (i, 0))])
    return pl.pallas_call(
        functools.partial(_out_mlp_kernel, fc=fc, nb_ctx=lay.n_ctx // tm, split_x=split_x, split_a=split_a),
        grid=(n_tiles,),
        in_specs=x_specs + a_specs + [
            _mod_spec(2, tm, lay, tile0), _weight_spec(wo),
            _mod_spec(3, tm, lay, tile0), _mod_spec(4, tm, lay, tile0), _mod_spec(5, tm, lay, tile0),
            _weight_spec(w1), _weight_spec(w2),
        ],
        out_specs=pl.BlockSpec((tm, d), lambda i: (i, 0)),
        out_shape=jax.ShapeDtypeStruct((n_tiles * tm, d), f32),
        scratch_shapes=[pltpu.VMEM((tm, d), f32)],
        compiler_params=_params(("parallel",)),
        name="out_proj_mlp",
    )(*xs, *a_parts, mod, _weight_arg(wo), mod, mod, mod, _weight_arg(w1), _weight_arg(w2))


def _da_lambda(lam_ref, lam_init):
    lp = lam_ref[...]
    e1 = jnp.exp(jnp.sum(lp[0:1] * lp[1:2], axis=-1, keepdims=True))
    e2 = jnp.exp(jnp.sum(lp[2:3] * lp[3:4], axis=-1, keepdims=True))
    return e1 - e2 + lam_init


def _da_qk(q, k_bf16):
    lane = lax.broadcasted_iota(jnp.int32, q.shape, 1)
    maps = [jnp.where((lane < HEAD) == (m == 0), q, 0.0).astype(bf16) for m in range(2)]
    return _dot_nt(jnp.concatenate(maps, axis=0), k_bf16)


def _with_ones(v_bf16):
    return jnp.concatenate([v_bf16, jnp.ones_like(v_bf16)], axis=1)


def _da_pv(scores, v_ones, lam):
    t = scores.shape[0] // 2
    e = jnp.exp2(scores - jnp.max(scores, axis=-1, keepdims=True))
    pv = _dot(e.astype(bf16), v_ones)
    o = pv[:, :LANES] * (1.0 / pv[:, LANES:])
    return o[:t] - lam * o[t:]


def _da_finish(o, sub_ref, lam_init):
    ms = jnp.mean(o * o, axis=-1, keepdims=True)
    return o * lax.rsqrt(ms + NORM_EPS) * sub_ref[...] * (1.0 - lam_init)


def _da_ctx_kernel(*refs, lam_init, layer):
    lam_ref, q_ref, k_ref, v_ref, qn_ref, kn_ref, sub_ref = refs[:7]
    o_ref, knew_ref, vnew_ref = refs[-3:]
    lam = _da_lambda(lam_ref, lam_init)
    q = _head_rms(q_ref[...].astype(f32), qn_ref[...]) * QK_SCALE
    k = _head_rms(k_ref[...], kn_ref[...])
    v = v_ref[...]
    if layer == 0:
        for l in range(knew_ref.shape[0]):
            knew_ref[l] = k if l == 0 else jnp.zeros_like(k)
            vnew_ref[l] = v if l == 0 else jnp.zeros_like(v)
    else:
        knew_ref[...] = k
        vnew_ref[...] = v
    kb = k.astype(bf16)
    vb = v.astype(bf16)
    cols = [slice(h * LANES, (h + 1) * LANES) for h in range(q.shape[1] // LANES)]
    s_next = _da_qk(q[:, cols[0]], kb[:, cols[0]])
    for h, cs in enumerate(cols):
        s_cur = s_next
        if h + 1 < len(cols):
            s_next = _da_qk(q[:, cols[h + 1]], kb[:, cols[h + 1]])
        o = _da_finish(_da_pv(s_cur, _with_ones(vb[:, cs]), lam), sub_ref, lam_init)
        o_ref[:, cs] = o.astype(o_ref.dtype)


def _da_lat_kernel(lam_ref, q_ref, k_ref, v_ref, ck_ref, cv_ref, qn_ref, kn_ref, sub_ref,
                   cq_ref, sq_ref, ckk_ref, skk_ref, o_ref, kf_scr, vf_scr, *, lam_init, past):
    cols = [slice(h * LANES, (h + 1) * LANES) for h in range(q_ref.shape[1] // LANES)]

    @pl.when(pl.program_id(2) == 0)
    def _():
        k = _rope(_head_rms(k_ref[...], kn_ref[...]), ckk_ref[...], skk_ref[...])
        kf_scr[0:past, :] = ck_ref[...].astype(bf16)
        kf_scr[past:, :] = k.astype(bf16)
        for h, cs in enumerate(cols):
            vf_scr[h, 0:past, :] = _with_ones(cv_ref[:, cs].astype(bf16))
            vf_scr[h, past:, :] = _with_ones(v_ref[:, cs].astype(bf16))

    lam = _da_lambda(lam_ref, lam_init)
    q = _rope(_head_rms(q_ref[...].astype(f32), qn_ref[...]), cq_ref[...], sq_ref[...]) * QK_SCALE
    s_next = _da_qk(q[:, cols[0]], kf_scr[:, cols[0]])
    for h, cs in enumerate(cols):
        s_cur = s_next
        if h + 1 < len(cols):
            s_next = _da_qk(q[:, cols[h + 1]], kf_scr[:, cols[h + 1]])
        o_ref[:, cs] = _da_finish(_da_pv(s_cur, vf_scr[h], lam), sub_ref, lam_init).astype(o_ref.dtype)


def _da_mixer(q, kv, cache_k, cache_v, j, n_layers, kv_prev, p, lam_init, rope_tab, lay, tq=512):
    d = D_MODEL
    nh = DA_HEADS
    assert (kv_prev is None) == (j == 0)
    qn = jnp.tile(p['q_norm'], 2).reshape(1, LANES)
    kn = jnp.tile(p['k_norm'], 2).reshape(1, LANES)
    qn_all = jnp.tile(qn, (1, nh))
    kn_all = jnp.tile(kn, (1, nh))
    sub = p['subln'].reshape(1, LANES)
    small = lambda shape: pl.BlockSpec(shape, lambda *_: (0,) * len(shape))
    t = lay.t_ctx
    ctx_rows = pl.BlockSpec((t, d), lambda b: (b, 0))
    kv_shape = jax.ShapeDtypeStruct((lay.b_ctx, n_layers, t, d), f32)
    if kv_prev is None:
        kv_spec = pl.BlockSpec((None, n_layers, t, d), lambda b: (b, 0, 0, 0))
        kv_in_specs, kv_args, aliases = [], (), {}
    else:
        kv_spec = pl.BlockSpec((None, None, t, d), lambda b: (b, j, 0, 0))
        kv_in_specs = [pl.BlockSpec(memory_space=pl.ANY)] * 2
        kv_args, aliases = tuple(kv_prev), {7: 1, 8: 2}
    o_ctx, k_new, v_new = pl.pallas_call(
        functools.partial(_da_ctx_kernel, lam_init=lam_init, layer=j),
        grid=(lay.b_ctx,),
        in_specs=[
            small((4, HEAD)),
            ctx_rows,
            ctx_rows,
            pl.BlockSpec((t, d), lambda b: (b, 1)),
            small((1, d)), small((1, d)), small((1, LANES)),
        ] + kv_in_specs,
        out_specs=[ctx_rows, kv_spec, kv_spec],
        out_shape=[jax.ShapeDtypeStruct((lay.n_ctx, d), bf16), kv_shape, kv_shape],
        input_output_aliases=aliases,
        compiler_params=_params(("parallel",)),
        name="da_ctx",
    )(p['lam'], q, kv, kv, qn_all, kn_all, sub, *kv_args)

    tl = lay.t_lat
    past = cache_k.shape[2]
    row0 = lay.n_ctx // tl
    q0 = lay.n_ctx // tq
    nq = tl // tq
    cos, sin = rope_tab
    ck = cache_k.reshape(lay.b_lat, cache_k.shape[1], past, d)
    cv = cache_v.reshape(lay.b_lat, cache_v.shape[1], past, d)
    hg = 2
    gw = hg * LANES
    ng = nh // hg
    cos_g, sin_g = jnp.tile(cos, (1, hg)), jnp.tile(sin, (1, hg))
    o_lat = pl.pallas_call(
        functools.partial(_da_lat_kernel, lam_init=lam_init, past=past),
        grid=(lay.b_lat, ng, nq),
        in_specs=[
            small((4, HEAD)),
            pl.BlockSpec((tq, gw), lambda b, h, i: (q0 + b * nq + i, h)),
            pl.BlockSpec((tl, gw), lambda b, h, i: (row0 + b, h)),
            pl.BlockSpec((tl, gw), lambda b, h, i: (row0 + b, ng + h)),
            pl.BlockSpec((None, None, past, gw), lambda b, h, i: (b, j, 0, h)),
            pl.BlockSpec((None, None, past, gw), lambda b, h, i: (b, j, 0, h)),
            small((1, gw)), small((1, gw)), small((1, LANES)),
            pl.BlockSpec((tq, gw), lambda b, h, i: (i, 0)),
            pl.BlockSpec((tq, gw), lambda b, h, i: (i, 0)),
            _resident((tl, gw)), _resident((tl, gw)),
        ],
        out_specs=pl.BlockSpec((tq, gw), lambda b, h, i: (b * nq + i, h)),
        out_shape=jax.ShapeDtypeStruct((lay.n_lat, d), bf16),
        scratch_shapes=[pltpu.VMEM((past + tl, gw), bf16), pltpu.VMEM((hg, past + tl, 2 * LANES), bf16)],
        compiler_params=_params(("parallel", "parallel", "arbitrary")),
        name="da_lat",
    )(p['lam'], q, kv, kv, ck, cv, qn_all[:, :gw], kn_all[:, :gw], sub, cos_g, sin_g, cos_g, sin_g)
    return o_ctx, o_lat, k_new, v_new


def _swa_head_mask(shape, j):
    lane = lax.broadcasted_iota(jnp.int32, shape, 1)
    return (lane >= j * HEAD) & (lane < (j + 1) * HEAD)


def _swa_heads(q, sink_ref, score_fn, mask_fn, pv_fn, o_ref):
    t = q.shape[0]

    def scores(g):
        qg = q[:, g * SWA_KV:(g + 1) * SWA_KV]
        rows = [jnp.where(_swa_head_mask(qg.shape, j), qg, 0.0).astype(bf16) for j in range(SWA_KV_HEADS)]
        return score_fn(jnp.concatenate(rows, axis=0))

    s_next = scores(0)
    for g in range(SWA_GROUP):
        s_grp = s_next
        if g + 1 < SWA_GROUP:
            s_next = scores(g + 1)
        o = None
        for j in range(SWA_KV_HEADS):
            s_cur = mask_fn([s[j * t:(j + 1) * t] for s in s_grp])
            sink = sink_ref[j * SWA_GROUP + g] * LOG2E
            m = sink
            for s in s_cur:
                m = jnp.maximum(jnp.max(s, axis=-1, keepdims=True), m)
            exps = [jnp.exp2(s - m) for s in s_cur]
            den = jnp.exp2(sink - m)
            for e in exps:
                den = den + jnp.sum(e, axis=-1, keepdims=True)
            oj = pv_fn(j, [e.astype(bf16) for e in exps]) * (1.0 / den)
            o = oj if j == 0 else o + oj
        o_ref[:, g * SWA_KV:(g + 1) * SWA_KV] = o.astype(o_ref.dtype)


def _swa_ctx_kernel(sink_ref, q_ref, k_ref, v_ref, qn_ref, kn_ref, o_ref, knew_ref, vnew_ref):
    q = _head_rms(q_ref[...].astype(f32), qn_ref[...]) * QK_SCALE
    k = _head_rms(k_ref[...], kn_ref[...])
    v = v_ref[...]
    knew_ref[...] = k
    vnew_ref[...] = v
    kb = k.astype(bf16)
    v_heads = [jnp.where(_swa_head_mask(v.shape, j), v, 0.0).astype(bf16) for j in range(SWA_KV_HEADS)]
    _swa_heads(q, sink_ref, lambda qm: [_dot_nt(qm, kb)], lambda s: s,
               lambda j, e: _dot(e[0], v_heads[j]), o_ref)


def _swa_lat_kernel(sink_ref, q_ref, k_ref, v_ref, ck_ref, cv_ref, qn_ref, kn_ref,
                    cq_ref, sq_ref, ckk_ref, skk_ref, o_ref, kp_scr, vp_scr, *, t_lat):
    n = pl.program_id(1)
    w = WINDOW

    @pl.when(n == 0)
    def _():
        zeros = jnp.zeros((w, SWA_KV), bf16)
        k = _rope(_head_rms(k_ref[...], kn_ref[...]), ckk_ref[...], skk_ref[...])
        kp_scr[0:w, :] = zeros
        kp_scr[w:w + t_lat, :] = k.astype(bf16)
        kp_scr[w + t_lat:, :] = zeros
        vp_scr[0:w, :] = zeros
        vp_scr[w:w + t_lat, :] = v_ref[...].astype(bf16)
        vp_scr[w + t_lat:, :] = zeros

    q = _rope(_head_rms(q_ref[...].astype(f32), qn_ref[...]), cq_ref[...], sq_ref[...]) * QK_SCALE
    start = pl.multiple_of(n * w, w)
    kband = kp_scr[pl.ds(start, 3 * w), :]
    vband = vp_scr[pl.ds(start, 3 * w), :]
    ckb = ck_ref[...].astype(bf16)
    cv = cv_ref[...]
    qi = lax.broadcasted_iota(jnp.int32, (w, 3 * w), 0)
    si = lax.broadcasted_iota(jnp.int32, (w, 3 * w), 1)
    rel = si - qi
    key_pos = (n - 1) * w + si
    valid = (rel >= 0) & (rel <= 2 * w) & (key_pos >= 0) & (key_pos < t_lat)
    cv_heads = [jnp.where(_swa_head_mask(cv.shape, j), cv, 0.0).astype(bf16) for j in range(SWA_KV_HEADS)]
    vb_heads = [jnp.where(_swa_head_mask(vband.shape, j), vband, jnp.zeros_like(vband))
                for j in range(SWA_KV_HEADS)]
    _swa_heads(q, sink_ref,
               lambda qm: [_dot_nt(qm, ckb), _dot_nt(qm, kband)],
               lambda s: [s[0], jnp.where(valid, s[1], NEG_INF)],
               lambda j, e: _dot(e[0], cv_heads[j]) + _dot(e[1], vb_heads[j]),
               o_ref)


def _swa_group_major_weights(wqkv, wo):
    d = D_MODEL
    wq = wqkv[:, :d].reshape(d, SWA_KV_HEADS, SWA_GROUP, HEAD).transpose(0, 2, 1, 3).reshape(d, d)
    wo = wo.reshape(SWA_KV_HEADS, SWA_GROUP, HEAD, d).transpose(1, 0, 2, 3).reshape(d, d)
    return jnp.concatenate([wq, wqkv[:, d:]], axis=1).astype(bf16), wo.astype(bf16)


def _swa_mixer(q, kv, cache_k, cache_v, j, p, rope_tab, lay):
    d = D_MODEL
    qn = jnp.tile(p['q_norm'], d // HEAD).reshape(1, d)
    kn = jnp.tile(p['k_norm'], SWA_KV_HEADS).reshape(1, SWA_KV)
    small = lambda shape: pl.BlockSpec(shape, lambda *_: (0,) * len(shape))
    smem = pl.BlockSpec(memory_space=pltpu.SMEM)
    t = lay.t_ctx
    kv_rows = pl.BlockSpec((t, SWA_KV), lambda b: (b, 0))
    o_ctx, k_new, v_new = pl.pallas_call(
        _swa_ctx_kernel,
        grid=(lay.b_ctx,),
        in_specs=[
            smem,
            pl.BlockSpec((t, d), lambda b: (b, 0)),
            kv_rows,
            pl.BlockSpec((t, SWA_KV), lambda b: (b, 1)),
            small((1, d)), small((1, SWA_KV)),
        ],
        out_specs=[pl.BlockSpec((t, d), lambda b: (b, 0)), kv_rows, kv_rows],
        out_shape=[jax.ShapeDtypeStruct((lay.n_ctx, d), bf16),
                   jax.ShapeDtypeStruct((lay.n_ctx, SWA_KV), f32),
                   jax.ShapeDtypeStruct((lay.n_ctx, SWA_KV), f32)],
        compiler_params=_params(("parallel",)),
        name="swa_ctx",
    )(p['sink'], q, kv, kv, qn, kn)

    tl = lay.t_lat
    w = WINDOW
    past = cache_k.shape[2]
    row0 = lay.n_ctx // tl
    q0 = lay.n_ctx // w
    nq = tl // w
    cos, sin = rope_tab
    ck = cache_k.reshape(lay.b_lat, cache_k.shape[1], past, SWA_KV)
    cv = cache_v.reshape(lay.b_lat, cache_v.shape[1], past, SWA_KV)
    o_lat = pl.pallas_call(
        functools.partial(_swa_lat_kernel, t_lat=tl),
        grid=(lay.b_lat, nq),
        in_specs=[
            smem,
            pl.BlockSpec((w, d), lambda b, i: (q0 + b * nq + i, 0)),
            pl.BlockSpec((tl, SWA_KV), lambda b, i: (row0 + b, 0)),
            pl.BlockSpec((tl, SWA_KV), lambda b, i: (row0 + b, 1)),
            pl.BlockSpec((None, None, past, SWA_KV), lambda b, i: (b, j, 0, 0)),
            pl.BlockSpec((None, None, past, SWA_KV), lambda b, i: (b, j, 0, 0)),
            small((1, d)), small((1, SWA_KV)),
            pl.BlockSpec((w, d), lambda b, i: (i, 0)),
            pl.BlockSpec((w, d), lambda b, i: (i, 0)),
            pl.BlockSpec((tl, SWA_KV), lambda b, i: (0, 0)),
            pl.BlockSpec((tl, SWA_KV), lambda b, i: (0, 0)),
        ],
        out_specs=pl.BlockSpec((w, d), lambda b, i: (b * nq + i, 0)),
        out_shape=jax.ShapeDtypeStruct((lay.n_lat, d), bf16),
        scratch_shapes=[pltpu.VMEM((tl + 2 * w, SWA_KV), bf16), pltpu.VMEM((tl + 2 * w, SWA_KV), bf16)],
        compiler_params=_params(("parallel", "arbitrary")),
        name="swa_lat",
    )(p['sink'], q, kv, kv, ck, cv, qn, kn, cos, sin, cos, sin)
    return o_ctx, o_lat, k_new, v_new


def _rwkv_proj_kernel(x_ref, xp_ref, xn_ref, sh_ref, sc_ref, mu_ref, wrkv_ref, w1_ref, w2_ref, w0_ref,
                      a1_ref, a2_ref, a0_ref, g1_ref, g2_ref,
                      r_ref, k_ref, v_ref, g_ref, a_ref, lw_ref, *, lay, tm):
    i = pl.program_id(0)
    r0 = i * tm
    in_ctx = r0 < lay.n_ctx
    pos = jnp.where(in_ctx, r0 % lay.t_ctx, (r0 - lay.n_ctx) % lay.t_lat)
    seq_len = jnp.where(in_ctx, lay.t_ctx, lay.t_lat)
    keep_prev = jnp.where(pos == 0, 0.0, 1.0)
    keep_next = jnp.where(pos + tm == seq_len, 0.0, 1.0)

    sh = sh_ref[...]
    sc = sc_ref[...]
    h = _adaln(x_ref[...], sh, sc)
    h_prev_row = _adaln(xp_ref[7:8, :], sh, sc) * keep_prev
    h_next_row = _adaln(xn_ref[0:1, :], sh, sc) * keep_next
    row = lax.broadcasted_iota(jnp.int32, h.shape, 0)
    prev = jnp.where(row == 0, h_prev_row, pltpu.roll(h, 1, 0))
    nxt = jnp.where(row == tm - 1, h_next_row, pltpu.roll(h, tm - 1, 0))
    xx = 0.5 * (prev + nxt) - h

    def mix(m):
        return (h + xx * mu_ref[m:m + 1, :]).astype(bf16)

    r_ref[...] = _dot(mix(0), wrkv_ref[0]).astype(r_ref.dtype)
    k_ref[...] = _dot(mix(1), wrkv_ref[1]).astype(k_ref.dtype)
    v_ref[...] = _dot(mix(2), wrkv_ref[2]).astype(v_ref.dtype)

    lane = lax.broadcasted_iota(jnp.int32, (tm, LANES), 1)
    first = lane < HEAD

    tw = jnp.tanh(_dot(mix(3), w1_ref[...]))
    la = _dot(mix(4), a1_ref[...])
    for z in range(2):
        sel = first if z == 0 else jnp.logical_not(first)
        w_lora = _dot(jnp.where(sel, tw, 0.0).astype(bf16), w2_ref[...])
        lw_ref[z] = -math.exp(-0.5) * jax.nn.sigmoid(w0_ref[z:z + 1, :] + w_lora)
        a_lora = _dot(jnp.where(sel, la, 0.0).astype(bf16), a2_ref[...])
        a_ref[z] = jax.nn.sigmoid(a0_ref[z:z + 1, :] + a_lora).astype(a_ref.dtype)

    g = _dot(jax.nn.sigmoid(_dot(mix(5), g1_ref[...])).astype(bf16), g2_ref[...])
    g_ref[...] = g.astype(g_ref.dtype)


def _rwkv_proj(x, mod, p, lay, tm=256):
    n, d = x.shape
    assert lay.t_ctx % tm == 0 and lay.t_lat % tm == 0
    nb8 = n // 8
    wrkv = p['wrkv'].astype(bf16)
    w1 = jnp.concatenate([p['w1'][0], p['w1'][1]], axis=1).astype(bf16)
    w2 = p['w2'].reshape(2 * p['w2'].shape[1], d).astype(bf16)
    a1 = jnp.concatenate([p['a1'][0], p['a1'][1]], axis=1).astype(bf16)
    a2 = p['a2'].reshape(2 * p['a2'].shape[1], d).astype(bf16)
    g1 = p['g1'].astype(bf16)
    g2 = p['g2'].astype(bf16)
    row_spec = pl.BlockSpec((tm, d), lambda i: (i, 0))
    z_spec = pl.BlockSpec((2, tm, d), lambda i: (0, i, 0))
    args = (x, x, x, mod, mod, p['mu'], wrkv, w1, w2, p['w0'], a1, a2, p['a0'], g1, g2)
    return pl.pallas_call(
        functools.partial(_rwkv_proj_kernel, lay=lay, tm=tm),
        grid=(n // tm,),
        in_specs=[
            row_spec,
            pl.BlockSpec((8, d), lambda i: (jnp.maximum(i * (tm // 8) - 1, 0), 0)),
            pl.BlockSpec((8, d), lambda i: (jnp.minimum((i + 1) * (tm // 8), nb8 - 1), 0)),
            _mod_spec(0, tm, lay),
            _mod_spec(1, tm, lay),
        ] + [_resident(a.shape) for a in args[5:]],
        out_specs=[row_spec] * 4 + [z_spec] * 2,
        out_shape=[jax.ShapeDtypeStruct((n, d), bf16)] * 4
        + [jax.ShapeDtypeStruct((2, n, d), bf16), jax.ShapeDtypeStruct((2, n, d), f32)],
        compiler_params=_params(("parallel",)),
        name="rwkv_proj",
    )(*args)


def _wkv_prep(r, k, v, a, lw, kk_gain, ka, forward):
    c_len = r.shape[0]
    r, k, v, a = (z.astype(f32) for z in (r, k, v, a))
    kkr = k * kk_gain
    kk = kkr * lax.rsqrt(jnp.maximum(_head_sum(kkr * kkr), 1e-24))
    beta = kk * a
    kd = k * (1.0 + (a - 1.0) * ka)
    row = lax.broadcasted_iota(jnp.int32, lw.shape, 0)
    cum = lw
    step = 1
    while step < c_len:
        if forward:
            cum = cum + jnp.where(row >= step, pltpu.roll(cum, step, 0), 0.0)
        else:
            cum = cum + jnp.where(row < c_len - step, pltpu.roll(cum, c_len - step, 0), 0.0)
        step *= 2
    total = cum[c_len - 1:c_len, :] if forward else cum[0:1, :]
    inv = jnp.exp(-cum)
    rem = jnp.exp(total - cum)
    return dict(a_t=-kk * jnp.exp(cum - lw), r_t=r * jnp.exp(cum), b_inv=beta * inv, k_inv=kd * inv,
                b_rem=beta * rem, k_rem=kd * rem, p_total=jnp.exp(total), v=v)


def _wkv_kernel(*refs, has_init):
    dir_refs = (refs[0:5], refs[5:10])
    kk_ref, ka_ref = refs[10:12]
    n_in = 13 if has_init else 12
    yf_ref, yb_ref, sout_ref, s_scr = refs[n_in:]
    n_par = yf_ref.shape[0]
    c_len = WKV_CHUNK
    gw = WKV_LANES
    hpg = gw // HEAD
    n_grp = yf_ref.shape[2] // gw
    c = pl.program_id(1)

    @pl.when(c == 0)
    def _():
        s_scr[...] = jnp.zeros_like(s_scr)
        if has_init:
            s0_ref = refs[n_in - 1]
            for p in range(n_par):
                for dr in range(2):
                    for h in range(n_grp * hpg):
                        o = (h % hpg) * HEAD
                        s_scr[p, dr, h // hpg, o:o + HEAD, o:o + HEAD] = s0_ref[p, dr, h]

    t_s = lax.broadcasted_iota(jnp.int32, (c_len, gw), 0)
    i_s = lax.broadcasted_iota(jnp.int32, (c_len, gw), 1) & (c_len - 1)
    strict = (i_s < t_s, i_s > t_s)
    incl = (i_s <= t_s, i_s >= t_s)
    eye = jnp.where(t_s == i_s, 1.0, 0.0)
    pair_block = (t_s >> 1) == (i_s >> 1)
    levels = []
    for lb in range(1, int(math.log2(c_len))):
        levels.append(((t_s >> (lb + 1)) == (i_s >> (lb + 1))) & ((t_s >> lb) != (i_s >> lb)))
    rb = lax.broadcasted_iota(jnp.int32, (hpg * c_len, gw), 0) // c_len
    cb = lax.broadcasted_iota(jnp.int32, (hpg * c_len, gw), 1) // HEAD
    same_head = rb == cb

    def bd(x):
        return jnp.where(same_head, jnp.concatenate([x] * hpg, axis=0), 0.0).astype(bf16)

    chains = []
    for p in range(n_par):
        for dr in range(2):
            op = _wkv_prep(*(ref[p] for ref in dir_refs[dr]), kk_ref[...], ka_ref[...], forward=(dr == 0))
            for pr in range(n_grp):
                chains.append((op, dr, (p, dr, pr), slice(pr * gw, (pr + 1) * gw),
                               (yf_ref if dr == 0 else yb_ref).at[p]))
    ids = range(len(chains))
    lhs = [jnp.concatenate([op['a_t'][:, cs], op['r_t'][:, cs]], axis=0).astype(bf16)
           for op, dr, si, cs, y_ref in chains]
    s0 = [s_scr[si] for op, dr, si, cs, y_ref in chains]
    sc = [_dot_nt(lhs[i], jnp.concatenate([bd(op['b_inv'][:, cs]), bd(op['k_inv'][:, cs]),
                                           s0[i].astype(bf16)], axis=0))
          for i, (op, dr, si, cs, y_ref) in enumerate(chains)]
    sc_b = [s[:, :gw] for s in sc]
    sc_k = [s[:, gw:2 * gw] for s in sc]
    x0 = [s[:, 2 * gw:] for s in sc]
    n_mat = [jnp.where(strict[dr], sc_b[i][:c_len], 0.0) for i, (op, dr, si, cs, y_ref) in enumerate(chains)]
    v_bd = [bd(op['v'][:, cs]) for op, dr, si, cs, y_ref in chains]
    x = [x0[i][:c_len] + _dot(jnp.where(strict[dr], sc_k[i][:c_len], 0.0).astype(bf16), v_bd[i])
         for i, (op, dr, si, cs, y_ref) in enumerate(chains)]
    t_inv = [eye + jnp.where(pair_block, n_mat[i], 0.0) for i in ids]
    for off_diag in levels:
        half = [_dot(t_inv[i].astype(bf16), bd(jnp.where(off_diag, n_mat[i], 0.0))) for i in ids]
        t_inv = [t_inv[i] + _dot(half[i].astype(bf16), bd(t_inv[i])) for i in ids]
    uv_bd = [jnp.concatenate([bd(_dot(t_inv[i].astype(bf16), bd(x[i]))), v_bd[i]], axis=0) for i in ids]
    for i, (op, dr, si, cs, y_ref) in enumerate(chains):
        rbk = jnp.concatenate([jnp.where(incl[dr], sc_b[i][c_len:], 0.0),
                               jnp.where(incl[dr], sc_k[i][c_len:], 0.0)], axis=1).astype(bf16)
        y_ref[:, cs] = (x0[i][c_len:] + _dot(rbk, uv_bd[i])).astype(y_ref.dtype)
    for i, (op, dr, si, cs, y_ref) in enumerate(chains):
        bk_rem = jnp.concatenate([bd(op['b_rem'][:, cs]), bd(op['k_rem'][:, cs])], axis=0)
        s_scr[si] = s0[i] * op['p_total'][:, cs] + _dot_tn(uv_bd[i], bk_rem)

    @pl.when(c == pl.num_programs(1) - 1)
    def _():
        for p in range(n_par):
            for dr in range(2):
                for h in range(n_grp * hpg):
                    o = (h % hpg) * HEAD
                    sout_ref[p, dr, h] = s_scr[p, dr, h // hpg, o:o + HEAD, o:o + HEAD]


def _wkv_scan(r, k, v, a2, lw2, p, s0, row0, n_seq, t_len, n_par=1):
    n, d = r.shape
    c_len = WKV_CHUNK
    nc = t_len // c_len
    n_grp = d // WKV_LANES
    has_init = s0 is not None
    if n_seq % n_par or (row0 // t_len) % n_par:
        n_par = 1
    assert n % t_len == 0 and row0 % t_len == 0
    u0 = row0 // t_len // n_par

    tok_f = pl.BlockSpec((n_par, c_len, d), lambda s, c: (u0 + s, c, 0))
    tok_b = pl.BlockSpec((n_par, c_len, d), lambda s, c: (u0 + s, nc - 1 - c, 0))
    z_f = pl.BlockSpec((None, n_par, c_len, d), lambda s, c: (0, u0 + s, c, 0))
    z_b = pl.BlockSpec((None, n_par, c_len, d), lambda s, c: (1, u0 + s, nc - 1 - c, 0))
    vec = pl.BlockSpec((1, d), lambda s, c: (0, 0))
    st = pl.BlockSpec((n_par, 2, RWKV_HEADS, HEAD, HEAD), lambda s, c: (s, 0, 0, 0, 0))
    r3, k3, v3 = (z.reshape(n // t_len, t_len, d) for z in (r, k, v))
    a4, lw4 = (z.reshape(2, n // t_len, t_len, d) for z in (a2, lw2))
    in_specs = [tok_f, tok_f, tok_f, z_f, z_f, tok_b, tok_b, tok_b, z_b, z_b, vec, vec]
    args = [r3, k3, v3, a4, lw4, r3, k3, v3, a4, lw4, p['k_k'].reshape(1, d), p['k_a'].reshape(1, d)]
    if has_init:
        in_specs.append(st)
        args.append(s0)
    y_shape = jax.ShapeDtypeStruct((n_seq, t_len, d), bf16)
    y_f, y_b, s_out = pl.pallas_call(
        functools.partial(_wkv_kernel, has_init=has_init),
        grid=(n_seq // n_par, nc),
        in_specs=in_specs,
        out_specs=[
            pl.BlockSpec((n_par, c_len, d), lambda s, c: (s, c, 0)),
            pl.BlockSpec((n_par, c_len, d), lambda s, c: (s, nc - 1 - c, 0)),
            st,
        ],
        out_shape=[y_shape, y_shape, jax.ShapeDtypeStruct((n_seq, 2, RWKV_HEADS, HEAD, HEAD), f32)],
        scratch_shapes=[pltpu.VMEM((n_par, 2, n_grp, WKV_LANES, WKV_LANES), f32)],
        compiler_params=_params(("parallel", "arbitrary")),
        name="wkv_scan",
    )(*args)
    return y_f.reshape(n_seq * t_len, d), y_b.reshape(n_seq * t_len, d), s_out


def _rwkv_post_kernel(x_ref, yfc_ref, ybc_ref, yfl_ref, ybl_ref, r_ref, k_ref, v_ref, g_ref, a_ref,
                      ka_ref, rk_ref, lnw_ref, lnb_ref, gate_ref, wo_ref, o_ref, *, nb_ctx):
    i = pl.program_id(0)
    y = (_pick_rows(i, nb_ctx, yfc_ref, yfl_ref).astype(f32)
         + _pick_rows(i, nb_ctx, ybc_ref, ybl_ref).astype(f32))
    mu = _head_sum(y) * (1.0 / HEAD)
    yc = y - mu
    var = _head_sum(yc * yc) * (1.0 / HEAD)
    yn = yc * lax.rsqrt(var + RWKV_GN_EPS) * lnw_ref[...] + lnb_ref[...]
    k = k_ref[...].astype(f32)
    ka = ka_ref[...]
    a_sum = a_ref[0].astype(f32) + a_ref[1].astype(f32)
    kd_sum = k * (2.0 + (a_sum - 2.0) * ka)
    bonus = _head_sum(r_ref[...].astype(f32) * kd_sum * rk_ref[...]) * v_ref[...].astype(f32)
    o = ((yn + bonus) * g_ref[...].astype(f32)).astype(bf16)
    o_ref[...] = x_ref[...] + gate_ref[...] * _dot(o, wo_ref[...])


def _rwkv_post_residual(x, mod, y_ctx, y_lat, r, k, v, g, a2, p, wo_bf16, lay, tm=512):
    n, d = r.shape
    tok = pl.BlockSpec((tm, d), lambda i: (i, 0))
    tok_z = pl.BlockSpec((2, tm, d), lambda i: (0, i, 0))
    vec = pl.BlockSpec((1, d), lambda i: (0, 0))
    yc_spec, yl_spec = _split_row_specs((tm, d), tm, lay)
    row = lambda a: a.reshape(1, d)
    return pl.pallas_call(
        functools.partial(_rwkv_post_kernel, nb_ctx=lay.n_ctx // tm),
        grid=(n // tm,),
        in_specs=[tok, yc_spec, yc_spec, yl_spec, yl_spec, tok, tok, tok, tok, tok_z, vec, vec, vec, vec,
                  _mod_spec(2, tm, lay), _resident((d, d))],
        out_specs=tok,
        out_shape=jax.ShapeDtypeStruct((n, d), f32),
        compiler_params=_params(("parallel",)),
        name="rwkv_post",
    )(x, y_ctx[0], y_ctx[1], y_lat[0], y_lat[1], r, k, v, g, a2,
      row(p['k_a']), row(p['r_k']), row(p['ln_w']), row(p['ln_b']), mod, wo_bf16)


def _rwkv_layer(x, mod, state0, p, wo_bf16, lay):
    r, k, v, g, a2, lw2 = _rwkv_proj(x, mod, p, lay)
    yf_c, yb_c, s_ctx = _wkv_scan(r, k, v, a2, lw2, p, None, 0, lay.b_ctx, lay.t_ctx)
    yf_l, yb_l, _ = _wkv_scan(r, k, v, a2, lw2, p, state0, lay.n_ctx, lay.b_lat, lay.t_lat)
    x = _rwkv_post_residual(x, mod, (yf_c, yb_c), (yf_l, yb_l), r, k, v, g, a2, p, wo_bf16, lay)
    return x, s_ctx


def _rope_table(n_tok, width):
    n_rows = n_tok // GRID_W
    rows = jnp.broadcast_to(jnp.arange(n_rows, dtype=f32)[:, None], (n_rows, GRID_W)).reshape(-1)
    cols = jnp.broadcast_to(jnp.arange(GRID_W, dtype=f32)[None, :], (n_rows, GRID_W)).reshape(-1)
    n_freq = HEAD // 4
    inv = ROPE_BASE ** (-jnp.arange(n_freq, dtype=f32) / n_freq)
    ang = jnp.concatenate([rows[:, None] * inv, cols[:, None] * inv], axis=-1)
    reps = width // (HEAD // 2)
    return jnp.tile(jnp.cos(ang), (1, reps)), jnp.tile(jnp.sin(ang), (1, reps))


def kernel(x_prompt, x_sample, cache_da_k, cache_da_v, state_rwkv, cache_swa_k, cache_swa_v, c, c_ctx, ada_w, ada_b, mlp_w1, mlp_w2, da_wqkv, da_q_norm, da_k_norm, da_lambda, da_subln, da_wo, rwkv_mu, rwkv_wrkv, rwkv_w0, rwkv_w1, rwkv_w2, rwkv_a0, rwkv_a1, rwkv_a2, rwkv_g1, rwkv_g2, rwkv_k_k, rwkv_k_a, rwkv_r_k, rwkv_ln_w, rwkv_ln_b, rwkv_wo, swa_wqkv, swa_q_norm, swa_k_norm, swa_sink, swa_wo):
    b_ctx, t_ctx, d = x_prompt.shape
    b_lat, t_lat, _ = x_sample.shape
    lay = Layout(b_ctx, t_ctx, b_lat, t_lat)
    depth = ada_w.shape[0]
    tm = 512
    assert d == D_MODEL and lay.n_ctx % t_lat == 0

    xs = (x_prompt.reshape(lay.n_ctx, d), x_sample.reshape(lay.n_lat, d))
    cond = jnp.concatenate([c_ctx[None, :], c, jnp.zeros((8 - 1 - b_lat, d), f32)], axis=0)
    mods = _modulation(cond, ada_w, ada_b)
    mods = mods[:, :1 + b_lat].reshape(depth, 1 + b_lat, 6, 1, d)

    rope_da = _rope_table(t_lat, LANES)
    rope_swa = _rope_table(t_lat, d)

    mlp_w1_b, mlp_w2_b = mlp_w1.astype(bf16), mlp_w2.astype(bf16)
    da_wqkv_b, da_wo_b = da_wqkv.astype(bf16), da_wo.astype(bf16)
    n_da = da_wqkv.shape[0]
    da_kv = None
    rwkv_s, swa_k, swa_v = [], [], []
    for i in range(depth):
        kind = i % N_MIXERS
        j = i // N_MIXERS
        mod = mods[i]
        w1 = LayerWeight(mlp_w1_b, i)
        w2 = LayerWeight(mlp_w2_b, i)
        if kind == 1:
            p = {'mu': rwkv_mu[j], 'wrkv': rwkv_wrkv[j], 'w0': rwkv_w0[j], 'w1': rwkv_w1[j],
                 'w2': rwkv_w2[j], 'a0': rwkv_a0[j], 'a1': rwkv_a1[j], 'a2': rwkv_a2[j],
                 'g1': rwkv_g1[j], 'g2': rwkv_g2[j], 'k_k': rwkv_k_k[j], 'k_a': rwkv_k_a[j],
                 'r_k': rwkv_r_k[j], 'ln_w': rwkv_ln_w[j], 'ln_b': rwkv_ln_b[j]}
            x = xs[0] if len(xs) == 1 else jnp.concatenate(xs, axis=0)
            x, s_new = _rwkv_layer(x, mod, state_rwkv[:, j], p, rwkv_wo[j].astype(bf16), lay)
            rwkv_s.append(s_new)
            xs = (_mlp(x, mod, w1, w2, lay),)
            continue
        if kind == 0:
            p = {'q_norm': da_q_norm[j], 'k_norm': da_k_norm[j], 'lam': da_lambda[j], 'subln': da_subln[j]}
            lam_init = 0.8 - 0.6 * math.exp(-0.3 * i)
            q, kv = _ln_matmul(xs, mod, LayerWeight(da_wqkv_b, j), lay)
            o_ctx, o_lat, *da_kv = _da_mixer(q, kv, cache_da_k, cache_da_v, j, n_da, da_kv, p, lam_init,
                                             rope_da, lay)
            wo = LayerWeight(da_wo_b, j)
        else:
            p = {'q_norm': swa_q_norm[j], 'k_norm': swa_k_norm[j], 'sink': swa_sink[j]}
            wqkv, wo = _swa_group_major_weights(swa_wqkv[j], swa_wo[j])
            q, kv = _ln_matmul(xs, mod, wqkv, lay)
            o_ctx, o_lat, k_new, v_new = _swa_mixer(q, kv, cache_swa_k, cache_swa_v, j, p, rope_swa, lay)
            swa_k.append(k_new.reshape(b_ctx, t_ctx, SWA_KV_HEADS, HEAD))
            swa_v.append(v_new.reshape(b_ctx, t_ctx, SWA_KV_HEADS, HEAD))
        if i == depth - 1 and len(xs) == 1:
            nb_ctx = lay.n_ctx // tm
            xs = (_out_proj_mlp(xs, (o_ctx,), mod, wo, w1, w2, lay, rows=(0, nb_ctx), tm=tm),
                  _out_proj_mlp(xs, (o_lat,), mod, wo, w1, w2, lay, rows=(nb_ctx, lay.n_lat // tm), tm=tm))
        else:
            xs = (_out_proj_mlp(xs, (o_ctx, o_lat), mod, wo, w1, w2, lay, tm=tm),)

    if len(xs) == 1:
        xs = (xs[0][:lay.n_ctx], xs[0][lay.n_ctx:])
    y_prompt = xs[0].reshape(b_ctx, t_ctx, d)
    y_sample = xs[1].reshape(b_lat, t_lat, d)
    new_da_k, new_da_v = (a.reshape(b_ctx, n_da, t_ctx, DA_HEADS, LANES) for a in da_kv)
    return (y_prompt, y_sample, new_da_k, new_da_v,
            jnp.stack(rwkv_s, axis=1), jnp.stack(swa_k, axis=1), jnp.stack(swa_v, axis=1))
```
